```python
import jax, jax.numpy as jnp
from jax import lax
import numpy as np

D_MODEL = 1024
BATCH = 8
SEQ = 4096
DEPTH = 2

PLE_DIM = 256
EPS = 1e-6
N_EVEN = (DEPTH + 1) // 2
N_ODD = DEPTH // 2

A_WIDTH = D_MODEL // 2
A_HEADS = 4
A_DK = A_WIDTH // A_HEADS
A_DV = A_WIDTH // A_HEADS
A_CHUNK = 32
B_WIDTH = D_MODEL // 2
CONV_W = 3
AB_IN = 4 * A_WIDTH + 3 * B_WIDTH
AB_SPLITS = (A_WIDTH, 2 * A_WIDTH, 3 * A_WIDTH, 4 * A_WIDTH,
             4 * A_WIDTH + B_WIDTH, 4 * A_WIDTH + 2 * B_WIDTH)
C_HEAD_DIM = 64
C_Q_HEADS = D_MODEL // C_HEAD_DIM
C_KV_HEADS = 2
C_GROUP = C_Q_HEADS // C_KV_HEADS
WINDOW = 128
C_BLOCK = WINDOW
C_IN = (C_Q_HEADS + 2 * C_KV_HEADS) * C_HEAD_DIM
N_EXPERTS = 16
N_GROUPS = 4
EXPERTS_PER_GROUP = N_EXPERTS // N_GROUPS
TOP_K = 2
D_EXPERT = 512

kernel_name = "hybrid_hgrn2_shortconv_swa_sink_grouped_moe"


def rms_norm(x, g):
    x32 = x.astype(jnp.float32)
    y = x32 * lax.rsqrt(jnp.mean(x32 * x32, axis=-1, keepdims=True) + EPS)
    return (y * g.astype(jnp.float32)).astype(x.dtype)


def hgrn2_chunkwise(q, k, v, log_f):
    bsz, t, h, _ = q.shape
    n = t // A_CHUNK

    def to_chunks(a):
        return a.reshape(bsz, n, A_CHUNK, h, a.shape[-1]).transpose(1, 0, 3, 2, 4)

    q, k, v, log_f = map(to_chunks, (q, k, v, log_f))
    b = jnp.cumsum(log_f, axis=3)
    b_last = b[:, :, :, -1, :]
    q_dec = q * jnp.exp(b)
    k_inv = k * jnp.exp(-b)
    k_tail = k * jnp.exp(b_last[:, :, :, None, :] - b)
    causal = jnp.tril(jnp.ones((A_CHUNK, A_CHUNK), jnp.float32))
    scores = jnp.einsum('nbhid,nbhjd->nbhij', q_dec, k_inv) * causal
    o_intra = jnp.einsum('nbhij,nbhjv->nbhiv', scores, v)

    def step(state, xs):
        q_c, k_c, v_c, decay = xs
        o_c = jnp.einsum('bhid,bhdv->bhiv', q_c, state)
        state = state * decay[..., None] + jnp.einsum('bhjd,bhjv->bhdv', k_c, v_c)
        return state, o_c

    s0 = jnp.zeros((bsz, h, q.shape[-1], v.shape[-1]), jnp.float32)
    _, o_inter = lax.scan(step, s0, (q_dec, k_tail, v, jnp.exp(b_last)))
    o = o_intra + o_inter
    return o.transpose(1, 0, 3, 2, 4).reshape(bsz, t, h, v.shape[-1])


def causal_depthwise_conv(y, w):
    return lax.conv_general_dilated(
        y, w[:, None, :].astype(y.dtype), window_strides=(1,),
        padding=[(CONV_W - 1, 0)], dimension_numbers=('NWC', 'WIO', 'NWC'),
        feature_group_count=y.shape[-1])


def hgrn2_shortconv_mixer(u, w_in, lb, out_norm, conv_w, w_out):
    bsz, t, _ = u.shape
    z = u @ w_in
    q, f, inp, g, gate_b, gate_c, h_b = jnp.split(z, AB_SPLITS, axis=-1)

    def heads(a):
        return a.reshape(bsz, t, A_HEADS, -1)

    forget = lb + (1.0 - lb) * jax.nn.sigmoid(f.astype(jnp.float32))
    o_a = hgrn2_chunkwise(heads(jax.nn.silu(q.astype(jnp.float32))),
                          heads(1.0 - forget),
                          heads(inp.astype(jnp.float32)),
                          heads(jnp.log(forget)))
    o_a = rms_norm(o_a, out_norm) * jax.nn.sigmoid(heads(g.astype(jnp.float32)))
    o_a = o_a.reshape(bsz, t, A_WIDTH).astype(u.dtype)
    o_b = gate_b * causal_depthwise_conv(gate_c * h_b, conv_w)
    return jnp.concatenate([o_a, o_b], axis=-1) @ w_out


def swa_sink_attention(u, w_in, b_in, sinks, w_out, b_out):
    bsz, t, _ = u.shape
    nb = t // C_BLOCK
    z = u @ w_in + b_in
    q, k, v = jnp.split(z, [C_Q_HEADS * C_HEAD_DIM, (C_Q_HEADS + C_KV_HEADS) * C_HEAD_DIM], axis=-1)
    q = q.reshape(bsz, nb, C_BLOCK, C_KV_HEADS, C_GROUP, C_HEAD_DIM)
    k = k.reshape(bsz, nb, C_BLOCK, C_KV_HEADS, C_HEAD_DIM)
    v = v.reshape(bsz, nb, C_BLOCK, C_KV_HEADS, C_HEAD_DIM)

    def with_prev(a):
        prev = jnp.concatenate([jnp.zeros_like(a[:, :1]), a[:, :-1]], axis=1)
        return jnp.concatenate([prev, a], axis=2)

    kk, vv = with_prev(k), with_prev(v)
    scale = C_HEAD_DIM ** -0.5
    s = jnp.einsum('bnqkgd,bnskd->bkgnqs', q, kk).astype(jnp.float32) * scale
    qi = jnp.arange(C_BLOCK)[:, None] + C_BLOCK
    si = jnp.arange(2 * C_BLOCK)[None, :]
    rel = qi - si
    band = (rel >= 0) & (rel < WINDOW)
    real = (jnp.arange(nb)[:, None, None] > 0) | (si[None] >= C_BLOCK)
    mask = band[None] & real
    s = jnp.where(mask, s, -jnp.inf)
    sink = sinks.astype(jnp.float32).reshape(C_KV_HEADS, C_GROUP)[None, :, :, None, None, None]
    m = jnp.maximum(jnp.max(s, axis=-1, keepdims=True), sink)
    e = jnp.exp(s - m)
    probs = e / (jnp.sum(e, axis=-1, keepdims=True) + jnp.exp(sink - m))
    o = jnp.einsum('bkgnqs,bnskd->bnqkgd', probs.astype(vv.dtype), vv)
    return o.reshape(bsz, t, D_MODEL) @ w_out + b_out


def grouped_moe(v, w_router, w_gate, w_up, w_down):
    bsz, t, d = v.shape
    xt = v.reshape(-1, d)
    scores = jax.nn.softmax((xt @ w_router).astype(jnp.float32), axis=-1)
    grouped = scores.reshape(-1, N_GROUPS, EXPERTS_PER_GROUP)
    group_score = jnp.sum(lax.top_k(grouped, TOP_K)[0], axis=-1)
    sel = jnp.argmax(group_score, axis=-1)
    in_group = jnp.take_along_axis(grouped, sel[:, None, None], axis=1)[:, 0]
    vals, idx = lax.top_k(in_group, TOP_K)
    expert_ids = sel[:, None] * EXPERTS_PER_GROUP + idx
    weights = vals / jnp.sum(vals, axis=-1, keepdims=True)
    combine = jnp.einsum('nk,nke->ne', weights,
                         jax.nn.one_hot(expert_ids, N_EXPERTS, dtype=jnp.float32)).astype(v.dtype)
    y = jnp.zeros_like(xt)
    for e in range(N_EXPERTS):
        hidden = jax.nn.silu(xt @ w_gate[e]) * (xt @ w_up[e])
        y = y + combine[:, e:e + 1] * (hidden @ w_down[e])
    return y.reshape(bsz, t, d)


def setup_inputs(seed: int = 0) -> dict:
    key = jax.random.key(seed)
    ks = jax.random.split(key, 24)

    def nrm(k, shape, scale):
        return jax.random.normal(k, shape, jnp.float32) * scale

    def gain(k, shape):
        return 1.0 + 0.05 * jax.random.normal(k, shape, jnp.float32)

    out_scale = (2 * DEPTH) ** -0.5
    return {
        "x": nrm(ks[0], (BATCH, SEQ, D_MODEL), 1.0),
        "p": nrm(ks[1], (DEPTH, BATCH, SEQ, PLE_DIM), 1.0),
        "norm_mix": gain(ks[2], (DEPTH, D_MODEL)),
        "norm_ffn": gain(ks[3], (DEPTH, D_MODEL)),
        "norm_ple": gain(ks[4], (DEPTH, D_MODEL)),
        "norm_final": gain(ks[5], (D_MODEL,)),
        "w_in_ab": nrm(ks[6], (N_EVEN, D_MODEL, AB_IN), D_MODEL ** -0.5),
        "hgrn_lb_logits": nrm(ks[7], (DEPTH + 1, A_WIDTH), 0.1),
        "hgrn_out_norm": gain(ks[8], (N_EVEN, A_DV)),
        "conv_w": nrm(ks[9], (N_EVEN, CONV_W, B_WIDTH), CONV_W ** -0.5),
        "w_out_ab": nrm(ks[10], (N_EVEN, D_MODEL, D_MODEL), D_MODEL ** -0.5 * out_scale),
        "w_in_c": nrm(ks[11], (N_ODD, D_MODEL, C_IN), D_MODEL ** -0.5),
        "b_in_c": nrm(ks[12], (N_ODD, C_IN), 0.02),
        "sinks": nrm(ks[13], (N_ODD, C_Q_HEADS), 1.0),
        "w_out_c": nrm(ks[14], (N_ODD, D_MODEL, D_MODEL), D_MODEL ** -0.5 * out_scale),
        "b_out_c": nrm(ks[15], (N_ODD, D_MODEL), 0.02),
        "w_router": nrm(ks[16], (D_MODEL, N_EXPERTS), D_MODEL ** -0.5),
        "w_gate_e": nrm(ks[17], (DEPTH, N_EXPERTS, D_MODEL, D_EXPERT), D_MODEL ** -0.5),
        "w_up_e": nrm(ks[18], (DEPTH, N_EXPERTS, D_MODEL, D_EXPERT), D_MODEL ** -0.5),
        "w_down_e": nrm(ks[19], (DEPTH, N_EXPERTS, D_EXPERT, D_MODEL), D_EXPERT ** -0.5 * out_scale),
        "w_ple_gate": nrm(ks[20], (DEPTH, D_MODEL, D_MODEL), D_MODEL ** -0.5),
        "w_ple_proj": nrm(ks[21], (DEPTH, PLE_DIM, D_MODEL), PLE_DIM ** -0.5 * out_scale),
    }


def reference(x, p, norm_mix, norm_ffn, norm_ple, norm_final, w_in_ab, hgrn_lb_logits,
              hgrn_out_norm, conv_w, w_out_ab, w_in_c, b_in_c, sinks, w_out_c, b_out_c,
              w_router, w_gate_e, w_up_e, w_down_e, w_ple_gate, w_ple_proj):
    lower_bounds = jnp.cumsum(jax.nn.softmax(hgrn_lb_logits.astype(jnp.float32), axis=0), axis=0)
    h = x
    for i in range(DEPTH):
        u = rms_norm(h, norm_mix[i])
        j = i // 2
        if i % 2 == 0:
            mix = hgrn2_shortconv_mixer(u, w_in_ab[j], lower_bounds[i], hgrn_out_norm[j],
                                        conv_w[j], w_out_ab[j])
        else:
            mix = swa_sink_attention(u, w_in_c[j], b_in_c[j], sinks[j], w_out_c[j], b_out_c[j])
        h = h + mix
        h = h + grouped_moe(rms_norm(h, norm_ffn[i]), w_router, w_gate_e[i], w_up_e[i], w_down_e[i])
        ple_gate = jax.nn.sigmoid(rms_norm(h, norm_ple[i]) @ w_ple_gate[i])
        h = h + ple_gate * (p[i] @ w_ple_proj[i])
    return rms_norm(h, norm_final)
```

```python
import functools

import jax
import jax.numpy as jnp
from jax import lax
from jax.experimental import pallas as pl
from jax.experimental.pallas import tpu as pltpu

F32 = jnp.float32
BF16 = jnp.bfloat16
MXU = jnp.bfloat16
I32 = jnp.int32
U32 = jnp.uint32

D_MODEL = 1024
EPS = 1e-6
A_WIDTH = 512
A_HEADS = 4
A_DK = 128
A_CHUNK = 32
AB_IN = 3584
HEAD_DIM = 64
Q_HEADS = 16
KV_HEADS = 2
GROUP = 8
WINDOW = 128
C_IN = 1280
N_EXPERTS = 16
N_GROUPS = 4
PER_GROUP = 4
D_EXPERT = 512
PLE_DIM = 256

PAIRS = ((0, 1), (0, 2), (0, 3), (1, 2), (1, 3), (2, 3))
N_CLASS = N_GROUPS * len(PAIRS)
CLASS_PAD = 32
X_WORDS = D_MODEL // 2
ROW_WORDS = X_WORDS + 128

L0_BLOCK = 256
L1_BLOCK = 512
MOE_TILE = 256
MOVE_BLOCK = 512
VMEM_LIMIT = 56 * 1024 * 1024

NT_DIMS = (((1,), (1,)), ((), ()))
TN_DIMS = (((0,), (0,)), ((), ()))


def _rms(x, g):
    ms = jnp.mean(x * x, axis=-1, keepdims=True)
    return x * lax.rsqrt(ms + EPS) * g


def _sigmoid(x):
    return 1.0 / (1.0 + jnp.exp(-x))


def _route_tail(h1, first, gffn_ref, wrt_ref, xext_ref, cls_ref, rank_ref, cnt_ref, run_scr,
                pack_scr):
    tb = h1.shape[0]

    @pl.when(first)
    def _():
        run_scr[...] = jnp.zeros_like(run_scr)

    xn = _rms(h1, gffn_ref[...])
    logits = lax.dot_general(wrt_ref[...], xn, NT_DIMS, precision=lax.Precision.HIGHEST,
                             preferred_element_type=F32)
    mx = jnp.max(logits, axis=0, keepdims=True)
    ex = jnp.exp(logits - mx)
    sc = ex / jnp.sum(ex, axis=0, keepdims=True)
    rows = [sc[i:i + 1, :] for i in range(N_EXPERTS)]

    gscore = []
    for g in range(N_GROUPS):
        v = rows[PER_GROUP * g:PER_GROUP * (g + 1)]
        best = v[0] + v[1]
        for (i, j) in PAIRS[1:]:
            best = jnp.maximum(best, v[i] + v[j])
        gscore.append(best)
    gmax = jnp.maximum(jnp.maximum(gscore[0], gscore[1]), jnp.maximum(gscore[2], gscore[3]))
    sel = jnp.where(gscore[0] >= gmax, 0, jnp.where(gscore[1] >= gmax, 1,
                                                    jnp.where(gscore[2] >= gmax, 2, 3))).astype(I32)
    v = [jnp.where(sel == 0, rows[i], jnp.where(sel == 1, rows[4 + i],
                                                jnp.where(sel == 2, rows[8 + i], rows[12 + i])))
         for i in range(PER_GROUP)]
    chosen = []
    for i in range(PER_GROUP):
        r = jnp.zeros_like(sel)
        for j in range(PER_GROUP):
            if j == i:
                continue
            ahead = (v[j] >= v[i]) if j < i else (v[j] > v[i])
            r = r + jnp.where(ahead, 1, 0).astype(I32)
        chosen.append(r < 2)
    vsum = jnp.zeros_like(v[0])
    code = jnp.zeros_like(sel)
    for i in range(PER_GROUP):
        vsum = vsum + jnp.where(chosen[i], v[i], 0.0)
        code = code + jnp.where(chosen[i], 1 << i, 0).astype(I32)
    wts = [jnp.where(chosen[i], v[i] / vsum, 0.0) for i in range(PER_GROUP)]
    pair = jnp.where(code == 3, 0, jnp.where(code == 5, 1, jnp.where(code == 9, 2,
                     jnp.where(code == 6, 3, jnp.where(code == 10, 4, 5))))).astype(I32)
    cls = sel * len(PAIRS) + pair

    erow = lax.broadcasted_iota(I32, (128, tb), 0)
    comb_t = jnp.zeros((128, tb), F32)
    for i in range(PER_GROUP):
        comb_t = comb_t + jnp.where(erow == sel * PER_GROUP + i, wts[i], 0.0)
    comb = comb_t.T

    comb_hi = comb.astype(BF16).astype(F32)
    halves = [(xn[:, j * 128:(j + 1) * 128], xn[:, X_WORDS + j * 128:X_WORDS + (j + 1) * 128])
              for j in range(X_WORDS // 128)] + [(comb_hi, comb - comb_hi)]
    for j, (first_half, second_half) in enumerate(halves):
        pack_scr[j, pl.ds(0, tb, stride=2), :] = first_half
        pack_scr[j, pl.ds(1, tb, stride=2), :] = second_half
        xext_ref[:, j * 128:(j + 1) * 128] = pltpu.bitcast(pack_scr[j].astype(BF16), U32)

    crow = lax.broadcasted_iota(I32, (CLASS_PAD, tb), 0)
    onehot = jnp.where(crow == cls, 1.0, 0.0)
    ii = lax.broadcasted_iota(I32, (tb, tb), 0)
    jj = lax.broadcasted_iota(I32, (tb, tb), 1)
    upper = jnp.where(ii < jj, 1.0, 0.0).astype(MXU)
    prefix = jnp.dot(onehot.astype(MXU), upper, preferred_element_type=F32)
    run = run_scr[...]
    rank = jnp.sum(onehot * (prefix + run[:, 0:1]), axis=0, keepdims=True)
    run_new = run + jnp.sum(onehot, axis=1, keepdims=True)
    run_scr[...] = run_new
    cnt_ref[...] = run_new
    cls_ref[0] = cls
    rank_ref[0] = rank.astype(I32)


def _l0_kernel(h_ref, gmix_ref, win_ref, lb_ref, onorm_ref, convw_ref, wout_ref, gffn_ref, wrt_ref,
               h1_ref, xext_ref, cls_ref, rank_ref, cnt_ref,
               z_scr, cat_scr, st_scr, carry_scr, run_scr, pack_scr):
    b = pl.program_id(0)
    t = pl.program_id(1)
    tb = h_ref.shape[1]

    @pl.when(t == 0)
    def _():
        st_scr[...] = jnp.zeros_like(st_scr)
        carry_scr[...] = jnp.zeros_like(carry_scr)

    h = h_ref[0]
    u = _rms(h, gmix_ref[...]).astype(MXU)
    z_scr[...] = jnp.dot(u, win_ref[...], preferred_element_type=F32)

    gate_b = z_scr[:, 4 * A_WIDTH:5 * A_WIDTH]
    cb = z_scr[:, 5 * A_WIDTH:6 * A_WIDTH] * z_scr[:, 6 * A_WIDTH:7 * A_WIDTH]
    row = lax.broadcasted_iota(I32, cb.shape, 0)
    prev1 = carry_scr[7:8, :]
    prev2 = carry_scr[6:7, :]
    m1 = jnp.where(row == 0, prev1, pltpu.roll(cb, 1, axis=0))
    m2 = jnp.where(row == 0, prev2, jnp.where(row == 1, prev1, pltpu.roll(cb, 2, axis=0)))
    cw = convw_ref[...]
    cat_scr[:, A_WIDTH:] = gate_b * (cw[0:1, :] * m2 + cw[1:2, :] * m1 + cw[2:3, :] * cb)
    carry_scr[...] = cb[tb - 8:tb, :]

    lb = lb_ref[...]
    onorm = onorm_ref[...]
    ci = lax.broadcasted_iota(I32, (A_CHUNK, A_CHUNK), 0)
    cj = lax.broadcasted_iota(I32, (A_CHUNK, A_CHUNK), 1)
    tri = jnp.where(ci >= cj, 1.0, 0.0)

    def chunk(c, carry):
        r0 = pl.multiple_of(c * A_CHUNK, A_CHUNK)
        rs = pl.ds(r0, A_CHUNK)
        zq = z_scr[rs, 0:A_WIDTH]
        zf = z_scr[rs, A_WIDTH:2 * A_WIDTH]
        zi = z_scr[rs, 2 * A_WIDTH:3 * A_WIDTH]
        zg = z_scr[rs, 3 * A_WIDTH:4 * A_WIDTH]
        f = lb + (1.0 - lb) * _sigmoid(zf)
        bcum = jnp.dot(tri, jnp.log(f), precision=lax.Precision.HIGHEST,
                       preferred_element_type=F32)
        blast = bcum[A_CHUNK - 1:A_CHUNK, :]
        k = 1.0 - f
        q_dec = (zq * _sigmoid(zq)) * jnp.exp(bcum)
        k_inv = k * jnp.exp(-bcum)
        k_tail = k * jnp.exp(blast - bcum)
        decay = jnp.exp(blast)
        gate = _sigmoid(zg)
        for hd in range(A_HEADS):
            sl = slice(hd * A_DK, (hd + 1) * A_DK)
            qd = q_dec[:, sl].astype(MXU)
            vv = zi[:, sl].astype(MXU)
            scores = lax.dot_general(qd, k_inv[:, sl].astype(MXU), NT_DIMS,
                                     preferred_element_type=F32) * tri
            st = st_scr[hd]
            o = (jnp.dot(scores.astype(MXU), vv, preferred_element_type=F32)
                 + lax.dot_general(qd, st.astype(MXU), NT_DIMS, preferred_element_type=F32))
            st_scr[hd] = st * decay[:, sl] + lax.dot_general(
                vv, k_tail[:, sl].astype(MXU), TN_DIMS, preferred_element_type=F32)
            cat_scr[rs, sl] = _rms(o, onorm) * gate[:, sl]
        return carry

    lax.fori_loop(0, tb // A_CHUNK, chunk, 0)

    h1 = h + jnp.dot(cat_scr[...].astype(MXU), wout_ref[...], preferred_element_type=F32)
    h1_ref[0] = h1
    _route_tail(h1, (b == 0) & (t == 0), gffn_ref, wrt_ref, xext_ref, cls_ref, rank_ref, cnt_ref,
                run_scr, pack_scr)


def _l1_kernel(sinks_ref, h_ref, gmix_ref, win_ref, bin_ref, wout_ref, bout_ref, gffn_ref, wrt_ref,
               h1_ref, xext_ref, cls_ref, rank_ref, cnt_ref,
               q_scr, kv_scr, att_scr, run_scr, pack_scr):
    b = pl.program_id(0)
    t = pl.program_id(1)
    tb = h_ref.shape[1]
    qw = Q_HEADS * HEAD_DIM
    kvw = KV_HEADS * HEAD_DIM

    @pl.when(t == 0)
    def _():
        kv_scr[0:WINDOW, :] = jnp.zeros((WINDOW, 2 * kvw), kv_scr.dtype)

    h = h_ref[0]
    u = _rms(h, gmix_ref[...]).astype(MXU)
    z = jnp.dot(u, win_ref[...], preferred_element_type=F32) + bin_ref[...]
    q_scr[...] = z[:, :qw]
    kv_scr[WINDOW:, :] = z[:, qw:].astype(kv_scr.dtype)

    qi = lax.broadcasted_iota(I32, (WINDOW, 2 * WINDOW), 0)
    si = lax.broadcasted_iota(I32, (WINDOW, 2 * WINDOW), 1)
    scale = HEAD_DIM ** -0.5

    def qblock(n, carry):
        r0 = pl.multiple_of(n * WINDOW, WINDOW)
        rs = pl.ds(r0, WINDOW)
        first_key = jnp.where((t > 0) | (n > 0), 0, WINDOW)
        mask = (si > jnp.maximum(qi, first_key - 1)) & (si <= qi + WINDOW)
        kv = kv_scr[pl.ds(r0, 2 * WINDOW), :]
        for kk in range(KV_HEADS):
            keys = kv[:, kk * HEAD_DIM:(kk + 1) * HEAD_DIM]
            vals = kv[:, kvw + kk * HEAD_DIM:kvw + (kk + 1) * HEAD_DIM]
            for g in range(GROUP):
                hidx = kk * GROUP + g
                hs = slice(hidx * HEAD_DIM, (hidx + 1) * HEAD_DIM)
                q = q_scr[rs, hs].astype(MXU)
                s = lax.dot_general(q, keys, NT_DIMS, preferred_element_type=F32) * scale
                s = jnp.where(mask, s, -jnp.inf)
                sink = sinks_ref[hidx]
                m = jnp.maximum(jnp.max(s, axis=-1, keepdims=True), sink)
                e = jnp.exp(s - m)
                probs = e / (jnp.sum(e, axis=-1, keepdims=True) + jnp.exp(sink - m))
                att_scr[rs, hs] = jnp.dot(probs.astype(MXU), vals, preferred_element_type=F32)
        return carry

    lax.fori_loop(0, tb // WINDOW, qblock, 0)
    kv_scr[0:WINDOW, :] = kv_scr[tb:tb + WINDOW, :]

    h1 = (h + jnp.dot(att_scr[...].astype(MXU), wout_ref[...], preferred_element_type=F32)
          + bout_ref[...])
    h1_ref[0] = h1
    _route_tail(h1, (b == 0) & (t == 0), gffn_ref, wrt_ref, xext_ref, cls_ref, rank_ref, cnt_ref,
                run_scr, pack_scr)


def _row_copy(src, src_row, dst, dst_row, sem):
    return pltpu.make_async_copy(src.at[pl.ds(src_row, 1)], dst.at[pl.ds(dst_row, 1)], sem)


def _scatter_kernel(pos_ref, x_ref, init_ref, out_ref, sem):
    del init_ref
    rows = x_ref.shape[0]
    base = pl.program_id(0) * rows

    def issue(r, carry):
        _row_copy(x_ref, r, out_ref, pos_ref[base + r], sem).start()
        return carry

    lax.fori_loop(0, rows, issue, 0)

    def drain(r, carry):
        _row_copy(x_ref, r, out_ref, pos_ref[base + r], sem).wait()
        return carry

    lax.fori_loop(0, rows, drain, 0)


def _moe_kernel(e1_ref, e2_ref, nused_ref, x_ref, wg1_ref, wu1_ref, wd1_ref, wg2_ref, wu2_ref,
                wd2_ref, y_ref, unpack_scr):
    i = pl.program_id(0)
    tm = x_ref.shape[0]

    @pl.when(i < nused_ref[0])
    def _():
        w = x_ref[...]
        nchunk = ROW_WORDS // 128
        for j in range(nchunk):
            unpack_scr[j] = pltpu.bitcast(w[:, j * 128:(j + 1) * 128], BF16).astype(F32)
        xa = [unpack_scr[j, pl.ds(0, tm, stride=2), :] for j in range(nchunk)]
        xb = [unpack_scr[j, pl.ds(1, tm, stride=2), :] for j in range(nchunk)]
        x = jnp.concatenate(xa[:-1] + xb[:-1], axis=1).astype(MXU)
        comb = xa[-1] + xb[-1]
        lane = lax.broadcasted_iota(I32, comb.shape, 1)

        def expert(e, wg_ref, wu_ref, wd_ref):
            cw = jnp.sum(jnp.where(lane == e, comb, 0.0), axis=1, keepdims=True)
            gate = jnp.dot(x, wg_ref[0], preferred_element_type=F32)
            up = jnp.dot(x, wu_ref[0], preferred_element_type=F32)
            hidden = (gate * _sigmoid(gate)) * up
            return cw * jnp.dot(hidden.astype(MXU), wd_ref[0], preferred_element_type=F32)

        y_ref[...] = (expert(e1_ref[i], wg1_ref, wu1_ref, wd1_ref)
                      + expert(e2_ref[i], wg2_ref, wu2_ref, wd2_ref))

    @pl.when(i >= nused_ref[0])
    def _():
        y_ref[...] = jnp.zeros_like(y_ref)


def _ple_kernel(pos_ref, h1_ref, p_ref, y_ref, gple_ref, wgate_ref, wproj_ref, gfin_ref,
                out_ref, ybuf, sem, *, final):
    rows = h1_ref.shape[0]
    base = pl.program_id(0) * rows

    def issue(r, carry):
        _row_copy(y_ref, pos_ref[base + r], ybuf, r, sem).start()
        return carry

    lax.fori_loop(0, rows, issue, 0)
    proj = jnp.dot(p_ref[...].astype(MXU), wproj_ref[...], preferred_element_type=F32)

    def drain(r, carry):
        _row_copy(y_ref, pos_ref[base + r], ybuf, r, sem).wait()
        return carry

    lax.fori_loop(0, rows, drain, 0)
    h2 = h1_ref[...] + ybuf[...]
    gate = _sigmoid(jnp.dot(_rms(h2, gple_ref[...]).astype(MXU), wgate_ref[...],
                            preferred_element_type=F32))
    h3 = h2 + gate * proj
    if final:
        h3 = _rms(h3, gfin_ref[...])
    out_ref[...] = h3


def _const_spec(shape):
    nd = len(shape)
    return pl.BlockSpec(shape, lambda *_: (0,) * nd)


def _route_out(bsz, seq, tb):
    nt = seq // tb
    n = bsz * seq
    shapes = [
        jax.ShapeDtypeStruct((bsz, seq, D_MODEL), F32),
        jax.ShapeDtypeStruct((n, ROW_WORDS), U32),
        jax.ShapeDtypeStruct((bsz * nt, 1, tb), I32),
        jax.ShapeDtypeStruct((bsz * nt, 1, tb), I32),
        jax.ShapeDtypeStruct((CLASS_PAD, 128), F32),
    ]
    specs = [
        pl.BlockSpec((1, tb, D_MODEL), lambda b, t, *_: (b, t, 0)),
        pl.BlockSpec((tb, ROW_WORDS), lambda b, t, *_: (b * nt + t, 0)),
        pl.BlockSpec((1, 1, tb), lambda b, t, *_: (b * nt + t, 0, 0)),
        pl.BlockSpec((1, 1, tb), lambda b, t, *_: (b * nt + t, 0, 0)),
        pl.BlockSpec((CLASS_PAD, 128), lambda b, t, *_: (0, 0)),
    ]
    return shapes, specs


def _params():
    return pltpu.CompilerParams(dimension_semantics=("arbitrary", "arbitrary"),
                                vmem_limit_bytes=VMEM_LIMIT)


def _layer0_mixer(h, gmix, win, lb, onorm, convw, wout, gffn, wrt):
    bsz, seq, _ = h.shape
    tb = L0_BLOCK
    assert seq % tb == 0
    shapes, ospecs = _route_out(bsz, seq, tb)
    return pl.pallas_call(
        _l0_kernel,
        grid=(bsz, seq // tb),
        in_specs=[
            pl.BlockSpec((1, tb, D_MODEL), lambda b, t: (b, t, 0)),
            _const_spec((1, D_MODEL)),
            _const_spec((D_MODEL, AB_IN)),
            _const_spec((1, A_WIDTH)),
            _const_spec((1, A_DK)),
            _const_spec((3, A_WIDTH)),
            _const_spec((D_MODEL, D_MODEL)),
            _const_spec((1, D_MODEL)),
            _const_spec((N_EXPERTS, D_MODEL)),
        ],
        out_specs=ospecs,
        out_shape=shapes,
        scratch_shapes=[
            pltpu.VMEM((tb, AB_IN), F32),
            pltpu.VMEM((tb, D_MODEL), F32),
            pltpu.VMEM((A_HEADS, A_DK, A_DK), F32),
            pltpu.VMEM((8, A_WIDTH), F32),
            pltpu.VMEM((CLASS_PAD, 128), F32),
            pltpu.VMEM((ROW_WORDS // 128, 2 * tb, 128), F32),
        ],
        compiler_params=_params(),
        name="layer0_mixer",
    )(h, gmix, win, lb, onorm, convw, wout, gffn, wrt)


def _layer1_mixer(h, sinks, gmix, win, bin_, wout, bout, gffn, wrt):
    bsz, seq, _ = h.shape
    tb = L1_BLOCK
    assert seq % tb == 0
    shapes, ospecs = _route_out(bsz, seq, tb)
    kvw = 2 * KV_HEADS * HEAD_DIM
    grid_spec = pltpu.PrefetchScalarGridSpec(
        num_scalar_prefetch=1,
        grid=(bsz, seq // tb),
        in_specs=[
            pl.BlockSpec((1, tb, D_MODEL), lambda b, t, *_: (b, t, 0)),
            _const_spec((1, D_MODEL)),
            _const_spec((D_MODEL, C_IN)),
            _const_spec((1, C_IN)),
            _const_spec((D_MODEL, D_MODEL)),
            _const_spec((1, D_MODEL)),
            _const_spec((1, D_MODEL)),
            _const_spec((N_EXPERTS, D_MODEL)),
        ],
        out_specs=ospecs,
        scratch_shapes=[
            pltpu.VMEM((tb, D_MODEL), F32),
            pltpu.VMEM((tb + WINDOW, kvw), MXU),
            pltpu.VMEM((tb, D_MODEL), F32),
            pltpu.VMEM((CLASS_PAD, 128), F32),
            pltpu.VMEM((ROW_WORDS // 128, 2 * tb, 128), F32),
        ],
    )
    return pl.pallas_call(
        _l1_kernel, grid_spec=grid_spec, out_shape=shapes, compiler_params=_params(),
        name="layer1_mixer",
    )(sinks, h, gmix, win, bin_, wout, bout, gffn, wrt)


def _dispatch(pos, xext, n_rows):
    n = xext.shape[0]
    blk = MOVE_BLOCK
    assert n % blk == 0
    grid_spec = pltpu.PrefetchScalarGridSpec(
        num_scalar_prefetch=1,
        grid=(n // blk,),
        in_specs=[
            pl.BlockSpec((blk, ROW_WORDS), lambda i, *_: (i, 0)),
            pl.BlockSpec(memory_space=pl.ANY),
        ],
        out_specs=pl.BlockSpec(memory_space=pl.ANY),
        scratch_shapes=[pltpu.SemaphoreType.DMA(())],
    )
    init = jnp.zeros((n_rows, ROW_WORDS), U32)
    return pl.pallas_call(
        _scatter_kernel, grid_spec=grid_spec,
        out_shape=jax.ShapeDtypeStruct((n_rows, ROW_WORDS), U32),
        input_output_aliases={2: 0},
        compiler_params=pltpu.CompilerParams(dimension_semantics=("arbitrary",),
                                             vmem_limit_bytes=VMEM_LIMIT),
        name="moe_dispatch",
    )(pos, xext, init)


def _moe(e1, e2, nused, xs, wg, wu, wd):
    n_rows = xs.shape[0]
    tm = MOE_TILE
    n_tiles = n_rows // tm

    def wspec(shape, which):
        return pl.BlockSpec((1,) + shape, lambda i, e1, e2, nu: ((e1, e2)[which][i], 0, 0))

    grid_spec = pltpu.PrefetchScalarGridSpec(
        num_scalar_prefetch=3,
        grid=(n_tiles,),
        in_specs=[
            pl.BlockSpec((tm, ROW_WORDS), lambda i, *_: (i, 0)),
            wspec((D_MODEL, D_EXPERT), 0), wspec((D_MODEL, D_EXPERT), 0), wspec((D_EXPERT, D_MODEL), 0),
            wspec((D_MODEL, D_EXPERT), 1), wspec((D_MODEL, D_EXPERT), 1), wspec((D_EXPERT, D_MODEL), 1),
        ],
        out_specs=pl.BlockSpec((tm, D_MODEL), lambda i, *_: (i, 0)),
        scratch_shapes=[pltpu.VMEM((ROW_WORDS // 128, 2 * tm, 128), F32)],
    )
    return pl.pallas_call(
        _moe_kernel, grid_spec=grid_spec,
        out_shape=jax.ShapeDtypeStruct((n_rows, D_MODEL), F32),
        compiler_params=pltpu.CompilerParams(dimension_semantics=("arbitrary",),
                                             vmem_limit_bytes=VMEM_LIMIT),
        name="moe_experts",
    )(e1, e2, nused, xs, wg, wu, wd, wg, wu, wd)


def _combine_ple(pos, h1, p, ys, gple, wgate, wproj, gfin, final):
    n = h1.shape[0]
    blk = MOVE_BLOCK
    assert n % blk == 0
    grid_spec = pltpu.PrefetchScalarGridSpec(
        num_scalar_prefetch=1,
        grid=(n // blk,),
        in_specs=[
            pl.BlockSpec((blk, D_MODEL), lambda i, *_: (i, 0)),
            pl.BlockSpec((blk, PLE_DIM), lambda i, *_: (i, 0)),
            pl.BlockSpec(memory_space=pl.ANY),
            _const_spec((1, D_MODEL)),
            _const_spec((D_MODEL, D_MODEL)),
            _const_spec((PLE_DIM, D_MODEL)),
            _const_spec((1, D_MODEL)),
        ],
        out_specs=pl.BlockSpec((blk, D_MODEL), lambda i, *_: (i, 0)),
        scratch_shapes=[pltpu.VMEM((blk, D_MODEL), F32), pltpu.SemaphoreType.DMA(())],
    )
    return pl.pallas_call(
        functools.partial(_ple_kernel, final=final), grid_spec=grid_spec,
        out_shape=jax.ShapeDtypeStruct((n, D_MODEL), F32),
        compiler_params=pltpu.CompilerParams(dimension_semantics=("arbitrary",),
                                             vmem_limit_bytes=VMEM_LIMIT),
        name="combine_ple",
    )(pos, h1, p, ys, gple, wgate, wproj, gfin)


def _routing_plan(cls, rank, counts, n_tokens):
    tm = MOE_TILE
    n_tiles = n_tokens // tm + N_CLASS
    cnt = counts[:N_CLASS, 0].astype(I32)
    tiles_per_class = (cnt + tm - 1) // tm
    tile_end = jnp.cumsum(tiles_per_class)
    class_row0 = (tile_end - tiles_per_class) * tm
    pos = jnp.take(class_row0, cls.reshape(-1)) + rank.reshape(-1)
    nused = tile_end[-1]
    tidx = jnp.minimum(jnp.arange(n_tiles, dtype=I32), nused - 1)
    tcls = jnp.sum((tidx[:, None] >= tile_end[None, :]).astype(I32), axis=1)
    tcls = jnp.minimum(tcls, N_CLASS - 1)
    lo = jnp.array([a for a, _ in PAIRS], I32)
    hi = jnp.array([b for _, b in PAIRS], I32)
    grp = tcls // len(PAIRS)
    e1 = grp * PER_GROUP + jnp.take(lo, tcls % len(PAIRS))
    e2 = grp * PER_GROUP + jnp.take(hi, tcls % len(PAIRS))
    return pos.astype(I32), e1.astype(I32), e2.astype(I32), nused.reshape(1).astype(I32), n_tiles * tm


def _ffn_and_ple(h1, xext, cls, rank, counts, p_i, wg, wu, wd, gple, wgate, wproj, gfin, final):
    bsz, seq, _ = h1.shape
    n = bsz * seq
    pos, e1, e2, nused, n_rows = _routing_plan(cls, rank, counts, n)
    xs = _dispatch(pos, xext, n_rows)
    ys = _moe(e1, e2, nused, xs, wg, wu, wd)
    out = _combine_ple(pos, h1.reshape(n, D_MODEL), p_i.reshape(n, PLE_DIM), ys, gple, wgate, wproj,
                       gfin, final)
    return out.reshape(bsz, seq, D_MODEL)


def kernel(x, p, norm_mix, norm_ffn, norm_ple, norm_final, w_in_ab, hgrn_lb_logits, hgrn_out_norm, conv_w, w_out_ab, w_in_c, b_in_c, sinks, w_out_c, b_out_c, w_router, w_gate_e, w_up_e, w_down_e, w_ple_gate, w_ple_proj):
    depth = p.shape[0]
    lower_bounds = jnp.cumsum(jax.nn.softmax(hgrn_lb_logits.astype(F32), axis=0), axis=0)
    wrt = w_router.T.astype(F32)
    gfin = norm_final.reshape(1, D_MODEL)
    h = x
    for i in range(depth):
        j = i // 2
        gmix = norm_mix[i].reshape(1, D_MODEL)
        gffn = norm_ffn[i].reshape(1, D_MODEL)
        if i % 2 == 0:
            h1, xext, cls, rank, counts = _layer0_mixer(
                h, gmix, w_in_ab[j].astype(MXU), lower_bounds[i].reshape(1, A_WIDTH),
                hgrn_out_norm[j].reshape(1, A_DK), conv_w[j], w_out_ab[j].astype(MXU), gffn, wrt)
        else:
            h1, xext, cls, rank, counts = _layer1_mixer(
                h, sinks[j].astype(F32), gmix, w_in_c[j].astype(MXU), b_in_c[j].reshape(1, C_IN),
                w_out_c[j].astype(MXU), b_out_c[j].reshape(1, D_MODEL), gffn, wrt)
        h = _ffn_and_ple(
            h1, xext, cls, rank, counts, p[i],
            w_gate_e[i].astype(MXU), w_up_e[i].astype(MXU), w_down_e[i].astype(MXU),
            norm_ple[i].reshape(1, D_MODEL), w_ple_gate[i].astype(MXU), w_ple_proj[i].astype(MXU),
            gfin, final=(i == depth - 1))
    return h
```

```python
import functools

import jax
import jax.numpy as jnp
from jax import lax
from jax.experimental import pallas as pl
from jax.experimental.pallas import tpu as pltpu

F32 = jnp.float32
BF16 = jnp.bfloat16
MXU = jnp.bfloat16
I32 = jnp.int32
U32 = jnp.uint32

D_MODEL = 1024
EPS = 1e-6
A_WIDTH = 512
A_HEADS = 4
A_DK = 128
A_CHUNK = 32
AB_IN = 3584
HEAD_DIM = 64
Q_HEADS = 16
KV_HEADS = 2
GROUP = 8
WINDOW = 128
C_IN = 1280
N_EXPERTS = 16
N_GROUPS = 4
PER_GROUP = 4
D_EXPERT = 512
PLE_DIM = 256

PAIRS = ((0, 1), (0, 2), (0, 3), (1, 2), (1, 3), (2, 3))
N_CLASS = N_GROUPS * len(PAIRS)
CLASS_PAD = 32
X_WORDS = D_MODEL // 2
ROW_WORDS = X_WORDS + 128
ROW_TILES = ROW_WORDS // 128

L0_BLOCK = 256
L1_BLOCK = 512
MOE_TILE = 256
MOVE_BLOCK = 512
VMEM_LIMIT = 56 * 1024 * 1024

NT_DIMS = (((1,), (1,)), ((), ()))
TN_DIMS = (((0,), (0,)), ((), ()))


def _rms(x, g):
    ms = jnp.mean(x * x, axis=-1, keepdims=True)
    return x * lax.rsqrt(ms + EPS) * g


def _sigmoid(x):
    return 1.0 / (1.0 + jnp.exp(-x))


def _route_tail(h1, first, gffn_ref, wr_ref, xext_ref, cls_ref, rank_ref, cnt_ref, run_scr,
                pack_scr):
    tb = h1.shape[0]

    @pl.when(first)
    def _():
        run_scr[...] = jnp.zeros_like(run_scr)

    xn = _rms(h1, gffn_ref[...])
    x_hi = xn.astype(BF16)
    x_lo = (xn - x_hi.astype(F32)).astype(BF16)
    prod = (jnp.dot(x_hi, wr_ref[...], preferred_element_type=F32)
            + jnp.dot(x_lo, wr_ref[...], preferred_element_type=F32)).T
    logits = prod[0:N_EXPERTS, :] + prod[N_EXPERTS:2 * N_EXPERTS, :]
    mx = jnp.max(logits, axis=0, keepdims=True)
    ex = jnp.exp(logits - mx)
    sc = ex / jnp.sum(ex, axis=0, keepdims=True)
    rows = [sc[i:i + 1, :] for i in range(N_EXPERTS)]

    gscore = []
    for g in range(N_GROUPS):
        v = rows[PER_GROUP * g:PER_GROUP * (g + 1)]
        best = v[0] + v[1]
        for (i, j) in PAIRS[1:]:
            best = jnp.maximum(best, v[i] + v[j])
        gscore.append(best)
    gmax = jnp.maximum(jnp.maximum(gscore[0], gscore[1]), jnp.maximum(gscore[2], gscore[3]))
    sel = jnp.where(gscore[0] >= gmax, 0, jnp.where(gscore[1] >= gmax, 1,
                                                    jnp.where(gscore[2] >= gmax, 2, 3))).astype(I32)
    v = [jnp.where(sel == 0, rows[i], jnp.where(sel == 1, rows[4 + i],
                                                jnp.where(sel == 2, rows[8 + i], rows[12 + i])))
         for i in range(PER_GROUP)]
    chosen = []
    for i in range(PER_GROUP):
        r = jnp.zeros_like(sel)
        for j in range(PER_GROUP):
            if j == i:
                continue
            ahead = (v[j] >= v[i]) if j < i else (v[j] > v[i])
            r = r + jnp.where(ahead, 1, 0).astype(I32)
        chosen.append(r < 2)
    vsum = jnp.zeros_like(v[0])
    code = jnp.zeros_like(sel)
    for i in range(PER_GROUP):
        vsum = vsum + jnp.where(chosen[i], v[i], 0.0)
        code = code + jnp.where(chosen[i], 1 << i, 0).astype(I32)
    wts = [jnp.where(chosen[i], v[i] / vsum, 0.0) for i in range(PER_GROUP)]
    pair = jnp.where(code == 3, 0, jnp.where(code == 5, 1, jnp.where(code == 9, 2,
                     jnp.where(code == 6, 3, jnp.where(code == 10, 4, 5))))).astype(I32)
    cls = sel * len(PAIRS) + pair

    erow = lax.broadcasted_iota(I32, (128, tb), 0)
    comb_t = jnp.zeros((128, tb), F32)
    for i in range(PER_GROUP):
        comb_t = comb_t + jnp.where(erow == sel * PER_GROUP + i, wts[i], 0.0)
    comb = comb_t.T

    comb_hi = comb.astype(BF16).astype(F32)
    halves = [(xn[:, j * 128:(j + 1) * 128], xn[:, X_WORDS + j * 128:X_WORDS + (j + 1) * 128])
              for j in range(X_WORDS // 128)] + [(comb_hi, comb - comb_hi)]
    for j, (first_half, second_half) in enumerate(halves):
        pack_scr[j, pl.ds(0, tb, stride=2), :] = first_half
        pack_scr[j, pl.ds(1, tb, stride=2), :] = second_half
        xext_ref[:, j * 128:(j + 1) * 128] = pltpu.bitcast(pack_scr[j].astype(BF16), U32)

    crow = lax.broadcasted_iota(I32, (CLASS_PAD, tb), 0)
    onehot = jnp.where(crow == cls, 1.0, 0.0)
    ii = lax.broadcasted_iota(I32, (tb, tb), 0)
    jj = lax.broadcasted_iota(I32, (tb, tb), 1)
    upper = jnp.where(ii < jj, 1.0, 0.0).astype(MXU)
    prefix = jnp.dot(onehot.astype(MXU), upper, preferred_element_type=F32)
    run = run_scr[...]
    rank = jnp.sum(onehot * (prefix + run[:, 0:1]), axis=0, keepdims=True)
    run_new = run + jnp.sum(onehot, axis=1, keepdims=True)
    run_scr[...] = run_new
    cnt_ref[...] = run_new
    cls_ref[0] = cls
    rank_ref[0] = rank.astype(I32)


def _l0_kernel(h_ref, gmix_ref, win_ref, lb_ref, onorm_ref, convw_ref, wout_ref, gffn_ref, wr_ref,
               h1_ref, xext_ref, cls_ref, rank_ref, cnt_ref,
               z_scr, cat_scr, st_scr, carry_scr, run_scr, pack_scr):
    b = pl.program_id(0)
    t = pl.program_id(1)
    tb = h_ref.shape[1]

    @pl.when(t == 0)
    def _():
        st_scr[...] = jnp.zeros_like(st_scr)
        carry_scr[...] = jnp.zeros_like(carry_scr)

    h = h_ref[0]
    u = _rms(h, gmix_ref[...]).astype(MXU)
    z_scr[...] = jnp.dot(u, win_ref[...], preferred_element_type=F32)

    gate_b = z_scr[:, 4 * A_WIDTH:5 * A_WIDTH]
    cb = z_scr[:, 5 * A_WIDTH:6 * A_WIDTH] * z_scr[:, 6 * A_WIDTH:7 * A_WIDTH]
    row = lax.broadcasted_iota(I32, cb.shape, 0)
    prev1 = carry_scr[7:8, :]
    prev2 = carry_scr[6:7, :]
    m1 = jnp.where(row == 0, prev1, pltpu.roll(cb, 1, axis=0))
    m2 = jnp.where(row == 0, prev2, jnp.where(row == 1, prev1, pltpu.roll(cb, 2, axis=0)))
    cw = convw_ref[...]
    cat_scr[:, A_WIDTH:] = gate_b * (cw[0:1, :] * m2 + cw[1:2, :] * m1 + cw[2:3, :] * cb)
    carry_scr[...] = cb[tb - 8:tb, :]

    lb = lb_ref[...]
    onorm = onorm_ref[...]
    ci = lax.broadcasted_iota(I32, (A_CHUNK, A_CHUNK), 0)
    cj = lax.broadcasted_iota(I32, (A_CHUNK, A_CHUNK), 1)
    tri = jnp.where(ci >= cj, 1.0, 0.0)

    def chunk(c, carry):
        r0 = pl.multiple_of(c * A_CHUNK, A_CHUNK)
        rs = pl.ds(r0, A_CHUNK)
        zq = z_scr[rs, 0:A_WIDTH]
        zf = z_scr[rs, A_WIDTH:2 * A_WIDTH]
        zi = z_scr[rs, 2 * A_WIDTH:3 * A_WIDTH]
        zg = z_scr[rs, 3 * A_WIDTH:4 * A_WIDTH]
        f = lb + (1.0 - lb) * _sigmoid(zf)
        bcum = jnp.dot(tri, jnp.log(f), precision=lax.Precision.HIGHEST,
                       preferred_element_type=F32)
        blast = bcum[A_CHUNK - 1:A_CHUNK, :]
        k = 1.0 - f
        q_dec = (zq * _sigmoid(zq)) * jnp.exp(bcum)
        k_inv = k * jnp.exp(-bcum)
        k_tail = k * jnp.exp(blast - bcum)
        decay = jnp.exp(blast)
        gate = _sigmoid(zg)
        for hd in range(A_HEADS):
            sl = slice(hd * A_DK, (hd + 1) * A_DK)
            qd = q_dec[:, sl].astype(MXU)
            vv = zi[:, sl].astype(MXU)
            scores = lax.dot_general(qd, k_inv[:, sl].astype(MXU), NT_DIMS,
                                     preferred_element_type=F32) * tri
            st = st_scr[hd]
            o = (jnp.dot(scores.astype(MXU), vv, preferred_element_type=F32)
                 + lax.dot_general(qd, st.astype(MXU), NT_DIMS, preferred_element_type=F32))
            st_scr[hd] = st * decay[:, sl] + lax.dot_general(
                vv, k_tail[:, sl].astype(MXU), TN_DIMS, preferred_element_type=F32)
            cat_scr[rs, sl] = _rms(o, onorm) * gate[:, sl]
        return carry

    lax.fori_loop(0, tb // A_CHUNK, chunk, 0, unroll=2)

    h1 = h + jnp.dot(cat_scr[...].astype(MXU), wout_ref[...], preferred_element_type=F32)
    h1_ref[0] = h1
    _route_tail(h1, (b == 0) & (t == 0), gffn_ref, wr_ref, xext_ref, cls_ref, rank_ref, cnt_ref,
                run_scr, pack_scr)


def _l1_kernel(sinks_ref, h_ref, gmix_ref, win_ref, bin_ref, wout_ref, bout_ref, gffn_ref, wr_ref,
               h1_ref, xext_ref, cls_ref, rank_ref, cnt_ref,
               q_scr, kv_scr, att_scr, run_scr, pack_scr):
    b = pl.program_id(0)
    t = pl.program_id(1)
    tb = h_ref.shape[1]
    qw = Q_HEADS * HEAD_DIM
    kvw = KV_HEADS * HEAD_DIM

    @pl.when(t == 0)
    def _():
        kv_scr[0:WINDOW, :] = jnp.zeros((WINDOW, 2 * kvw), kv_scr.dtype)

    h = h_ref[0]
    u = _rms(h, gmix_ref[...]).astype(MXU)
    z = jnp.dot(u, win_ref[...], preferred_element_type=F32) + bin_ref[...]
    q_scr[...] = z[:, :qw]
    kv_scr[WINDOW:, :] = z[:, qw:].astype(kv_scr.dtype)

    qi = lax.broadcasted_iota(I32, (WINDOW, 2 * WINDOW), 0)
    si = lax.broadcasted_iota(I32, (WINDOW, 2 * WINDOW), 1)
    scale = HEAD_DIM ** -0.5

    def qblock(n, carry):
        r0 = pl.multiple_of(n * WINDOW, WINDOW)
        rs = pl.ds(r0, WINDOW)
        first_key = jnp.where((t > 0) | (n > 0), 0, WINDOW)
        mask = (si > jnp.maximum(qi, first_key - 1)) & (si <= qi + WINDOW)
        kv = kv_scr[pl.ds(r0, 2 * WINDOW), :]
        for kk in range(KV_HEADS):
            keys = kv[:, kk * HEAD_DIM:(kk + 1) * HEAD_DIM]
            vals = kv[:, kvw + kk * HEAD_DIM:kvw + (kk + 1) * HEAD_DIM]
            for g in range(GROUP):
                hidx = kk * GROUP + g
                hs = slice(hidx * HEAD_DIM, (hidx + 1) * HEAD_DIM)
                q = q_scr[rs, hs].astype(MXU)
                s = lax.dot_general(q, keys, NT_DIMS, preferred_element_type=F32) * scale
                s = jnp.where(mask, s, -jnp.inf)
                sink = sinks_ref[hidx]
                m = jnp.maximum(jnp.max(s, axis=-1, keepdims=True), sink)
                e = jnp.exp(s - m)
                probs = e / (jnp.sum(e, axis=-1, keepdims=True) + jnp.exp(sink - m))
                att_scr[rs, hs] = jnp.dot(probs.astype(MXU), vals, preferred_element_type=F32)
        return carry

    lax.fori_loop(0, tb // WINDOW, qblock, 0)
    kv_scr[0:WINDOW, :] = kv_scr[tb:tb + WINDOW, :]

    h1 = (h + jnp.dot(att_scr[...].astype(MXU), wout_ref[...], preferred_element_type=F32)
          + bout_ref[...])
    h1_ref[0] = h1
    _route_tail(h1, (b == 0) & (t == 0), gffn_ref, wr_ref, xext_ref, cls_ref, rank_ref, cnt_ref,
                run_scr, pack_scr)


def _row_copy(src, src_row, dst, dst_row, sem):
    return pltpu.make_async_copy(src.at[pl.ds(src_row, 1)], dst.at[pl.ds(dst_row, 1)], sem)


ISSUE_UNROLL = 8


def _scatter_kernel(pos_ref, x_ref, init_ref, out_ref, sem):
    del init_ref
    rows = x_ref.shape[0]
    base = pl.program_id(0) * rows

    def issue(g, carry):
        r0 = pl.multiple_of(g * ISSUE_UNROLL, ISSUE_UNROLL)
        for k in range(ISSUE_UNROLL):
            _row_copy(x_ref, r0 + k, out_ref, pos_ref[base + r0 + k], sem).start()
        return carry

    lax.fori_loop(0, rows // ISSUE_UNROLL, issue, 0)
    pltpu.make_async_copy(x_ref, out_ref.at[pl.ds(0, rows)], sem).wait()


def _moe_kernel(e1_ref, e2_ref, nused_ref, x_ref, wg1_ref, wu1_ref, wd1_ref, wg2_ref, wu2_ref,
                wd2_ref, y_ref, unpack_scr):
    i = pl.program_id(0)
    tm = x_ref.shape[0]

    @pl.when(i < nused_ref[0])
    def _():
        nchunk = ROW_TILES
        for j in range(nchunk):
            unpack_scr[j] = pltpu.bitcast(x_ref[:, j * 128:(j + 1) * 128], BF16).astype(F32)
        xa = [unpack_scr[j, pl.ds(0, tm, stride=2), :] for j in range(nchunk)]
        xb = [unpack_scr[j, pl.ds(1, tm, stride=2), :] for j in range(nchunk)]
        x = jnp.concatenate(xa[:-1] + xb[:-1], axis=1).astype(MXU)
        comb = xa[-1] + xb[-1]
        lane = lax.broadcasted_iota(I32, comb.shape, 1)

        def expert(e, wg_ref, wu_ref, wd_ref):
            cw = jnp.sum(jnp.where(lane == e, comb, 0.0), axis=1, keepdims=True)
            gate = jnp.dot(x, wg_ref[0], preferred_element_type=F32)
            up = jnp.dot(x, wu_ref[0], preferred_element_type=F32)
            hidden = (gate * _sigmoid(gate)) * up
            return cw * jnp.dot(hidden.astype(MXU), wd_ref[0], preferred_element_type=F32)

        y_ref[...] = (expert(e1_ref[i], wg1_ref, wu1_ref, wd1_ref)
                      + expert(e2_ref[i], wg2_ref, wu2_ref, wd2_ref))

    @pl.when(i >= nused_ref[0])
    def _():
        y_ref[...] = jnp.zeros_like(y_ref)


def _ple_kernel(pos_ref, h1_ref, p_ref, y_ref, gple_ref, wgate_ref, wproj_ref, gfin_ref,
                out_ref, ybuf, sem, *, final):
    rows = h1_ref.shape[0]
    i = pl.program_id(0)
    nsteps = pl.num_programs(0)
    slot = i % 2

    def gather(step, buf):
        def issue(g, carry):
            r0 = pl.multiple_of(g * ISSUE_UNROLL, ISSUE_UNROLL)
            for k in range(ISSUE_UNROLL):
                _row_copy(y_ref, pos_ref[step * rows + r0 + k], ybuf.at[buf], r0 + k,
                          sem.at[buf]).start()
            return carry

        lax.fori_loop(0, rows // ISSUE_UNROLL, issue, 0)

    @pl.when(i == 0)
    def _():
        gather(0, 0)

    @pl.when(i + 1 < nsteps)
    def _():
        gather(i + 1, 1 - slot)

    proj = jnp.dot(p_ref[...].astype(MXU), wproj_ref[...], preferred_element_type=F32)
    pltpu.make_async_copy(y_ref.at[pl.ds(0, rows)], ybuf.at[slot], sem.at[slot]).wait()
    h2 = h1_ref[...] + ybuf[slot]
    gate = _sigmoid(jnp.dot(_rms(h2, gple_ref[...]).astype(MXU), wgate_ref[...],
                            preferred_element_type=F32))
    h3 = h2 + gate * proj
    if final:
        h3 = _rms(h3, gfin_ref[...])
    out_ref[...] = h3


def _const_spec(shape):
    nd = len(shape)
    return pl.BlockSpec(shape, lambda *_: (0,) * nd)


def _route_out(bsz, seq, tb):
    nt = seq // tb
    n = bsz * seq
    shapes = [
        jax.ShapeDtypeStruct((bsz, seq, D_MODEL), F32),
        jax.ShapeDtypeStruct((n, ROW_WORDS), U32),
        jax.ShapeDtypeStruct((bsz * nt, 1, tb), I32),
        jax.ShapeDtypeStruct((bsz * nt, 1, tb), I32),
        jax.ShapeDtypeStruct((CLASS_PAD, 128), F32),
    ]
    specs = [
        pl.BlockSpec((1, tb, D_MODEL), lambda b, t, *_: (b, t, 0)),
        pl.BlockSpec((tb, ROW_WORDS), lambda b, t, *_: (b * nt + t, 0)),
        pl.BlockSpec((1, 1, tb), lambda b, t, *_: (b * nt + t, 0, 0)),
        pl.BlockSpec((1, 1, tb), lambda b, t, *_: (b * nt + t, 0, 0)),
        pl.BlockSpec((CLASS_PAD, 128), lambda b, t, *_: (0, 0)),
    ]
    return shapes, specs


def _params():
    return pltpu.CompilerParams(dimension_semantics=("arbitrary", "arbitrary"),
                                vmem_limit_bytes=VMEM_LIMIT)


def _layer0_mixer(h, gmix, win, lb, onorm, convw, wout, gffn, wrt):
    bsz, seq, _ = h.shape
    tb = L0_BLOCK
    assert seq % tb == 0
    shapes, ospecs = _route_out(bsz, seq, tb)
    return pl.pallas_call(
        _l0_kernel,
        grid=(bsz, seq // tb),
        in_specs=[
            pl.BlockSpec((1, tb, D_MODEL), lambda b, t: (b, t, 0)),
            _const_spec((1, D_MODEL)),
            _const_spec((D_MODEL, AB_IN)),
            _const_spec((1, A_WIDTH)),
            _const_spec((1, A_DK)),
            _const_spec((3, A_WIDTH)),
            _const_spec((D_MODEL, D_MODEL)),
            _const_spec((1, D_MODEL)),
            _const_spec((D_MODEL, 128)),
        ],
        out_specs=ospecs,
        out_shape=shapes,
        scratch_shapes=[
            pltpu.VMEM((tb, AB_IN), F32),
            pltpu.VMEM((tb, D_MODEL), F32),
            pltpu.VMEM((A_HEADS, A_DK, A_DK), F32),
            pltpu.VMEM((8, A_WIDTH), F32),
            pltpu.VMEM((CLASS_PAD, 128), F32),
            pltpu.VMEM((ROW_TILES, 2 * tb, 128), F32),
        ],
        compiler_params=_params(),
        name="layer0_mixer",
    )(h, gmix, win, lb, onorm, convw, wout, gffn, wrt)


def _layer1_mixer(h, sinks, gmix, win, bin_, wout, bout, gffn, wrt):
    bsz, seq, _ = h.shape
    tb = L1_BLOCK
    assert seq % tb == 0
    shapes, ospecs = _route_out(bsz, seq, tb)
    kvw = 2 * KV_HEADS * HEAD_DIM
    grid_spec = pltpu.PrefetchScalarGridSpec(
        num_scalar_prefetch=1,
        grid=(bsz, seq // tb),
        in_specs=[
            pl.BlockSpec((1, tb, D_MODEL), lambda b, t, *_: (b, t, 0)),
            _const_spec((1, D_MODEL)),
            _const_spec((D_MODEL, C_IN)),
            _const_spec((1, C_IN)),
            _const_spec((D_MODEL, D_MODEL)),
            _const_spec((1, D_MODEL)),
            _const_spec((1, D_MODEL)),
            _const_spec((D_MODEL, 128)),
        ],
        out_specs=ospecs,
        scratch_shapes=[
            pltpu.VMEM((tb, D_MODEL), F32),
            pltpu.VMEM((tb + WINDOW, kvw), MXU),
            pltpu.VMEM((tb, D_MODEL), F32),
            pltpu.VMEM((CLASS_PAD, 128), F32),
            pltpu.VMEM((ROW_TILES, 2 * tb, 128), F32),
        ],
    )
    return pl.pallas_call(
        _l1_kernel, grid_spec=grid_spec, out_shape=shapes, compiler_params=_params(),
        name="layer1_mixer",
    )(sinks, h, gmix, win, bin_, wout, bout, gffn, wrt)


def _dispatch(pos, xext, n_rows):
    n = xext.shape[0]
    blk = MOVE_BLOCK
    assert n % blk == 0
    grid_spec = pltpu.PrefetchScalarGridSpec(
        num_scalar_prefetch=1,
        grid=(n // blk,),
        in_specs=[
            pl.BlockSpec((blk, ROW_WORDS), lambda i, *_: (i, 0)),
            pl.BlockSpec(memory_space=pl.ANY),
        ],
        out_specs=pl.BlockSpec(memory_space=pl.ANY),
        scratch_shapes=[pltpu.SemaphoreType.DMA(())],
    )
    init = jnp.zeros((n_rows, ROW_WORDS), U32)
    return pl.pallas_call(
        _scatter_kernel, grid_spec=grid_spec,
        out_shape=jax.ShapeDtypeStruct((n_rows, ROW_WORDS), U32),
        input_output_aliases={2: 0},
        compiler_params=pltpu.CompilerParams(dimension_semantics=("arbitrary",),
                                             vmem_limit_bytes=VMEM_LIMIT),
        name="moe_dispatch",
    )(pos, xext, init)


def _moe(e1, e2, nused, xs, wg, wu, wd):
    n_rows = xs.shape[0]
    tm = MOE_TILE
    n_tiles = n_rows // tm

    def wspec(shape, which):
        return pl.BlockSpec((1,) + shape, lambda i, e1, e2, nu: ((e1, e2)[which][i], 0, 0))

    grid_spec = pltpu.PrefetchScalarGridSpec(
        num_scalar_prefetch=3,
        grid=(n_tiles,),
        in_specs=[
            pl.BlockSpec((tm, ROW_WORDS), lambda i, *_: (i, 0)),
            wspec((D_MODEL, D_EXPERT), 0), wspec((D_MODEL, D_EXPERT), 0), wspec((D_EXPERT, D_MODEL), 0),
            wspec((D_MODEL, D_EXPERT), 1), wspec((D_MODEL, D_EXPERT), 1), wspec((D_EXPERT, D_MODEL), 1),
        ],
        out_specs=pl.BlockSpec((tm, D_MODEL), lambda i, *_: (i, 0)),
        scratch_shapes=[pltpu.VMEM((ROW_TILES, 2 * tm, 128), F32)],
    )
    return pl.pallas_call(
        _moe_kernel, grid_spec=grid_spec,
        out_shape=jax.ShapeDtypeStruct((n_rows, D_MODEL), F32),
        compiler_params=pltpu.CompilerParams(dimension_semantics=("arbitrary",),
                                             vmem_limit_bytes=VMEM_LIMIT),
        name="moe_experts",
    )(e1, e2, nused, xs, wg, wu, wd, wg, wu, wd)


def _combine_ple(pos, h1, p, ys, gple, wgate, wproj, gfin, final):
    n = h1.shape[0]
    blk = MOVE_BLOCK
    assert n % blk == 0
    grid_spec = pltpu.PrefetchScalarGridSpec(
        num_scalar_prefetch=1,
        grid=(n // blk,),
        in_specs=[
            pl.BlockSpec((blk, D_MODEL), lambda i, *_: (i, 0)),
            pl.BlockSpec((blk, PLE_DIM), lambda i, *_: (i, 0)),
            pl.BlockSpec(memory_space=pl.ANY),
            _const_spec((1, D_MODEL)),
            _const_spec((D_MODEL, D_MODEL)),
            _const_spec((PLE_DIM, D_MODEL)),
            _const_spec((1, D_MODEL)),
        ],
        out_specs=pl.BlockSpec((blk, D_MODEL), lambda i, *_: (i, 0)),
        scratch_shapes=[pltpu.VMEM((2, blk, D_MODEL), F32), pltpu.SemaphoreType.DMA((2,))],
    )
    return pl.pallas_call(
        functools.partial(_ple_kernel, final=final), grid_spec=grid_spec,
        out_shape=jax.ShapeDtypeStruct((n, D_MODEL), F32),
        compiler_params=pltpu.CompilerParams(dimension_semantics=("arbitrary",),
                                             vmem_limit_bytes=VMEM_LIMIT),
        name="combine_ple",
    )(pos, h1, p, ys, gple, wgate, wproj, gfin)


def _routing_plan(cls, rank, counts, n_tokens):
    tm = MOE_TILE
    n_tiles = n_tokens // tm + N_CLASS
    cnt = counts[:N_CLASS, 0].astype(I32)
    tiles_per_class = (cnt + tm - 1) // tm
    tile_end = jnp.cumsum(tiles_per_class)
    class_row0 = (tile_end - tiles_per_class) * tm
    pos = jnp.take(class_row0, cls.reshape(-1)) + rank.reshape(-1)
    nused = tile_end[-1]
    tidx = jnp.minimum(jnp.arange(n_tiles, dtype=I32), nused - 1)
    tcls = jnp.sum((tidx[:, None] >= tile_end[None, :]).astype(I32), axis=1)
    tcls = jnp.minimum(tcls, N_CLASS - 1)
    lo = jnp.array([a for a, _ in PAIRS], I32)
    hi = jnp.array([b for _, b in PAIRS], I32)
    grp = tcls // len(PAIRS)
    e1 = grp * PER_GROUP + jnp.take(lo, tcls % len(PAIRS))
    e2 = grp * PER_GROUP + jnp.take(hi, tcls % len(PAIRS))
    return pos.astype(I32), e1.astype(I32), e2.astype(I32), nused.reshape(1).astype(I32), n_tiles * tm


def _ffn_and_ple(h1, xext, cls, rank, counts, p_i, wg, wu, wd, gple, wgate, wproj, gfin, final):
    bsz, seq, _ = h1.shape
    n = bsz * seq
    pos, e1, e2, nused, n_rows = _routing_plan(cls, rank, counts, n)
    xs = _dispatch(pos, xext, n_rows)
    ys = _moe(e1, e2, nused, xs, wg, wu, wd)
    out = _combine_ple(pos, h1.reshape(n, D_MODEL), p_i.reshape(n, PLE_DIM), ys, gple, wgate, wproj,
                       gfin, final)
    return out.reshape(bsz, seq, D_MODEL)


def kernel(x, p, norm_mix, norm_ffn, norm_ple, norm_final, w_in_ab, hgrn_lb_logits, hgrn_out_norm, conv_w, w_out_ab, w_in_c, b_in_c, sinks, w_out_c, b_out_c, w_router, w_gate_e, w_up_e, w_down_e, w_ple_gate, w_ple_proj):
    depth = p.shape[0]
    lower_bounds = jnp.cumsum(jax.nn.softmax(hgrn_lb_logits.astype(F32), axis=0), axis=0)
    wr32 = w_router.astype(F32)
    wr_hi = wr32.astype(BF16)
    wr_lo = (wr32 - wr_hi.astype(F32)).astype(BF16)
    wrt = jnp.concatenate(
        [wr_hi, wr_lo, jnp.zeros((D_MODEL, 128 - 2 * N_EXPERTS), BF16)], axis=1)
    gfin = norm_final.reshape(1, D_MODEL)
    h = x
    for i in range(depth):
        j = i // 2
        gmix = norm_mix[i].reshape(1, D_MODEL)
        gffn = norm_ffn[i].reshape(1, D_MODEL)
        if i % 2 == 0:
            h1, xext, cls, rank, counts = _layer0_mixer(
                h, gmix, w_in_ab[j].astype(MXU), lower_bounds[i].reshape(1, A_WIDTH),
                hgrn_out_norm[j].reshape(1, A_DK), conv_w[j], w_out_ab[j].astype(MXU), gffn, wrt)
        else:
            h1, xext, cls, rank, counts = _layer1_mixer(
                h, sinks[j].astype(F32), gmix, w_in_c[j].astype(MXU), b_in_c[j].reshape(1, C_IN),
                w_out_c[j].astype(MXU), b_out_c[j].reshape(1, D_MODEL), gffn, wrt)
        h = _ffn_and_ple(
            h1, xext, cls, rank, counts, p[i],
            w_gate_e[i].astype(MXU), w_up_e[i].astype(MXU), w_down_e[i].astype(MXU),
            norm_ple[i].reshape(1, D_MODEL), w_ple_gate[i].astype(MXU), w_ple_proj[i].astype(MXU),
            gfin, final=(i == depth - 1))
    return h
```

```python
import functools

import jax
import jax.numpy as jnp
from jax import lax
from jax.experimental import pallas as pl
from jax.experimental.pallas import tpu as pltpu

F32 = jnp.float32
BF16 = jnp.bfloat16
MXU = jnp.bfloat16
I32 = jnp.int32
U32 = jnp.uint32

D_MODEL = 1024
EPS = 1e-6
A_WIDTH = 512
A_HEADS = 4
A_DK = 128
A_CHUNK = 32
AB_IN = 3584
HEAD_DIM = 64
Q_HEADS = 16
KV_HEADS = 2
GROUP = 8
WINDOW = 128
C_IN = 1280
N_EXPERTS = 16
N_GROUPS = 4
PER_GROUP = 4
D_EXPERT = 512
PLE_DIM = 256

PAIRS = ((0, 1), (0, 2), (0, 3), (1, 2), (1, 3), (2, 3))
N_CLASS = N_GROUPS * len(PAIRS)
CLASS_PAD = 32
X_WORDS = D_MODEL // 2
ROW_WORDS = X_WORDS + 128
ROW_TILES = ROW_WORDS // 128

L0_BLOCK = 256
L1_BLOCK = 512
MOE_TILE = 256
MOVE_BLOCK = 512
VMEM_LIMIT = 56 * 1024 * 1024

NT_DIMS = (((1,), (1,)), ((), ()))
TN_DIMS = (((0,), (0,)), ((), ()))


def _rms(x, g):
    ms = jnp.mean(x * x, axis=-1, keepdims=True)
    return x * lax.rsqrt(ms + EPS) * g


def _sigmoid(x):
    return 1.0 / (1.0 + jnp.exp(-x))


def _route_tail(h1, first, gffn_ref, wr_ref, xext_ref, cls_ref, rank_ref, cnt_ref, run_scr,
                pack_scr):
    tb = h1.shape[0]

    @pl.when(first)
    def _():
        run_scr[...] = jnp.zeros_like(run_scr)

    xn = _rms(h1, gffn_ref[...])
    x_hi = xn.astype(BF16)
    x_lo = (xn - x_hi.astype(F32)).astype(BF16)
    prod = (jnp.dot(x_hi, wr_ref[...], preferred_element_type=F32)
            + jnp.dot(x_lo, wr_ref[...], preferred_element_type=F32)).T
    logits = prod[0:N_EXPERTS, :] + prod[N_EXPERTS:2 * N_EXPERTS, :]
    mx = jnp.max(logits, axis=0, keepdims=True)
    ex = jnp.exp(logits - mx)
    sc = ex / jnp.sum(ex, axis=0, keepdims=True)
    rows = [sc[i:i + 1, :] for i in range(N_EXPERTS)]

    gscore = []
    for g in range(N_GROUPS):
        v = rows[PER_GROUP * g:PER_GROUP * (g + 1)]
        best = v[0] + v[1]
        for (i, j) in PAIRS[1:]:
            best = jnp.maximum(best, v[i] + v[j])
        gscore.append(best)
    gmax = jnp.maximum(jnp.maximum(gscore[0], gscore[1]), jnp.maximum(gscore[2], gscore[3]))
    sel = jnp.where(gscore[0] >= gmax, 0, jnp.where(gscore[1] >= gmax, 1,
                                                    jnp.where(gscore[2] >= gmax, 2, 3))).astype(I32)
    v = [jnp.where(sel == 0, rows[i], jnp.where(sel == 1, rows[4 + i],
                                                jnp.where(sel == 2, rows[8 + i], rows[12 + i])))
         for i in range(PER_GROUP)]
    chosen = []
    for i in range(PER_GROUP):
        r = jnp.zeros_like(sel)
        for j in range(PER_GROUP):
            if j == i:
                continue
            ahead = (v[j] >= v[i]) if j < i else (v[j] > v[i])
            r = r + jnp.where(ahead, 1, 0).astype(I32)
        chosen.append(r < 2)
    vsum = jnp.zeros_like(v[0])
    code = jnp.zeros_like(sel)
    for i in range(PER_GROUP):
        vsum = vsum + jnp.where(chosen[i], v[i], 0.0)
        code = code + jnp.where(chosen[i], 1 << i, 0).astype(I32)
    wts = [jnp.where(chosen[i], v[i] / vsum, 0.0) for i in range(PER_GROUP)]
    pair = jnp.where(code == 3, 0, jnp.where(code == 5, 1, jnp.where(code == 9, 2,
                     jnp.where(code == 6, 3, jnp.where(code == 10, 4, 5))))).astype(I32)
    cls = sel * len(PAIRS) + pair

    erow = lax.broadcasted_iota(I32, (128, tb), 0)
    comb_t = jnp.zeros((128, tb), F32)
    for i in range(PER_GROUP):
        comb_t = comb_t + jnp.where(erow == sel * PER_GROUP + i, wts[i], 0.0)
    comb = comb_t.T

    comb_hi = comb.astype(BF16).astype(F32)
    halves = [(xn[:, j * 128:(j + 1) * 128], xn[:, X_WORDS + j * 128:X_WORDS + (j + 1) * 128])
              for j in range(X_WORDS // 128)] + [(comb_hi, comb - comb_hi)]
    for j, (first_half, second_half) in enumerate(halves):
        pack_scr[j, pl.ds(0, tb, stride=2), :] = first_half
        pack_scr[j, pl.ds(1, tb, stride=2), :] = second_half
        xext_ref[:, j * 128:(j + 1) * 128] = pltpu.bitcast(pack_scr[j].astype(BF16), U32)

    crow = lax.broadcasted_iota(I32, (CLASS_PAD, tb), 0)
    onehot = jnp.where(crow == cls, 1.0, 0.0)
    ii = lax.broadcasted_iota(I32, (tb, tb), 0)
    jj = lax.broadcasted_iota(I32, (tb, tb), 1)
    upper = jnp.where(ii < jj, 1.0, 0.0).astype(MXU)
    prefix = jnp.dot(onehot.astype(MXU), upper, preferred_element_type=F32)
    run = run_scr[...]
    rank = jnp.sum(onehot * (prefix + run[:, 0:1]), axis=0, keepdims=True)
    run_new = run + jnp.sum(onehot, axis=1, keepdims=True)
    run_scr[...] = run_new
    cnt_ref[...] = run_new
    cls_ref[0] = cls
    rank_ref[0] = rank.astype(I32)


def _l0_kernel(h_ref, gmix_ref, win_ref, lb_ref, onorm_ref, convw_ref, wout_ref, gffn_ref, wr_ref,
               h1_ref, xext_ref, cls_ref, rank_ref, cnt_ref,
               z_scr, cat_scr, st_scr, carry_scr, run_scr, pack_scr):
    b = pl.program_id(0)
    t = pl.program_id(1)
    tb = h_ref.shape[1]

    @pl.when(t == 0)
    def _():
        st_scr[...] = jnp.zeros_like(st_scr)
        carry_scr[...] = jnp.zeros_like(carry_scr)

    h = h_ref[0]
    u = _rms(h, gmix_ref[...]).astype(MXU)
    z_scr[...] = jnp.dot(u, win_ref[...], preferred_element_type=F32)

    gate_b = z_scr[:, 4 * A_WIDTH:5 * A_WIDTH]
    cb = z_scr[:, 5 * A_WIDTH:6 * A_WIDTH] * z_scr[:, 6 * A_WIDTH:7 * A_WIDTH]
    row = lax.broadcasted_iota(I32, cb.shape, 0)
    prev1 = carry_scr[7:8, :]
    prev2 = carry_scr[6:7, :]
    m1 = jnp.where(row == 0, prev1, pltpu.roll(cb, 1, axis=0))
    m2 = jnp.where(row == 0, prev2, jnp.where(row == 1, prev1, pltpu.roll(cb, 2, axis=0)))
    cw = convw_ref[...]
    cat_scr[:, A_WIDTH:] = gate_b * (cw[0:1, :] * m2 + cw[1:2, :] * m1 + cw[2:3, :] * cb)
    carry_scr[...] = cb[tb - 8:tb, :]

    nc = tb // A_CHUNK
    lb = lb_ref[...]
    onorm = onorm_ref[...]
    zq = z_scr[:, 0:A_WIDTH]
    zf = z_scr[:, A_WIDTH:2 * A_WIDTH]
    zi = z_scr[:, 2 * A_WIDTH:3 * A_WIDTH]
    zg = z_scr[:, 3 * A_WIDTH:4 * A_WIDTH]
    f = lb + (1.0 - lb) * _sigmoid(zf)
    logf = jnp.log(f)
    ri = lax.broadcasted_iota(I32, (tb, tb), 0)
    rj = lax.broadcasted_iota(I32, (tb, tb), 1)
    same_chunk = (ri // A_CHUNK) == (rj // A_CHUNK)
    causal = same_chunk & (ri >= rj)
    tri = jnp.where(causal, 1.0, 0.0).astype(BF16)
    p1 = logf.astype(BF16)
    r1 = logf - p1.astype(F32)
    p2 = r1.astype(BF16)
    p3 = (r1 - p2.astype(F32)).astype(BF16)
    pieces = jnp.dot(tri, jnp.concatenate([p1, p2, p3], axis=1), preferred_element_type=F32)
    bcum = (pieces[:, 0:A_WIDTH] + pieces[:, A_WIDTH:2 * A_WIDTH]) + pieces[:, 2 * A_WIDTH:]
    blast3 = bcum.reshape(nc, A_CHUNK, A_WIDTH)[:, A_CHUNK - 1:A_CHUNK, :]
    blast = jnp.broadcast_to(blast3, (nc, A_CHUNK, A_WIDTH)).reshape(tb, A_WIDTH)
    k = 1.0 - f
    q_dec = ((zq * _sigmoid(zq)) * jnp.exp(bcum)).astype(MXU)
    k_inv = (k * jnp.exp(-bcum)).astype(MXU)
    k_tail = (k * jnp.exp(blast - bcum)).astype(MXU)
    decay = jnp.exp(blast3)
    gate = _sigmoid(zg)
    vals = zi.astype(MXU)
    in_block = (lax.broadcasted_iota(I32, (tb, nc * A_DK), 0) // A_CHUNK
                == lax.broadcasted_iota(I32, (tb, nc * A_DK), 1) // A_DK)
    zero = jnp.zeros((), MXU)
    for hd in range(A_HEADS):
        sl = slice(hd * A_DK, (hd + 1) * A_DK)
        qd = q_dec[:, sl]
        vv = vals[:, sl]
        scores = lax.dot_general(qd, k_inv[:, sl], NT_DIMS, preferred_element_type=F32)
        scores = jnp.where(causal, scores, 0.0).astype(MXU)
        o = jnp.dot(scores, vv, preferred_element_type=F32)
        kt_blocks = jnp.where(in_block, jnp.tile(k_tail[:, sl], (1, nc)), zero)
        kv_all = lax.dot_general(vv, kt_blocks, TN_DIMS, preferred_element_type=F32)
        st = st_scr[hd]
        states = []
        for c in range(nc):
            cs = slice(c * A_DK, (c + 1) * A_DK)
            states.append(st.astype(MXU))
            st = st * decay[c, :, sl] + kv_all[:, cs]
        st_scr[hd] = st
        qd_blocks = jnp.where(in_block, jnp.tile(qd, (1, nc)), zero)
        o = o + lax.dot_general(qd_blocks, jnp.concatenate(states, axis=1), NT_DIMS,
                                preferred_element_type=F32)
        cat_scr[:, sl] = _rms(o, onorm) * gate[:, sl]

    h1 = h + jnp.dot(cat_scr[...].astype(MXU), wout_ref[...], preferred_element_type=F32)
    h1_ref[0] = h1
    _route_tail(h1, (b == 0) & (t == 0), gffn_ref, wr_ref, xext_ref, cls_ref, rank_ref, cnt_ref,
                run_scr, pack_scr)


def _l1_kernel(sinks_ref, h_ref, gmix_ref, wqt_ref, bqt_ref, wk_ref, bk_ref, wvt_ref, bvt_ref,
               wout_ref, bout_ref, gffn_ref, wr_ref,
               h1_ref, xext_ref, cls_ref, rank_ref, cnt_ref,
               qt_scr, k_scr, vt_scr, att_scr, run_scr, pack_scr):
    b = pl.program_id(0)
    t = pl.program_id(1)
    tb = h_ref.shape[1]

    @pl.when(t == 0)
    def _():
        k_scr[:, 0:WINDOW, :] = jnp.zeros((KV_HEADS, WINDOW, HEAD_DIM), k_scr.dtype)
        vt_scr[:, 0:WINDOW] = jnp.zeros((KV_HEADS * HEAD_DIM, WINDOW), vt_scr.dtype)

    h = h_ref[0]
    u = _rms(h, gmix_ref[...]).astype(MXU)
    scale = HEAD_DIM ** -0.5
    qt_scr[...] = ((lax.dot_general(wqt_ref[...], u, NT_DIMS, preferred_element_type=F32)
                    + bqt_ref[...]) * scale).astype(qt_scr.dtype)
    kz = (jnp.dot(u, wk_ref[...], preferred_element_type=F32) + bk_ref[...]).astype(k_scr.dtype)
    for kk in range(KV_HEADS):
        k_scr[kk, WINDOW:, :] = kz[:, kk * HEAD_DIM:(kk + 1) * HEAD_DIM]
    vt_scr[:, WINDOW:] = (lax.dot_general(wvt_ref[...], u, NT_DIMS, preferred_element_type=F32)
                          + bvt_ref[...]).astype(vt_scr.dtype)

    gw = GROUP * WINDOW
    kj = lax.broadcasted_iota(I32, (2 * WINDOW, gw), 0)
    qi = lax.broadcasted_iota(I32, (2 * WINDOW, gw), 1) % WINDOW
    for n in range(tb // WINDOW):
        rs = slice(n * WINDOW, (n + 1) * WINDOW)
        win = slice(n * WINDOW, (n + 2) * WINDOW)
        first_key = 0 if n > 0 else jnp.where(t > 0, 0, WINDOW)
        mask = (kj > jnp.maximum(qi, first_key - 1)) & (kj <= qi + WINDOW)
        for kk in range(KV_HEADS):
            heads = range(kk * GROUP, (kk + 1) * GROUP)
            keys = k_scr[kk, win, :]
            vals_t = vt_scr[kk * HEAD_DIM:(kk + 1) * HEAD_DIM, win]
            q_t = jnp.concatenate(
                [qt_scr[hd * HEAD_DIM:(hd + 1) * HEAD_DIM, rs] for hd in heads], axis=1)
            sink = jnp.concatenate(
                [jnp.full((1, WINDOW), sinks_ref[hd], F32) for hd in heads], axis=1)
            s = jnp.dot(keys, q_t, preferred_element_type=F32)
            s = jnp.where(mask, s, -jnp.inf)
            m = jnp.maximum(jnp.max(s, axis=0, keepdims=True), sink)
            e = jnp.exp(s - m)
            denom = jnp.sum(e, axis=0, keepdims=True) + jnp.exp(sink - m)
            o_t = jnp.dot(vals_t, e.astype(MXU), preferred_element_type=F32) / denom
            for g2 in range(GROUP // 2):
                pair = jnp.concatenate([o_t[:, (2 * g2) * WINDOW:(2 * g2 + 1) * WINDOW],
                                        o_t[:, (2 * g2 + 1) * WINDOW:(2 * g2 + 2) * WINDOW]], axis=0)
                h0 = (kk * GROUP + 2 * g2) * HEAD_DIM
                att_scr[rs, h0:h0 + 2 * HEAD_DIM] = pair.T
    k_scr[:, 0:WINDOW, :] = k_scr[:, tb:tb + WINDOW, :]
    vt_scr[:, 0:WINDOW] = vt_scr[:, tb:tb + WINDOW]

    h1 = (h + jnp.dot(att_scr[...].astype(MXU), wout_ref[...], preferred_element_type=F32)
          + bout_ref[...])
    h1_ref[0] = h1
    _route_tail(h1, (b == 0) & (t == 0), gffn_ref, wr_ref, xext_ref, cls_ref, rank_ref, cnt_ref,
                run_scr, pack_scr)


def _row_copy(src, src_row, dst, dst_row, sem):
    return pltpu.make_async_copy(src.at[pl.ds(src_row, 1)], dst.at[pl.ds(dst_row, 1)], sem)


ISSUE_UNROLL = 8


def _scatter_kernel(pos_ref, x_ref, init_ref, out_ref, sem):
    del init_ref
    rows = x_ref.shape[0]
    base = pl.program_id(0) * rows

    def issue(g, carry):
        r0 = pl.multiple_of(g * ISSUE_UNROLL, ISSUE_UNROLL)
        group = x_ref.at[pl.ds(r0, ISSUE_UNROLL)]
        for k in range(ISSUE_UNROLL):
            _row_copy(group, k, out_ref, pos_ref[base + r0 + k], sem).start()
        return carry

    lax.fori_loop(0, rows // ISSUE_UNROLL, issue, 0)
    pltpu.make_async_copy(x_ref, out_ref.at[pl.ds(0, rows)], sem).wait()


def _moe_kernel(e1_ref, e2_ref, nused_ref, x_ref, wg1_ref, wu1_ref, wd1_ref, wg2_ref, wu2_ref,
                wd2_ref, y_ref, unpack_scr):
    i = pl.program_id(0)
    tm = x_ref.shape[0]

    @pl.when(i < nused_ref[0])
    def _():
        nchunk = ROW_TILES
        for j in range(nchunk):
            unpack_scr[j] = pltpu.bitcast(x_ref[:, j * 128:(j + 1) * 128], BF16).astype(F32)
        xa = [unpack_scr[j, pl.ds(0, tm, stride=2), :] for j in range(nchunk)]
        xb = [unpack_scr[j, pl.ds(1, tm, stride=2), :] for j in range(nchunk)]
        x = jnp.concatenate(xa[:-1] + xb[:-1], axis=1).astype(MXU)
        comb = xa[-1] + xb[-1]
        lane = lax.broadcasted_iota(I32, comb.shape, 1)

        def expert(e, wg_ref, wu_ref, wd_ref):
            cw = jnp.sum(jnp.where(lane == e, comb, 0.0), axis=1, keepdims=True)
            gate = jnp.dot(x, wg_ref[0], preferred_element_type=F32)
            up = jnp.dot(x, wu_ref[0], preferred_element_type=F32)
            hidden = (gate * _sigmoid(gate)) * up
            return cw * jnp.dot(hidden.astype(MXU), wd_ref[0], preferred_element_type=F32)

        y_ref[...] = (expert(e1_ref[i], wg1_ref, wu1_ref, wd1_ref)
                      + expert(e2_ref[i], wg2_ref, wu2_ref, wd2_ref))

    @pl.when(i >= nused_ref[0])
    def _():
        y_ref[...] = jnp.zeros_like(y_ref)


def _ple_kernel(pos_ref, h1_ref, p_ref, y_ref, gple_ref, wgate_ref, wproj_ref, gfin_ref,
                out_ref, ybuf, sem, *, final):
    rows = h1_ref.shape[0]
    i = pl.program_id(0)
    nsteps = pl.num_programs(0)
    slot = i % 2

    def gather(step, buf):
        def issue(g, carry):
            r0 = pl.multiple_of(g * ISSUE_UNROLL, ISSUE_UNROLL)
            group = ybuf.at[buf, pl.ds(r0, ISSUE_UNROLL)]
            for k in range(ISSUE_UNROLL):
                _row_copy(y_ref, pos_ref[step * rows + r0 + k], group, k, sem.at[buf]).start()
            return carry

        lax.fori_loop(0, rows // ISSUE_UNROLL, issue, 0)

    @pl.when(i == 0)
    def _():
        gather(0, 0)

    @pl.when(i + 1 < nsteps)
    def _():
        gather(i + 1, 1 - slot)

    proj = jnp.dot(p_ref[...].astype(MXU), wproj_ref[...], preferred_element_type=F32)
    pltpu.make_async_copy(y_ref.at[pl.ds(0, rows)], ybuf.at[slot], sem.at[slot]).wait()
    h2 = h1_ref[...] + ybuf[slot]
    gate = _sigmoid(jnp.dot(_rms(h2, gple_ref[...]).astype(MXU), wgate_ref[...],
                            preferred_element_type=F32))
    h3 = h2 + gate * proj
    if final:
        h3 = _rms(h3, gfin_ref[...])
    out_ref[...] = h3


def _const_spec(shape):
    nd = len(shape)
    return pl.BlockSpec(shape, lambda *_: (0,) * nd)


def _route_out(bsz, seq, tb):
    nt = seq // tb
    n = bsz * seq
    shapes = [
        jax.ShapeDtypeStruct((bsz, seq, D_MODEL), F32),
        jax.ShapeDtypeStruct((n, ROW_WORDS), U32),
        jax.ShapeDtypeStruct((bsz * nt, 1, tb), I32),
        jax.ShapeDtypeStruct((bsz * nt, 1, tb), I32),
        jax.ShapeDtypeStruct((CLASS_PAD, 128), F32),
    ]
    specs = [
        pl.BlockSpec((1, tb, D_MODEL), lambda b, t, *_: (b, t, 0)),
        pl.BlockSpec((tb, ROW_WORDS), lambda b, t, *_: (b * nt + t, 0)),
        pl.BlockSpec((1, 1, tb), lambda b, t, *_: (b * nt + t, 0, 0)),
        pl.BlockSpec((1, 1, tb), lambda b, t, *_: (b * nt + t, 0, 0)),
        pl.BlockSpec((CLASS_PAD, 128), lambda b, t, *_: (0, 0)),
    ]
    return shapes, specs


def _params():
    return pltpu.CompilerParams(dimension_semantics=("arbitrary", "arbitrary"),
                                vmem_limit_bytes=VMEM_LIMIT)


def _layer0_mixer(h, gmix, win, lb, onorm, convw, wout, gffn, wrt):
    bsz, seq, _ = h.shape
    tb = L0_BLOCK
    assert seq % tb == 0
    shapes, ospecs = _route_out(bsz, seq, tb)
    return pl.pallas_call(
        _l0_kernel,
        grid=(bsz, seq // tb),
        in_specs=[
            pl.BlockSpec((1, tb, D_MODEL), lambda b, t: (b, t, 0)),
            _const_spec((1, D_MODEL)),
            _const_spec((D_MODEL, AB_IN)),
            _const_spec((1, A_WIDTH)),
            _const_spec((1, A_DK)),
            _const_spec((3, A_WIDTH)),
            _const_spec((D_MODEL, D_MODEL)),
            _const_spec((1, D_MODEL)),
            _const_spec((D_MODEL, 128)),
        ],
        out_specs=ospecs,
        out_shape=shapes,
        scratch_shapes=[
            pltpu.VMEM((tb, AB_IN), F32),
            pltpu.VMEM((tb, D_MODEL), F32),
            pltpu.VMEM((A_HEADS, A_DK, A_DK), F32),
            pltpu.VMEM((8, A_WIDTH), F32),
            pltpu.VMEM((CLASS_PAD, 128), F32),
            pltpu.VMEM((ROW_TILES, 2 * tb, 128), F32),
        ],
        compiler_params=_params(),
        name="layer0_mixer",
    )(h, gmix, win, lb, onorm, convw, wout, gffn, wrt)


def _layer1_mixer(h, sinks, gmix, win, bin_, wout, bout, gffn, wrt):
    bsz, seq, _ = h.shape
    tb = L1_BLOCK
    assert seq % tb == 0
    shapes, ospecs = _route_out(bsz, seq, tb)
    qw = Q_HEADS * HEAD_DIM
    vw = KV_HEADS * HEAD_DIM
    wqt, wk, wvt = win[:, :qw].T, win[:, qw:qw + vw], win[:, qw + vw:].T
    bqt, bk, bvt = bin_[:, :qw].reshape(qw, 1), bin_[:, qw:qw + vw], bin_[:, qw + vw:].reshape(vw, 1)
    grid_spec = pltpu.PrefetchScalarGridSpec(
        num_scalar_prefetch=1,
        grid=(bsz, seq // tb),
        in_specs=[
            pl.BlockSpec((1, tb, D_MODEL), lambda b, t, *_: (b, t, 0)),
            _const_spec((1, D_MODEL)),
            _const_spec((qw, D_MODEL)),
            _const_spec((qw, 1)),
            _const_spec((D_MODEL, vw)),
            _const_spec((1, vw)),
            _const_spec((vw, D_MODEL)),
            _const_spec((vw, 1)),
            _const_spec((D_MODEL, D_MODEL)),
            _const_spec((1, D_MODEL)),
            _const_spec((1, D_MODEL)),
            _const_spec((D_MODEL, 128)),
        ],
        out_specs=ospecs,
        scratch_shapes=[
            pltpu.VMEM((qw, tb), MXU),
            pltpu.VMEM((KV_HEADS, tb + WINDOW, HEAD_DIM), MXU),
            pltpu.VMEM((vw, tb + WINDOW), MXU),
            pltpu.VMEM((tb, D_MODEL), F32),
            pltpu.VMEM((CLASS_PAD, 128), F32),
            pltpu.VMEM((ROW_TILES, 2 * tb, 128), F32),
        ],
    )
    return pl.pallas_call(
        _l1_kernel, grid_spec=grid_spec, out_shape=shapes, compiler_params=_params(),
        name="layer1_mixer",
    )(sinks, h, gmix, wqt, bqt, wk, bk, wvt, bvt, wout, bout, gffn, wrt)


def _dispatch(pos, xext, n_rows):
    n = xext.shape[0]
    blk = MOVE_BLOCK
    assert n % blk == 0
    grid_spec = pltpu.PrefetchScalarGridSpec(
        num_scalar_prefetch=1,
        grid=(n // blk,),
        in_specs=[
            pl.BlockSpec((blk, ROW_WORDS), lambda i, *_: (i, 0)),
            pl.BlockSpec(memory_space=pl.ANY),
        ],
        out_specs=pl.BlockSpec(memory_space=pl.ANY),
        scratch_shapes=[pltpu.SemaphoreType.DMA(())],
    )
    init = jnp.zeros((n_rows, ROW_WORDS), U32)
    return pl.pallas_call(
        _scatter_kernel, grid_spec=grid_spec,
        out_shape=jax.ShapeDtypeStruct((n_rows, ROW_WORDS), U32),
        input_output_aliases={2: 0},
        compiler_params=pltpu.CompilerParams(dimension_semantics=("arbitrary",),
                                             vmem_limit_bytes=VMEM_LIMIT),
        name="moe_dispatch",
    )(pos, xext, init)


def _moe(e1, e2, nused, xs, wg, wu, wd):
    n_rows = xs.shape[0]
    tm = MOE_TILE
    n_tiles = n_rows // tm

    def wspec(shape, which):
        return pl.BlockSpec((1,) + shape, lambda i, e1, e2, nu: ((e1, e2)[which][i], 0, 0))

    grid_spec = pltpu.PrefetchScalarGridSpec(
        num_scalar_prefetch=3,
        grid=(n_tiles,),
        in_specs=[
            pl.BlockSpec((tm, ROW_WORDS), lambda i, *_: (i, 0)),
            wspec((D_MODEL, D_EXPERT), 0), wspec((D_MODEL, D_EXPERT), 0), wspec((D_EXPERT, D_MODEL), 0),
            wspec((D_MODEL, D_EXPERT), 1), wspec((D_MODEL, D_EXPERT), 1), wspec((D_EXPERT, D_MODEL), 1),
        ],
        out_specs=pl.BlockSpec((tm, D_MODEL), lambda i, *_: (i, 0)),
        scratch_shapes=[pltpu.VMEM((ROW_TILES, 2 * tm, 128), F32)],
    )
    return pl.pallas_call(
        _moe_kernel, grid_spec=grid_spec,
        out_shape=jax.ShapeDtypeStruct((n_rows, D_MODEL), F32),
        compiler_params=pltpu.CompilerParams(dimension_semantics=("arbitrary",),
                                             vmem_limit_bytes=VMEM_LIMIT),
        name="moe_experts",
    )(e1, e2, nused, xs, wg, wu, wd, wg, wu, wd)


def _combine_ple(pos, h1, p, layer, ys, gple, wgate, wproj, gfin, final):
    n = h1.shape[0]
    blk = MOVE_BLOCK
    assert n % blk == 0
    grid_spec = pltpu.PrefetchScalarGridSpec(
        num_scalar_prefetch=1,
        grid=(n // blk,),
        in_specs=[
            pl.BlockSpec((blk, D_MODEL), lambda i, *_: (i, 0)),
            pl.BlockSpec((None, blk, PLE_DIM), lambda i, *_: (layer, i, 0)),
            pl.BlockSpec(memory_space=pl.ANY),
            _const_spec((1, D_MODEL)),
            _const_spec((D_MODEL, D_MODEL)),
            _const_spec((PLE_DIM, D_MODEL)),
            _const_spec((1, D_MODEL)),
        ],
        out_specs=pl.BlockSpec((blk, D_MODEL), lambda i, *_: (i, 0)),
        scratch_shapes=[pltpu.VMEM((2, blk, D_MODEL), F32), pltpu.SemaphoreType.DMA((2,))],
    )
    return pl.pallas_call(
        functools.partial(_ple_kernel, final=final), grid_spec=grid_spec,
        out_shape=jax.ShapeDtypeStruct((n, D_MODEL), F32),
        compiler_params=pltpu.CompilerParams(dimension_semantics=("arbitrary",),
                                             vmem_limit_bytes=VMEM_LIMIT),
        name="combine_ple",
    )(pos, h1, p, ys, gple, wgate, wproj, gfin)


def _routing_plan(cls, rank, counts, n_tokens):
    tm = MOE_TILE
    n_tiles = n_tokens // tm + N_CLASS
    cnt = counts[:N_CLASS, 0].astype(I32)
    tiles_per_class = (cnt + tm - 1) // tm
    tile_end = jnp.cumsum(tiles_per_class)
    class_row0 = (tile_end - tiles_per_class) * tm
    pos = jnp.take(class_row0, cls.reshape(-1)) + rank.reshape(-1)
    nused = tile_end[-1]
    tidx = jnp.minimum(jnp.arange(n_tiles, dtype=I32), nused - 1)
    tcls = jnp.sum((tidx[:, None] >= tile_end[None, :]).astype(I32), axis=1)
    tcls = jnp.minimum(tcls, N_CLASS - 1)
    lo = jnp.array([a for a, _ in PAIRS], I32)
    hi = jnp.array([b for _, b in PAIRS], I32)
    grp = tcls // len(PAIRS)
    e1 = grp * PER_GROUP + jnp.take(lo, tcls % len(PAIRS))
    e2 = grp * PER_GROUP + jnp.take(hi, tcls % len(PAIRS))
    return pos.astype(I32), e1.astype(I32), e2.astype(I32), nused.reshape(1).astype(I32), n_tiles * tm


def _ffn_and_ple(h1, xext, cls, rank, counts, p, layer, wg, wu, wd, gple, wgate, wproj, gfin,
                 final):
    bsz, seq, _ = h1.shape
    n = bsz * seq
    pos, e1, e2, nused, n_rows = _routing_plan(cls, rank, counts, n)
    xs = _dispatch(pos, xext, n_rows)
    ys = _moe(e1, e2, nused, xs, wg, wu, wd)
    out = _combine_ple(pos, h1.reshape(n, D_MODEL), p.reshape(p.shape[0], n, PLE_DIM), layer, ys,
                       gple, wgate, wproj, gfin, final)
    return out.reshape(bsz, seq, D_MODEL)


def kernel(x, p, norm_mix, norm_ffn, norm_ple, norm_final, w_in_ab, hgrn_lb_logits, hgrn_out_norm, conv_w, w_out_ab, w_in_c, b_in_c, sinks, w_out_c, b_out_c, w_router, w_gate_e, w_up_e, w_down_e, w_ple_gate, w_ple_proj):
    depth = p.shape[0]
    lower_bounds = jnp.cumsum(jax.nn.softmax(hgrn_lb_logits.astype(F32), axis=0), axis=0)
    wr32 = w_router.astype(F32)
    wr_hi = wr32.astype(BF16)
    wr_lo = (wr32 - wr_hi.astype(F32)).astype(BF16)
    wrt = jnp.concatenate(
        [wr_hi, wr_lo, jnp.zeros((D_MODEL, 128 - 2 * N_EXPERTS), BF16)], axis=1)
    gfin = norm_final.reshape(1, D_MODEL)
    h = x
    for i in range(depth):
        j = i // 2
        gmix = norm_mix[i].reshape(1, D_MODEL)
        gffn = norm_ffn[i].reshape(1, D_MODEL)
        if i % 2 == 0:
            h1, xext, cls, rank, counts = _layer0_mixer(
                h, gmix, w_in_ab[j].astype(MXU), lower_bounds[i].reshape(1, A_WIDTH),
                hgrn_out_norm[j].reshape(1, A_DK), conv_w[j], w_out_ab[j].astype(MXU), gffn, wrt)
        else:
            h1, xext, cls, rank, counts = _layer1_mixer(
                h, sinks[j].astype(F32), gmix, w_in_c[j].astype(MXU), b_in_c[j].reshape(1, C_IN),
                w_out_c[j].astype(MXU), b_out_c[j].reshape(1, D_MODEL), gffn, wrt)
        h = _ffn_and_ple(
            h1, xext, cls, rank, counts, p, i,
            w_gate_e[i].astype(MXU), w_up_e[i].astype(MXU), w_down_e[i].astype(MXU),
            norm_ple[i].reshape(1, D_MODEL), w_ple_gate[i].astype(MXU), w_ple_proj[i].astype(MXU),
            gfin, final=(i == depth - 1))
    return h
```

```python
import functools

import jax
import jax.numpy as jnp
from jax import lax
from jax.experimental import pallas as pl
from jax.experimental.pallas import tpu as pltpu

F32 = jnp.float32
BF16 = jnp.bfloat16
MXU = jnp.bfloat16
I32 = jnp.int32
U32 = jnp.uint32

D_MODEL = 1024
EPS = 1e-6
A_WIDTH = 512
A_HEADS = 4
A_DK = 128
A_CHUNK = 32
AB_IN = 3584
HEAD_DIM = 64
Q_HEADS = 16
KV_HEADS = 2
GROUP = 8
WINDOW = 128
C_IN = 1280
N_EXPERTS = 16
N_GROUPS = 4
PER_GROUP = 4
D_EXPERT = 512
PLE_DIM = 256

PAIRS = ((0, 1), (0, 2), (0, 3), (1, 2), (1, 3), (2, 3))
N_CLASS = N_GROUPS * len(PAIRS)
CLASS_PAD = 32
X_WORDS = D_MODEL // 2
ROW_WORDS = X_WORDS + 128
ROW_TILES = ROW_WORDS // 128

L0_BLOCK = 256
L1_BLOCK = 512
MOE_TILE = 256
MOVE_BLOCK = 512
VMEM_LIMIT = 56 * 1024 * 1024

NT_DIMS = (((1,), (1,)), ((), ()))
TN_DIMS = (((0,), (0,)), ((), ()))


def _rms(x, g):
    ms = jnp.mean(x * x, axis=-1, keepdims=True)
    return x * lax.rsqrt(ms + EPS) * g


def _sigmoid(x):
    return 1.0 / (1.0 + jnp.exp(-x))


def _route_tail(h1, first, gffn_ref, wr_ref, xext_ref, cls_ref, rank_ref, cnt_ref, run_scr,
                pack_scr):
    tb = h1.shape[0]

    @pl.when(first)
    def _():
        run_scr[...] = jnp.zeros_like(run_scr)

    xn = _rms(h1, gffn_ref[...])
    x_hi = xn.astype(BF16)
    x_lo = (xn - x_hi.astype(F32)).astype(BF16)
    prod = (jnp.dot(x_hi, wr_ref[...], preferred_element_type=F32)
            + jnp.dot(x_lo, wr_ref[...], preferred_element_type=F32)).T
    logits = prod[0:N_EXPERTS, :] + prod[N_EXPERTS:2 * N_EXPERTS, :]
    mx = jnp.max(logits, axis=0, keepdims=True)
    ex = jnp.exp(logits - mx)
    sc = ex / jnp.sum(ex, axis=0, keepdims=True)
    rows = [sc[i:i + 1, :] for i in range(N_EXPERTS)]

    gscore = []
    for g in range(N_GROUPS):
        v = rows[PER_GROUP * g:PER_GROUP * (g + 1)]
        best = v[0] + v[1]
        for (i, j) in PAIRS[1:]:
            best = jnp.maximum(best, v[i] + v[j])
        gscore.append(best)
    gmax = jnp.maximum(jnp.maximum(gscore[0], gscore[1]), jnp.maximum(gscore[2], gscore[3]))
    sel = jnp.where(gscore[0] >= gmax, 0, jnp.where(gscore[1] >= gmax, 1,
                                                    jnp.where(gscore[2] >= gmax, 2, 3))).astype(I32)
    v = [jnp.where(sel == 0, rows[i], jnp.where(sel == 1, rows[4 + i],
                                                jnp.where(sel == 2, rows[8 + i], rows[12 + i])))
         for i in range(PER_GROUP)]
    chosen = []
    for i in range(PER_GROUP):
        r = jnp.zeros_like(sel)
        for j in range(PER_GROUP):
            if j == i:
                continue
            ahead = (v[j] >= v[i]) if j < i else (v[j] > v[i])
            r = r + jnp.where(ahead, 1, 0).astype(I32)
        chosen.append(r < 2)
    vsum = jnp.zeros_like(v[0])
    code = jnp.zeros_like(sel)
    for i in range(PER_GROUP):
        vsum = vsum + jnp.where(chosen[i], v[i], 0.0)
        code = code + jnp.where(chosen[i], 1 << i, 0).astype(I32)
    wts = [jnp.where(chosen[i], v[i] / vsum, 0.0) for i in range(PER_GROUP)]
    pair = jnp.where(code == 3, 0, jnp.where(code == 5, 1, jnp.where(code == 9, 2,
                     jnp.where(code == 6, 3, jnp.where(code == 10, 4, 5))))).astype(I32)
    cls = sel * len(PAIRS) + pair

    erow = lax.broadcasted_iota(I32, (128, tb), 0)
    comb_t = jnp.zeros((128, tb), F32)
    for i in range(PER_GROUP):
        comb_t = comb_t + jnp.where(erow == sel * PER_GROUP + i, wts[i], 0.0)
    comb = comb_t.T

    comb_hi = comb.astype(BF16).astype(F32)
    halves = [(xn[:, j * 128:(j + 1) * 128], xn[:, X_WORDS + j * 128:X_WORDS + (j + 1) * 128])
              for j in range(X_WORDS // 128)] + [(comb_hi, comb - comb_hi)]
    for j, (first_half, second_half) in enumerate(halves):
        pack_scr[j, pl.ds(0, tb, stride=2), :] = first_half
        pack_scr[j, pl.ds(1, tb, stride=2), :] = second_half
        xext_ref[:, j * 128:(j + 1) * 128] = pltpu.bitcast(pack_scr[j].astype(BF16), U32)

    crow = lax.broadcasted_iota(I32, (CLASS_PAD, tb), 0)
    onehot = jnp.where(crow == cls, 1.0, 0.0)
    ii = lax.broadcasted_iota(I32, (tb, tb), 0)
    jj = lax.broadcasted_iota(I32, (tb, tb), 1)
    upper = jnp.where(ii < jj, 1.0, 0.0).astype(MXU)
    prefix = jnp.dot(onehot.astype(MXU), upper, preferred_element_type=F32)
    run = run_scr[...]
    rank = jnp.sum(onehot * (prefix + run[:, 0:1]), axis=0, keepdims=True)
    run_new = run + jnp.sum(onehot, axis=1, keepdims=True)
    run_scr[...] = run_new
    cnt_ref[...] = run_new
    cls_ref[0] = cls
    rank_ref[0] = rank.astype(I32)


def _l0_kernel(h_ref, gmix_ref, win_ref, lb_ref, onorm_ref, convw_ref, wout_ref, gffn_ref, wr_ref,
               h1_ref, xext_ref, cls_ref, rank_ref, cnt_ref,
               z_scr, cat_scr, st_scr, carry_scr, run_scr, pack_scr):
    b = pl.program_id(0)
    t = pl.program_id(1)
    tb = h_ref.shape[1]

    @pl.when(t == 0)
    def _():
        st_scr[...] = jnp.zeros_like(st_scr)
        carry_scr[...] = jnp.zeros_like(carry_scr)

    h = h_ref[0]
    u = _rms(h, gmix_ref[...]).astype(MXU)
    z_scr[...] = jnp.dot(u, win_ref[...], preferred_element_type=F32)

    gate_b = z_scr[:, 4 * A_WIDTH:5 * A_WIDTH]
    cb = z_scr[:, 5 * A_WIDTH:6 * A_WIDTH] * z_scr[:, 6 * A_WIDTH:7 * A_WIDTH]
    row = lax.broadcasted_iota(I32, cb.shape, 0)
    prev1 = carry_scr[7:8, :]
    prev2 = carry_scr[6:7, :]
    m1 = jnp.where(row == 0, prev1, pltpu.roll(cb, 1, axis=0))
    m2 = jnp.where(row == 0, prev2, jnp.where(row == 1, prev1, pltpu.roll(cb, 2, axis=0)))
    cw = convw_ref[...]
    cat_scr[:, A_WIDTH:] = gate_b * (cw[0:1, :] * m2 + cw[1:2, :] * m1 + cw[2:3, :] * cb)
    carry_scr[...] = cb[tb - 8:tb, :]

    nc = tb // A_CHUNK
    lb = lb_ref[...]
    onorm = onorm_ref[...]
    zq = z_scr[:, 0:A_WIDTH]
    zf = z_scr[:, A_WIDTH:2 * A_WIDTH]
    zi = z_scr[:, 2 * A_WIDTH:3 * A_WIDTH]
    zg = z_scr[:, 3 * A_WIDTH:4 * A_WIDTH]
    f = lb + (1.0 - lb) * _sigmoid(zf)
    logf = jnp.log(f)
    ri = lax.broadcasted_iota(I32, (tb, tb), 0)
    rj = lax.broadcasted_iota(I32, (tb, tb), 1)
    same_chunk = (ri // A_CHUNK) == (rj // A_CHUNK)
    causal = same_chunk & (ri >= rj)
    tri = jnp.where(causal, 1.0, 0.0).astype(BF16)
    p1 = logf.astype(BF16)
    r1 = logf - p1.astype(F32)
    p2 = r1.astype(BF16)
    p3 = (r1 - p2.astype(F32)).astype(BF16)
    pieces = jnp.dot(tri, jnp.concatenate([p1, p2, p3], axis=1), preferred_element_type=F32)
    bcum = (pieces[:, 0:A_WIDTH] + pieces[:, A_WIDTH:2 * A_WIDTH]) + pieces[:, 2 * A_WIDTH:]
    blast3 = bcum.reshape(nc, A_CHUNK, A_WIDTH)[:, A_CHUNK - 1:A_CHUNK, :]
    blast = jnp.broadcast_to(blast3, (nc, A_CHUNK, A_WIDTH)).reshape(tb, A_WIDTH)
    k = 1.0 - f
    q_dec = ((zq * _sigmoid(zq)) * jnp.exp(bcum)).astype(MXU)
    k_inv = (k * jnp.exp(-bcum)).astype(MXU)
    k_tail = (k * jnp.exp(blast - bcum)).astype(MXU)
    decay = jnp.exp(blast3)
    gate = _sigmoid(zg)
    vals = zi.astype(MXU)
    in_block = (lax.broadcasted_iota(I32, (tb, nc * A_DK), 0) // A_CHUNK
                == lax.broadcasted_iota(I32, (tb, nc * A_DK), 1) // A_DK)
    zero = jnp.zeros((), MXU)
    for hd in range(A_HEADS):
        sl = slice(hd * A_DK, (hd + 1) * A_DK)
        qd = q_dec[:, sl]
        vv = vals[:, sl]
        scores = lax.dot_general(qd, k_inv[:, sl], NT_DIMS, preferred_element_type=F32)
        scores = jnp.where(causal, scores, 0.0).astype(MXU)
        o = jnp.dot(scores, vv, preferred_element_type=F32)
        kt_blocks = jnp.where(in_block, jnp.tile(k_tail[:, sl], (1, nc)), zero)
        kv_all = lax.dot_general(vv, kt_blocks, TN_DIMS, preferred_element_type=F32)
        st = st_scr[hd]
        states = []
        for c in range(nc):
            cs = slice(c * A_DK, (c + 1) * A_DK)
            states.append(st.astype(MXU))
            st = st * decay[c, :, sl] + kv_all[:, cs]
        st_scr[hd] = st
        qd_blocks = jnp.where(in_block, jnp.tile(qd, (1, nc)), zero)
        o = o + lax.dot_general(qd_blocks, jnp.concatenate(states, axis=1), NT_DIMS,
                                preferred_element_type=F32)
        cat_scr[:, sl] = _rms(o, onorm) * gate[:, sl]

    h1 = h + jnp.dot(cat_scr[...].astype(MXU), wout_ref[...], preferred_element_type=F32)
    h1_ref[0] = h1
    _route_tail(h1, (b == 0) & (t == 0), gffn_ref, wr_ref, xext_ref, cls_ref, rank_ref, cnt_ref,
                run_scr, pack_scr)


def _l1_kernel(sinks_ref, h_ref, gmix_ref, wqt_ref, bqt_ref, wk_ref, bk_ref, wvt_ref, bvt_ref,
               wout_ref, bout_ref, gffn_ref, wr_ref,
               h1_ref, xext_ref, cls_ref, rank_ref, cnt_ref,
               qt_scr, k_scr, vt_scr, att_scr, run_scr, pack_scr):
    b = pl.program_id(0)
    t = pl.program_id(1)
    tb = h_ref.shape[1]

    @pl.when(t == 0)
    def _():
        k_scr[:, 0:WINDOW, :] = jnp.zeros((KV_HEADS, WINDOW, HEAD_DIM), k_scr.dtype)
        vt_scr[:, 0:WINDOW] = jnp.zeros((KV_HEADS * HEAD_DIM, WINDOW), vt_scr.dtype)

    h = h_ref[0]
    u = _rms(h, gmix_ref[...]).astype(MXU)
    scale = HEAD_DIM ** -0.5
    qt_scr[...] = ((lax.dot_general(wqt_ref[...], u, NT_DIMS, preferred_element_type=F32)
                    + bqt_ref[...]) * scale).astype(qt_scr.dtype)
    kz = (jnp.dot(u, wk_ref[...], preferred_element_type=F32) + bk_ref[...]).astype(k_scr.dtype)
    for kk in range(KV_HEADS):
        k_scr[kk, WINDOW:, :] = kz[:, kk * HEAD_DIM:(kk + 1) * HEAD_DIM]
    vt_scr[:, WINDOW:] = (lax.dot_general(wvt_ref[...], u, NT_DIMS, preferred_element_type=F32)
                          + bvt_ref[...]).astype(vt_scr.dtype)

    gw = GROUP * WINDOW
    kj = lax.broadcasted_iota(I32, (2 * WINDOW, gw), 0)
    qi = lax.broadcasted_iota(I32, (2 * WINDOW, gw), 1) % WINDOW
    for n in range(tb // WINDOW):
        rs = slice(n * WINDOW, (n + 1) * WINDOW)
        win = slice(n * WINDOW, (n + 2) * WINDOW)
        first_key = 0 if n > 0 else jnp.where(t > 0, 0, WINDOW)
        mask = (kj > jnp.maximum(qi, first_key - 1)) & (kj <= qi + WINDOW)
        for kk in range(KV_HEADS):
            heads = range(kk * GROUP, (kk + 1) * GROUP)
            keys = k_scr[kk, win, :]
            vals_t = vt_scr[kk * HEAD_DIM:(kk + 1) * HEAD_DIM, win]
            q_t = jnp.concatenate(
                [qt_scr[hd * HEAD_DIM:(hd + 1) * HEAD_DIM, rs] for hd in heads], axis=1)
            sink = jnp.concatenate(
                [jnp.full((1, WINDOW), sinks_ref[hd], F32) for hd in heads], axis=1)
            s = jnp.dot(keys, q_t, preferred_element_type=F32)
            s = jnp.where(mask, s, -jnp.inf)
            m = jnp.maximum(jnp.max(s, axis=0, keepdims=True), sink)
            e = jnp.exp(s - m)
            denom = jnp.sum(e, axis=0, keepdims=True) + jnp.exp(sink - m)
            o_t = jnp.dot(vals_t, e.astype(MXU), preferred_element_type=F32) / denom
            for g2 in range(GROUP // 2):
                pair = jnp.concatenate([o_t[:, (2 * g2) * WINDOW:(2 * g2 + 1) * WINDOW],
                                        o_t[:, (2 * g2 + 1) * WINDOW:(2 * g2 + 2) * WINDOW]], axis=0)
                h0 = (kk * GROUP + 2 * g2) * HEAD_DIM
                att_scr[rs, h0:h0 + 2 * HEAD_DIM] = pair.T
    k_scr[:, 0:WINDOW, :] = k_scr[:, tb:tb + WINDOW, :]
    vt_scr[:, 0:WINDOW] = vt_scr[:, tb:tb + WINDOW]

    h1 = (h + jnp.dot(att_scr[...].astype(MXU), wout_ref[...], preferred_element_type=F32)
          + bout_ref[...])
    h1_ref[0] = h1
    _route_tail(h1, (b == 0) & (t == 0), gffn_ref, wr_ref, xext_ref, cls_ref, rank_ref, cnt_ref,
                run_scr, pack_scr)


def _row_copy(src, src_row, dst, dst_row, sem):
    return pltpu.make_async_copy(src.at[pl.ds(src_row, 1)], dst.at[pl.ds(dst_row, 1)], sem)


ISSUE_UNROLL = 8


def _moe_kernel(e1_ref, e2_ref, nused_ref, pos_ref, pad0_ref, pad1_ref,
                x_ref, wg1_ref, wu1_ref, wd1_ref, wg2_ref, wu2_ref, wd2_ref,
                y_ref, src_ref, xbuf, sem, unpack_scr):
    i = pl.program_id(0)
    tm = xbuf.shape[1]
    n_tokens = pos_ref.shape[0]
    nused = nused_ref[0]
    slot = i % 2

    def wait(buf):
        pltpu.make_async_copy(x_ref.at[pl.ds(0, tm)], xbuf.at[buf], sem.at[buf]).wait()

    @pl.when(i == 0)
    def _():
        def pad_class(c, carry):
            def pad_row(r, carry):
                src_ref[r] = 0
                return carry
            return lax.fori_loop(pad0_ref[c], pad1_ref[c], pad_row, carry)

        lax.fori_loop(0, N_CLASS, pad_class, 0)

        def invert(g, carry):
            t0 = g * ISSUE_UNROLL
            for k in range(ISSUE_UNROLL):
                src_ref[pos_ref[t0 + k]] = t0 + k
            return carry

        lax.fori_loop(0, n_tokens // ISSUE_UNROLL, invert, 0)

        def issue(g, carry):
            r0 = pl.multiple_of(g * ISSUE_UNROLL, ISSUE_UNROLL)
            group = xbuf.at[0, pl.ds(r0, ISSUE_UNROLL)]
            for k in range(ISSUE_UNROLL):
                _row_copy(x_ref, src_ref[r0 + k], group, k, sem.at[0]).start()
            return carry

        lax.fori_loop(0, tm // ISSUE_UNROLL, issue, 0)

    @pl.when(i < nused)
    def _():
        wait(slot)
        nchunk = ROW_TILES
        for j in range(nchunk):
            unpack_scr[j] = pltpu.bitcast(xbuf[slot, :, j * 128:(j + 1) * 128], BF16).astype(F32)
        nxt = jnp.minimum(i + 1, nused - 1) * tm

        def issue_quarter(q):
            for r in range(q * tm // 4, (q + 1) * tm // 4):
                group = xbuf.at[1 - slot, pl.ds((r // 8) * 8, 8)]
                _row_copy(x_ref, src_ref[nxt + r], group, r % 8, sem.at[1 - slot]).start()

        xa = [unpack_scr[j, pl.ds(0, tm, stride=2), :] for j in range(nchunk)]
        xb = [unpack_scr[j, pl.ds(1, tm, stride=2), :] for j in range(nchunk)]
        x = jnp.concatenate(xa[:-1] + xb[:-1], axis=1).astype(MXU)
        comb = xa[-1] + xb[-1]
        lane = lax.broadcasted_iota(I32, comb.shape, 1)

        def expert(e, wg_ref, wu_ref, wd_ref, first_quarter):
            cw = jnp.sum(jnp.where(lane == e, comb, 0.0), axis=1, keepdims=True)
            issue_quarter(first_quarter)
            gate = jnp.dot(x, wg_ref[...], preferred_element_type=F32)
            issue_quarter(first_quarter + 1)
            up = jnp.dot(x, wu_ref[...], preferred_element_type=F32)
            hidden = (gate * _sigmoid(gate)) * up
            return cw * jnp.dot(hidden.astype(MXU), wd_ref[...], preferred_element_type=F32)

        y_ref[...] = (expert(e1_ref[i], wg1_ref, wu1_ref, wd1_ref, 0)
                      + expert(e2_ref[i], wg2_ref, wu2_ref, wd2_ref, 2))

    @pl.when(i == nused - 1)
    def _():
        wait(1 - slot)

    @pl.when(i >= nused)
    def _():
        y_ref[...] = jnp.zeros_like(y_ref)


def _ple_kernel(pos_ref, h1_ref, p_ref, y_ref, gple_ref, wgate_ref, wproj_ref, gfin_ref,
                out_ref, ybuf, sem, *, final):
    rows = h1_ref.shape[0]
    i = pl.program_id(0)
    nsteps = pl.num_programs(0)
    slot = i % 2

    def gather(step, buf):
        def issue(g, carry):
            r0 = pl.multiple_of(g * ISSUE_UNROLL, ISSUE_UNROLL)
            group = ybuf.at[buf, pl.ds(r0, ISSUE_UNROLL)]
            for k in range(ISSUE_UNROLL):
                _row_copy(y_ref, pos_ref[step * rows + r0 + k], group, k, sem.at[buf]).start()
            return carry

        lax.fori_loop(0, rows // ISSUE_UNROLL, issue, 0)

    def wait(buf):
        pltpu.make_async_copy(y_ref.at[pl.ds(0, rows)], ybuf.at[buf], sem.at[buf]).wait()

    @pl.when(i == 0)
    def _():
        gather(0, 0)

    wait(slot)
    h2 = h1_ref[...] + ybuf[slot]
    nxt = jnp.minimum(i + 1, nsteps - 1)
    for r in range(rows):
        group = ybuf.at[1 - slot, pl.ds((r // 8) * 8, 8)]
        _row_copy(y_ref, pos_ref[nxt * rows + r], group, r % 8, sem.at[1 - slot]).start()

    proj = jnp.dot(p_ref[...].astype(MXU), wproj_ref[...], preferred_element_type=F32)
    gate = _sigmoid(jnp.dot(_rms(h2, gple_ref[...]).astype(MXU), wgate_ref[...],
                            preferred_element_type=F32))
    h3 = h2 + gate * proj
    if final:
        h3 = _rms(h3, gfin_ref[...])
    out_ref[...] = h3

    @pl.when(i == nsteps - 1)
    def _():
        wait(1 - slot)


def _const_spec(shape):
    nd = len(shape)
    return pl.BlockSpec(shape, lambda *_: (0,) * nd)


def _route_out(bsz, seq, tb):
    nt = seq // tb
    n = bsz * seq
    shapes = [
        jax.ShapeDtypeStruct((bsz, seq, D_MODEL), F32),
        jax.ShapeDtypeStruct((n, ROW_WORDS), U32),
        jax.ShapeDtypeStruct((bsz * nt, 1, tb), I32),
        jax.ShapeDtypeStruct((bsz * nt, 1, tb), I32),
        jax.ShapeDtypeStruct((CLASS_PAD, 128), F32),
    ]
    specs = [
        pl.BlockSpec((1, tb, D_MODEL), lambda b, t, *_: (b, t, 0)),
        pl.BlockSpec((tb, ROW_WORDS), lambda b, t, *_: (b * nt + t, 0)),
        pl.BlockSpec((1, 1, tb), lambda b, t, *_: (b * nt + t, 0, 0)),
        pl.BlockSpec((1, 1, tb), lambda b, t, *_: (b * nt + t, 0, 0)),
        pl.BlockSpec((CLASS_PAD, 128), lambda b, t, *_: (0, 0)),
    ]
    return shapes, specs


def _params():
    return pltpu.CompilerParams(dimension_semantics=("arbitrary", "arbitrary"),
                                vmem_limit_bytes=VMEM_LIMIT)


def _layer0_mixer(h, gmix, win, lb, onorm, convw, wout, gffn, wrt):
    bsz, seq, _ = h.shape
    tb = L0_BLOCK
    assert seq % tb == 0
    shapes, ospecs = _route_out(bsz, seq, tb)
    return pl.pallas_call(
        _l0_kernel,
        grid=(bsz, seq // tb),
        in_specs=[
            pl.BlockSpec((1, tb, D_MODEL), lambda b, t: (b, t, 0)),
            _const_spec((1, D_MODEL)),
            _const_spec((D_MODEL, AB_IN)),
            _const_spec((1, A_WIDTH)),
            _const_spec((1, A_DK)),
            _const_spec((3, A_WIDTH)),
            _const_spec((D_MODEL, D_MODEL)),
            _const_spec((1, D_MODEL)),
            _const_spec((D_MODEL, 128)),
        ],
        out_specs=ospecs,
        out_shape=shapes,
        scratch_shapes=[
            pltpu.VMEM((tb, AB_IN), F32),
            pltpu.VMEM((tb, D_MODEL), F32),
            pltpu.VMEM((A_HEADS, A_DK, A_DK), F32),
            pltpu.VMEM((8, A_WIDTH), F32),
            pltpu.VMEM((CLASS_PAD, 128), F32),
            pltpu.VMEM((ROW_TILES, 2 * tb, 128), F32),
        ],
        compiler_params=_params(),
        name="layer0_mixer",
    )(h, gmix, win, lb, onorm, convw, wout, gffn, wrt)


def _layer1_mixer(h, sinks, gmix, win, bin_, wout, bout, gffn, wrt):
    bsz, seq, _ = h.shape
    tb = L1_BLOCK
    assert seq % tb == 0
    shapes, ospecs = _route_out(bsz, seq, tb)
    qw = Q_HEADS * HEAD_DIM
    vw = KV_HEADS * HEAD_DIM
    wqt, wk, wvt = win[:, :qw].T, win[:, qw:qw + vw], win[:, qw + vw:].T
    bqt, bk, bvt = bin_[:, :qw].reshape(qw, 1), bin_[:, qw:qw + vw], bin_[:, qw + vw:].reshape(vw, 1)
    grid_spec = pltpu.PrefetchScalarGridSpec(
        num_scalar_prefetch=1,
        grid=(bsz, seq // tb),
        in_specs=[
            pl.BlockSpec((1, tb, D_MODEL), lambda b, t, *_: (b, t, 0)),
            _const_spec((1, D_MODEL)),
            _const_spec((qw, D_MODEL)),
            _const_spec((qw, 1)),
            _const_spec((D_MODEL, vw)),
            _const_spec((1, vw)),
            _const_spec((vw, D_MODEL)),
            _const_spec((vw, 1)),
            _const_spec((D_MODEL, D_MODEL)),
            _const_spec((1, D_MODEL)),
            _const_spec((1, D_MODEL)),
            _const_spec((D_MODEL, 128)),
        ],
        out_specs=ospecs,
        scratch_shapes=[
            pltpu.VMEM((qw, tb), MXU),
            pltpu.VMEM((KV_HEADS, tb + WINDOW, HEAD_DIM), MXU),
            pltpu.VMEM((vw, tb + WINDOW), MXU),
            pltpu.VMEM((tb, D_MODEL), F32),
            pltpu.VMEM((CLASS_PAD, 128), F32),
            pltpu.VMEM((ROW_TILES, 2 * tb, 128), F32),
        ],
    )
    return pl.pallas_call(
        _l1_kernel, grid_spec=grid_spec, out_shape=shapes, compiler_params=_params(),
        name="layer1_mixer",
    )(sinks, h, gmix, wqt, bqt, wk, bk, wvt, bvt, wout, bout, gffn, wrt)


def _moe(plan, xext, layer, wg, wu, wd):
    pos, e1, e2, nused, pad0, pad1, n_rows = plan
    tm = MOE_TILE
    n_tiles = n_rows // tm

    def wspec(shape, which):
        return pl.BlockSpec((None, None) + shape,
                            lambda i, e1, e2, *_: (layer, (e1, e2)[which][i], 0, 0))

    grid_spec = pltpu.PrefetchScalarGridSpec(
        num_scalar_prefetch=6,
        grid=(n_tiles,),
        in_specs=[
            pl.BlockSpec(memory_space=pl.ANY),
            wspec((D_MODEL, D_EXPERT), 0), wspec((D_MODEL, D_EXPERT), 0), wspec((D_EXPERT, D_MODEL), 0),
            wspec((D_MODEL, D_EXPERT), 1), wspec((D_MODEL, D_EXPERT), 1), wspec((D_EXPERT, D_MODEL), 1),
        ],
        out_specs=pl.BlockSpec((tm, D_MODEL), lambda i, *_: (i, 0)),
        scratch_shapes=[
            pltpu.SMEM((n_rows,), I32),
            pltpu.VMEM((2, tm, ROW_WORDS), U32),
            pltpu.SemaphoreType.DMA((2,)),
            pltpu.VMEM((ROW_TILES, 2 * tm, 128), F32),
        ],
    )
    return pl.pallas_call(
        _moe_kernel, grid_spec=grid_spec,
        out_shape=jax.ShapeDtypeStruct((n_rows, D_MODEL), F32),
        compiler_params=pltpu.CompilerParams(dimension_semantics=("arbitrary",),
                                             vmem_limit_bytes=VMEM_LIMIT),
        name="moe_experts",
    )(e1, e2, nused, pos, pad0, pad1, xext, wg, wu, wd, wg, wu, wd)


def _combine_ple(pos, h1, p, layer, ys, gple, wgate, wproj, gfin, final):
    n = h1.shape[0]
    blk = MOVE_BLOCK
    assert n % blk == 0
    grid_spec = pltpu.PrefetchScalarGridSpec(
        num_scalar_prefetch=1,
        grid=(n // blk,),
        in_specs=[
            pl.BlockSpec((blk, D_MODEL), lambda i, *_: (i, 0)),
            pl.BlockSpec((None, blk, PLE_DIM), lambda i, *_: (layer, i, 0)),
            pl.BlockSpec(memory_space=pl.ANY),
            _const_spec((1, D_MODEL)),
            _const_spec((D_MODEL, D_MODEL)),
            _const_spec((PLE_DIM, D_MODEL)),
            _const_spec((1, D_MODEL)),
        ],
        out_specs=pl.BlockSpec((blk, D_MODEL), lambda i, *_: (i, 0)),
        scratch_shapes=[pltpu.VMEM((2, blk, D_MODEL), F32), pltpu.SemaphoreType.DMA((2,))],
    )
    return pl.pallas_call(
        functools.partial(_ple_kernel, final=final), grid_spec=grid_spec,
        out_shape=jax.ShapeDtypeStruct((n, D_MODEL), F32),
        compiler_params=pltpu.CompilerParams(dimension_semantics=("arbitrary",),
                                             vmem_limit_bytes=VMEM_LIMIT),
        name="combine_ple",
    )(pos, h1, p, ys, gple, wgate, wproj, gfin)


def _routing_plan(cls, rank, counts, n_tokens):
    tm = MOE_TILE
    n_tiles = n_tokens // tm + N_CLASS
    cnt = counts[:N_CLASS, 0].astype(I32)
    tiles_per_class = (cnt + tm - 1) // tm
    tile_end = jnp.cumsum(tiles_per_class)
    class_row0 = (tile_end - tiles_per_class) * tm
    pos = jnp.take(class_row0, cls.reshape(-1)) + rank.reshape(-1)
    nused = tile_end[-1]
    tidx = jnp.minimum(jnp.arange(n_tiles, dtype=I32), nused - 1)
    tcls = jnp.sum((tidx[:, None] >= tile_end[None, :]).astype(I32), axis=1)
    tcls = jnp.minimum(tcls, N_CLASS - 1)
    lo = jnp.array([a for a, _ in PAIRS], I32)
    hi = jnp.array([b for _, b in PAIRS], I32)
    grp = tcls // len(PAIRS)
    e1 = grp * PER_GROUP + jnp.take(lo, tcls % len(PAIRS))
    e2 = grp * PER_GROUP + jnp.take(hi, tcls % len(PAIRS))
    pad0 = class_row0 + cnt
    pad1 = tile_end * tm
    return (pos.astype(I32), e1.astype(I32), e2.astype(I32), nused.reshape(1).astype(I32),
            pad0.astype(I32), pad1.astype(I32), n_tiles * tm)


def _ffn_and_ple(h1, xext, cls, rank, counts, p, layer, wg, wu, wd, gple, wgate, wproj, gfin,
                 final):
    bsz, seq, _ = h1.shape
    n = bsz * seq
    plan = _routing_plan(cls, rank, counts, n)
    ys = _moe(plan, xext, layer, wg, wu, wd)
    out = _combine_ple(plan[0], h1.reshape(n, D_MODEL), p.reshape(p.shape[0], n, PLE_DIM), layer,
                       ys, gple, wgate, wproj, gfin, final)
    return out.reshape(bsz, seq, D_MODEL)


def kernel(x, p, norm_mix, norm_ffn, norm_ple, norm_final, w_in_ab, hgrn_lb_logits, hgrn_out_norm, conv_w, w_out_ab, w_in_c, b_in_c, sinks, w_out_c, b_out_c, w_router, w_gate_e, w_up_e, w_down_e, w_ple_gate, w_ple_proj):
    depth = p.shape[0]
    lower_bounds = jnp.cumsum(jax.nn.softmax(hgrn_lb_logits.astype(F32), axis=0), axis=0)
    wr32 = w_router.astype(F32)
    wr_hi = wr32.astype(BF16)
    wr_lo = (wr32 - wr_hi.astype(F32)).astype(BF16)
    wrt = jnp.concatenate(
        [wr_hi, wr_lo, jnp.zeros((D_MODEL, 128 - 2 * N_EXPERTS), BF16)], axis=1)
    gfin = norm_final.reshape(1, D_MODEL)
    wg, wu, wd = w_gate_e.astype(MXU), w_up_e.astype(MXU), w_down_e.astype(MXU)
    h = x
    for i in range(depth):
        j = i // 2
        gmix = norm_mix[i].reshape(1, D_MODEL)
        gffn = norm_ffn[i].reshape(1, D_MODEL)
        if i % 2 == 0:
            h1, xext, cls, rank, counts = _layer0_mixer(
                h, gmix, w_in_ab[j].astype(MXU), lower_bounds[i].reshape(1, A_WIDTH),
                hgrn_out_norm[j].reshape(1, A_DK), conv_w[j], w_out_ab[j].astype(MXU), gffn, wrt)
        else:
            h1, xext, cls, rank, counts = _layer1_mixer(
                h, sinks[j].astype(F32), gmix, w_in_c[j].astype(MXU), b_in_c[j].reshape(1, C_IN),
                w_out_c[j].astype(MXU), b_out_c[j].reshape(1, D_MODEL), gffn, wrt)
        h = _ffn_and_ple(
            h1, xext, cls, rank, counts, p, i, wg, wu, wd,
            norm_ple[i].reshape(1, D_MODEL), w_ple_gate[i].astype(MXU), w_ple_proj[i].astype(MXU),
            gfin, final=(i == depth - 1))
    return h
```

```python
import functools

import jax
import jax.numpy as jnp
from jax import lax
from jax.experimental import pallas as pl
from jax.experimental.pallas import tpu as pltpu

F32 = jnp.float32
BF16 = jnp.bfloat16
MXU = jnp.bfloat16
I32 = jnp.int32
U32 = jnp.uint32

D_MODEL = 1024
EPS = 1e-6
A_WIDTH = 512
A_HEADS = 4
A_DK = 128
A_CHUNK = 32
AB_IN = 3584
HEAD_DIM = 64
Q_HEADS = 16
KV_HEADS = 2
GROUP = 8
WINDOW = 128
C_IN = 1280
N_EXPERTS = 16
N_GROUPS = 4
PER_GROUP = 4
D_EXPERT = 512
PLE_DIM = 256

PAIRS = ((0, 1), (0, 2), (0, 3), (1, 2), (1, 3), (2, 3))
N_CLASS = N_GROUPS * len(PAIRS)
CLASS_PAD = 32
X_WORDS = D_MODEL // 2
ROW_WORDS = X_WORDS + 128
ROW_TILES = ROW_WORDS // 128

L0_BLOCK = 256
L1_BLOCK = 512
MOE_TILE = 256
MOE_BUFFERS = 3
MOVE_BLOCK = 512
VMEM_LIMIT = 56 * 1024 * 1024

NT_DIMS = (((1,), (1,)), ((), ()))
TN_DIMS = (((0,), (0,)), ((), ()))


def _rms(x, g):
    ms = jnp.mean(x * x, axis=-1, keepdims=True)
    return x * lax.rsqrt(ms + EPS) * g


def _sigmoid(x):
    return 1.0 / (1.0 + jnp.exp(-x))


def _route_tail(h1, first, gffn_ref, wr_ref, xext_ref, cls_ref, rank_ref, cnt_ref, run_scr,
                pack_scr):
    tb = h1.shape[0]

    @pl.when(first)
    def _():
        run_scr[...] = jnp.zeros_like(run_scr)

    xn = _rms(h1, gffn_ref[...])
    x_hi = xn.astype(BF16)
    x_lo = (xn - x_hi.astype(F32)).astype(BF16)
    prod = (jnp.dot(x_hi, wr_ref[...], preferred_element_type=F32)
            + jnp.dot(x_lo, wr_ref[...], preferred_element_type=F32)).T
    logits = prod[0:N_EXPERTS, :] + prod[N_EXPERTS:2 * N_EXPERTS, :]
    mx = jnp.max(logits, axis=0, keepdims=True)
    ex = jnp.exp(logits - mx)
    sc = ex / jnp.sum(ex, axis=0, keepdims=True)
    rows = [sc[i:i + 1, :] for i in range(N_EXPERTS)]

    gscore = []
    for g in range(N_GROUPS):
        v = rows[PER_GROUP * g:PER_GROUP * (g + 1)]
        best = v[0] + v[1]
        for (i, j) in PAIRS[1:]:
            best = jnp.maximum(best, v[i] + v[j])
        gscore.append(best)
    gmax = jnp.maximum(jnp.maximum(gscore[0], gscore[1]), jnp.maximum(gscore[2], gscore[3]))
    sel = jnp.where(gscore[0] >= gmax, 0, jnp.where(gscore[1] >= gmax, 1,
                                                    jnp.where(gscore[2] >= gmax, 2, 3))).astype(I32)
    v = [jnp.where(sel == 0, rows[i], jnp.where(sel == 1, rows[4 + i],
                                                jnp.where(sel == 2, rows[8 + i], rows[12 + i])))
         for i in range(PER_GROUP)]
    chosen = []
    for i in range(PER_GROUP):
        r = jnp.zeros_like(sel)
        for j in range(PER_GROUP):
            if j == i:
                continue
            ahead = (v[j] >= v[i]) if j < i else (v[j] > v[i])
            r = r + jnp.where(ahead, 1, 0).astype(I32)
        chosen.append(r < 2)
    vsum = jnp.zeros_like(v[0])
    code = jnp.zeros_like(sel)
    for i in range(PER_GROUP):
        vsum = vsum + jnp.where(chosen[i], v[i], 0.0)
        code = code + jnp.where(chosen[i], 1 << i, 0).astype(I32)
    wts = [jnp.where(chosen[i], v[i] / vsum, 0.0) for i in range(PER_GROUP)]
    pair = jnp.where(code == 3, 0, jnp.where(code == 5, 1, jnp.where(code == 9, 2,
                     jnp.where(code == 6, 3, jnp.where(code == 10, 4, 5))))).astype(I32)
    cls = sel * len(PAIRS) + pair

    erow = lax.broadcasted_iota(I32, (128, tb), 0)
    comb_t = jnp.zeros((128, tb), F32)
    for i in range(PER_GROUP):
        comb_t = comb_t + jnp.where(erow == sel * PER_GROUP + i, wts[i], 0.0)
    comb = comb_t.T

    comb_hi = comb.astype(BF16).astype(F32)
    halves = [(xn[:, j * 128:(j + 1) * 128], xn[:, X_WORDS + j * 128:X_WORDS + (j + 1) * 128])
              for j in range(X_WORDS // 128)] + [(comb_hi, comb - comb_hi)]
    for j, (first_half, second_half) in enumerate(halves):
        pack_scr[j, pl.ds(0, tb, stride=2), :] = first_half
        pack_scr[j, pl.ds(1, tb, stride=2), :] = second_half
        xext_ref[:, j * 128:(j + 1) * 128] = pltpu.bitcast(pack_scr[j].astype(BF16), U32)

    crow = lax.broadcasted_iota(I32, (CLASS_PAD, tb), 0)
    onehot = jnp.where(crow == cls, 1.0, 0.0)
    ii = lax.broadcasted_iota(I32, (tb, tb), 0)
    jj = lax.broadcasted_iota(I32, (tb, tb), 1)
    upper = jnp.where(ii < jj, 1.0, 0.0).astype(MXU)
    prefix = jnp.dot(onehot.astype(MXU), upper, preferred_element_type=F32)
    run = run_scr[...]
    rank = jnp.sum(onehot * (prefix + run[:, 0:1]), axis=0, keepdims=True)
    run_new = run + jnp.sum(onehot, axis=1, keepdims=True)
    run_scr[...] = run_new
    cnt_ref[...] = run_new
    cls_ref[0] = cls
    rank_ref[0] = rank.astype(I32)


def _l0_kernel(h_ref, gmix_ref, win_ref, lb_ref, onorm_ref, convw_ref, wout_ref, gffn_ref, wr_ref,
               h1_ref, xext_ref, cls_ref, rank_ref, cnt_ref,
               z_scr, cat_scr, st_scr, carry_scr, run_scr, pack_scr):
    b = pl.program_id(0)
    t = pl.program_id(1)
    tb = h_ref.shape[1]

    @pl.when(t == 0)
    def _():
        st_scr[...] = jnp.zeros_like(st_scr)
        carry_scr[...] = jnp.zeros_like(carry_scr)

    h = h_ref[0]
    u = _rms(h, gmix_ref[...]).astype(MXU)
    z_scr[...] = jnp.dot(u, win_ref[...], preferred_element_type=F32)

    gate_b = z_scr[:, 4 * A_WIDTH:5 * A_WIDTH]
    cb = z_scr[:, 5 * A_WIDTH:6 * A_WIDTH] * z_scr[:, 6 * A_WIDTH:7 * A_WIDTH]
    row = lax.broadcasted_iota(I32, cb.shape, 0)
    prev1 = carry_scr[7:8, :]
    prev2 = carry_scr[6:7, :]
    m1 = jnp.where(row == 0, prev1, pltpu.roll(cb, 1, axis=0))
    m2 = jnp.where(row == 0, prev2, jnp.where(row == 1, prev1, pltpu.roll(cb, 2, axis=0)))
    cw = convw_ref[...]
    cat_scr[:, A_WIDTH:] = gate_b * (cw[0:1, :] * m2 + cw[1:2, :] * m1 + cw[2:3, :] * cb)
    carry_scr[...] = cb[tb - 8:tb, :]

    nc = tb // A_CHUNK
    lb = lb_ref[...]
    onorm = onorm_ref[...]
    zq = z_scr[:, 0:A_WIDTH]
    zf = z_scr[:, A_WIDTH:2 * A_WIDTH]
    zi = z_scr[:, 2 * A_WIDTH:3 * A_WIDTH]
    zg = z_scr[:, 3 * A_WIDTH:4 * A_WIDTH]
    f = lb + (1.0 - lb) * _sigmoid(zf)
    logf = jnp.log(f)
    ri = lax.broadcasted_iota(I32, (tb, tb), 0)
    rj = lax.broadcasted_iota(I32, (tb, tb), 1)
    same_chunk = (ri // A_CHUNK) == (rj // A_CHUNK)
    causal = same_chunk & (ri >= rj)
    tri = jnp.where(causal, 1.0, 0.0).astype(BF16)
    p1 = logf.astype(BF16)
    r1 = logf - p1.astype(F32)
    p2 = r1.astype(BF16)
    p3 = (r1 - p2.astype(F32)).astype(BF16)
    pieces = jnp.dot(tri, jnp.concatenate([p1, p2, p3], axis=1), preferred_element_type=F32)
    bcum = (pieces[:, 0:A_WIDTH] + pieces[:, A_WIDTH:2 * A_WIDTH]) + pieces[:, 2 * A_WIDTH:]
    blast3 = bcum.reshape(nc, A_CHUNK, A_WIDTH)[:, A_CHUNK - 1:A_CHUNK, :]
    blast = jnp.broadcast_to(blast3, (nc, A_CHUNK, A_WIDTH)).reshape(tb, A_WIDTH)
    k = 1.0 - f
    q_dec = ((zq * _sigmoid(zq)) * jnp.exp(bcum)).astype(MXU)
    k_inv = (k * jnp.exp(-bcum)).astype(MXU)
    k_tail = (k * jnp.exp(blast - bcum)).astype(MXU)
    decay = jnp.exp(blast3)
    gate = _sigmoid(zg)
    vals = zi.astype(MXU)
    in_block = (lax.broadcasted_iota(I32, (tb, nc * A_DK), 0) // A_CHUNK
                == lax.broadcasted_iota(I32, (tb, nc * A_DK), 1) // A_DK)
    zero = jnp.zeros((), MXU)
    for hd in range(A_HEADS):
        sl = slice(hd * A_DK, (hd + 1) * A_DK)
        qd = q_dec[:, sl]
        vv = vals[:, sl]
        scores = lax.dot_general(qd, k_inv[:, sl], NT_DIMS, preferred_element_type=F32)
        scores = jnp.where(causal, scores, 0.0).astype(MXU)
        o = jnp.dot(scores, vv, preferred_element_type=F32)
        kt_blocks = jnp.where(in_block, jnp.tile(k_tail[:, sl], (1, nc)), zero)
        kv_all = lax.dot_general(vv, kt_blocks, TN_DIMS, preferred_element_type=F32)
        st = st_scr[hd]
        states = []
        for c in range(nc):
            cs = slice(c * A_DK, (c + 1) * A_DK)
            states.append(st.astype(MXU))
            st = st * decay[c, :, sl] + kv_all[:, cs]
        st_scr[hd] = st
        qd_blocks = jnp.where(in_block, jnp.tile(qd, (1, nc)), zero)
        o = o + lax.dot_general(qd_blocks, jnp.concatenate(states, axis=1), NT_DIMS,
                                preferred_element_type=F32)
        cat_scr[:, sl] = _rms(o, onorm) * gate[:, sl]

    h1 = h + jnp.dot(cat_scr[...].astype(MXU), wout_ref[...], preferred_element_type=F32)
    h1_ref[0] = h1
    _route_tail(h1, (b == 0) & (t == 0), gffn_ref, wr_ref, xext_ref, cls_ref, rank_ref, cnt_ref,
                run_scr, pack_scr)


def _l1_kernel(sinks_ref, h_ref, gmix_ref, wqt_ref, bqt_ref, wk_ref, bk_ref, wvt_ref, bvt_ref,
               wout_ref, bout_ref, gffn_ref, wr_ref,
               h1_ref, xext_ref, cls_ref, rank_ref, cnt_ref,
               qt_scr, k_scr, vt_scr, att_scr, run_scr, pack_scr):
    b = pl.program_id(0)
    t = pl.program_id(1)
    tb = h_ref.shape[1]

    @pl.when(t == 0)
    def _():
        k_scr[:, 0:WINDOW, :] = jnp.zeros((KV_HEADS, WINDOW, HEAD_DIM), k_scr.dtype)
        vt_scr[:, 0:WINDOW] = jnp.zeros((KV_HEADS * HEAD_DIM, WINDOW), vt_scr.dtype)

    h = h_ref[0]
    u = _rms(h, gmix_ref[...]).astype(MXU)
    scale = HEAD_DIM ** -0.5
    qt_scr[...] = ((lax.dot_general(wqt_ref[...], u, NT_DIMS, preferred_element_type=F32)
                    + bqt_ref[...]) * scale).astype(qt_scr.dtype)
    kz = (jnp.dot(u, wk_ref[...], preferred_element_type=F32) + bk_ref[...]).astype(k_scr.dtype)
    for kk in range(KV_HEADS):
        k_scr[kk, WINDOW:, :] = kz[:, kk * HEAD_DIM:(kk + 1) * HEAD_DIM]
    vt_scr[:, WINDOW:] = (lax.dot_general(wvt_ref[...], u, NT_DIMS, preferred_element_type=F32)
                          + bvt_ref[...]).astype(vt_scr.dtype)

    gw = GROUP * WINDOW
    kj = lax.broadcasted_iota(I32, (2 * WINDOW, gw), 0)
    qi = lax.broadcasted_iota(I32, (2 * WINDOW, gw), 1) % WINDOW
    for n in range(tb // WINDOW):
        rs = slice(n * WINDOW, (n + 1) * WINDOW)
        win = slice(n * WINDOW, (n + 2) * WINDOW)
        first_key = 0 if n > 0 else jnp.where(t > 0, 0, WINDOW)
        mask = (kj > jnp.maximum(qi, first_key - 1)) & (kj <= qi + WINDOW)
        for kk in range(KV_HEADS):
            heads = range(kk * GROUP, (kk + 1) * GROUP)
            keys = k_scr[kk, win, :]
            vals_t = vt_scr[kk * HEAD_DIM:(kk + 1) * HEAD_DIM, win]
            q_t = jnp.concatenate(
                [qt_scr[hd * HEAD_DIM:(hd + 1) * HEAD_DIM, rs] for hd in heads], axis=1)
            sink = jnp.concatenate(
                [jnp.full((1, WINDOW), sinks_ref[hd], F32) for hd in heads], axis=1)
            s = jnp.dot(keys, q_t, preferred_element_type=F32)
            s = jnp.where(mask, s, -jnp.inf)
            m = jnp.maximum(jnp.max(s, axis=0, keepdims=True), sink)
            e = jnp.exp(s - m)
            denom = jnp.sum(e, axis=0, keepdims=True) + jnp.exp(sink - m)
            o_t = jnp.dot(vals_t, e.astype(MXU), preferred_element_type=F32) / denom
            for g2 in range(GROUP // 2):
                pair = jnp.concatenate([o_t[:, (2 * g2) * WINDOW:(2 * g2 + 1) * WINDOW],
                                        o_t[:, (2 * g2 + 1) * WINDOW:(2 * g2 + 2) * WINDOW]], axis=0)
                h0 = (kk * GROUP + 2 * g2) * HEAD_DIM
                att_scr[rs, h0:h0 + 2 * HEAD_DIM] = pair.T
    k_scr[:, 0:WINDOW, :] = k_scr[:, tb:tb + WINDOW, :]
    vt_scr[:, 0:WINDOW] = vt_scr[:, tb:tb + WINDOW]

    h1 = (h + jnp.dot(att_scr[...].astype(MXU), wout_ref[...], preferred_element_type=F32)
          + bout_ref[...])
    h1_ref[0] = h1
    _route_tail(h1, (b == 0) & (t == 0), gffn_ref, wr_ref, xext_ref, cls_ref, rank_ref, cnt_ref,
                run_scr, pack_scr)


def _row_copy(src, src_row, dst, dst_row, sem):
    return pltpu.make_async_copy(src.at[pl.ds(src_row, 1)], dst.at[pl.ds(dst_row, 1)], sem)


ISSUE_UNROLL = 8


def _moe_kernel(e1_ref, e2_ref, nused_ref, pos_ref, pad0_ref, pad1_ref,
                x_ref, wg1_ref, wu1_ref, wd1_ref, wg2_ref, wu2_ref, wd2_ref,
                y_ref, src_ref, xbuf, sem, unpack_scr):
    i = pl.program_id(0)
    nbuf, tm = xbuf.shape[0], xbuf.shape[1]
    n_tokens = pos_ref.shape[0]
    nused = nused_ref[0]
    slot = i % nbuf
    ahead = nbuf - 1

    def wait(buf):
        pltpu.make_async_copy(x_ref.at[pl.ds(0, tm)], xbuf.at[buf], sem.at[buf]).wait()

    def tile_rows(tile):
        return jnp.minimum(tile, nused - 1) * tm

    @pl.when(i == 0)
    def _():
        def pad_class(c, carry):
            def pad_row(r, carry):
                src_ref[r] = 0
                return carry
            return lax.fori_loop(pad0_ref[c], pad1_ref[c], pad_row, carry)

        lax.fori_loop(0, N_CLASS, pad_class, 0)

        def invert(g, carry):
            t0 = g * ISSUE_UNROLL
            for k in range(ISSUE_UNROLL):
                src_ref[pos_ref[t0 + k]] = t0 + k
            return carry

        lax.fori_loop(0, n_tokens // ISSUE_UNROLL, invert, 0)

        for first in range(ahead):
            row0 = tile_rows(first)

            def issue(g, carry, first=first, row0=row0):
                r0 = pl.multiple_of(g * ISSUE_UNROLL, ISSUE_UNROLL)
                group = xbuf.at[first, pl.ds(r0, ISSUE_UNROLL)]
                for k in range(ISSUE_UNROLL):
                    _row_copy(x_ref, src_ref[row0 + r0 + k], group, k, sem.at[first]).start()
                return carry

            lax.fori_loop(0, tm // ISSUE_UNROLL, issue, 0)

    @pl.when(i < nused)
    def _():
        wait(slot)
        nchunk = ROW_TILES
        for j in range(nchunk):
            unpack_scr[j] = pltpu.bitcast(xbuf[slot, :, j * 128:(j + 1) * 128], BF16).astype(F32)
        nxt = tile_rows(i + ahead)
        nslot = (i + ahead) % nbuf
        for r in range(tm):
            group = xbuf.at[nslot, pl.ds((r // 8) * 8, 8)]
            _row_copy(x_ref, src_ref[nxt + r], group, r % 8, sem.at[nslot]).start()

        xa = [unpack_scr[j, pl.ds(0, tm, stride=2), :] for j in range(nchunk)]
        xb = [unpack_scr[j, pl.ds(1, tm, stride=2), :] for j in range(nchunk)]
        x = jnp.concatenate(xa[:-1] + xb[:-1], axis=1).astype(MXU)
        comb = xa[-1] + xb[-1]
        lane = lax.broadcasted_iota(I32, comb.shape, 1)

        def expert(e, wg_ref, wu_ref, wd_ref):
            cw = jnp.sum(jnp.where(lane == e, comb, 0.0), axis=1, keepdims=True)
            gate = jnp.dot(x, wg_ref[...], preferred_element_type=F32)
            up = jnp.dot(x, wu_ref[...], preferred_element_type=F32)
            hidden = (gate * _sigmoid(gate)) * up
            return cw * jnp.dot(hidden.astype(MXU), wd_ref[...], preferred_element_type=F32)

        y_ref[...] = (expert(e1_ref[i], wg1_ref, wu1_ref, wd1_ref)
                      + expert(e2_ref[i], wg2_ref, wu2_ref, wd2_ref))

    @pl.when(i == nused - 1)
    def _():
        for k in range(1, nbuf):
            wait((i + k) % nbuf)

    @pl.when(i >= nused)
    def _():
        y_ref[...] = jnp.zeros_like(y_ref)


def _ple_kernel(pos_ref, h1_ref, p_ref, y_ref, gple_ref, wgate_ref, wproj_ref, gfin_ref,
                out_ref, ybuf, sem, *, final):
    rows = h1_ref.shape[0]
    i = pl.program_id(0)
    nsteps = pl.num_programs(0)
    slot = i % 2

    def gather(step, buf):
        def issue(g, carry):
            r0 = pl.multiple_of(g * ISSUE_UNROLL, ISSUE_UNROLL)
            group = ybuf.at[buf, pl.ds(r0, ISSUE_UNROLL)]
            for k in range(ISSUE_UNROLL):
                _row_copy(y_ref, pos_ref[step * rows + r0 + k], group, k, sem.at[buf]).start()
            return carry

        lax.fori_loop(0, rows // ISSUE_UNROLL, issue, 0)

    def wait(buf):
        pltpu.make_async_copy(y_ref.at[pl.ds(0, rows)], ybuf.at[buf], sem.at[buf]).wait()

    @pl.when(i == 0)
    def _():
        gather(0, 0)

    wait(slot)
    h2 = h1_ref[...] + ybuf[slot]
    nxt = jnp.minimum(i + 1, nsteps - 1)
    for r in range(rows):
        group = ybuf.at[1 - slot, pl.ds((r // 8) * 8, 8)]
        _row_copy(y_ref, pos_ref[nxt * rows + r], group, r % 8, sem.at[1 - slot]).start()

    proj = jnp.dot(p_ref[...].astype(MXU), wproj_ref[...], preferred_element_type=F32)
    gate = _sigmoid(jnp.dot(_rms(h2, gple_ref[...]).astype(MXU), wgate_ref[...],
                            preferred_element_type=F32))
    h3 = h2 + gate * proj
    if final:
        h3 = _rms(h3, gfin_ref[...])
    out_ref[...] = h3

    @pl.when(i == nsteps - 1)
    def _():
        wait(1 - slot)


def _const_spec(shape):
    nd = len(shape)
    return pl.BlockSpec(shape, lambda *_: (0,) * nd)


def _route_out(bsz, seq, tb):
    nt = seq // tb
    n = bsz * seq
    shapes = [
        jax.ShapeDtypeStruct((bsz, seq, D_MODEL), F32),
        jax.ShapeDtypeStruct((n, ROW_WORDS), U32),
        jax.ShapeDtypeStruct((bsz * nt, 1, tb), I32),
        jax.ShapeDtypeStruct((bsz * nt, 1, tb), I32),
        jax.ShapeDtypeStruct((CLASS_PAD, 128), F32),
    ]
    specs = [
        pl.BlockSpec((1, tb, D_MODEL), lambda b, t, *_: (b, t, 0)),
        pl.BlockSpec((tb, ROW_WORDS), lambda b, t, *_: (b * nt + t, 0)),
        pl.BlockSpec((1, 1, tb), lambda b, t, *_: (b * nt + t, 0, 0)),
        pl.BlockSpec((1, 1, tb), lambda b, t, *_: (b * nt + t, 0, 0)),
        pl.BlockSpec((CLASS_PAD, 128), lambda b, t, *_: (0, 0)),
    ]
    return shapes, specs


def _params():
    return pltpu.CompilerParams(dimension_semantics=("arbitrary", "arbitrary"),
                                vmem_limit_bytes=VMEM_LIMIT)


def _layer0_mixer(h, gmix, win, lb, onorm, convw, wout, gffn, wrt):
    bsz, seq, _ = h.shape
    tb = L0_BLOCK
    assert seq % tb == 0
    shapes, ospecs = _route_out(bsz, seq, tb)
    return pl.pallas_call(
        _l0_kernel,
        grid=(bsz, seq // tb),
        in_specs=[
            pl.BlockSpec((1, tb, D_MODEL), lambda b, t: (b, t, 0)),
            _const_spec((1, D_MODEL)),
            _const_spec((D_MODEL, AB_IN)),
            _const_spec((1, A_WIDTH)),
            _const_spec((1, A_DK)),
            _const_spec((3, A_WIDTH)),
            _const_spec((D_MODEL, D_MODEL)),
            _const_spec((1, D_MODEL)),
            _const_spec((D_MODEL, 128)),
        ],
        out_specs=ospecs,
        out_shape=shapes,
        scratch_shapes=[
            pltpu.VMEM((tb, AB_IN), F32),
            pltpu.VMEM((tb, D_MODEL), F32),
            pltpu.VMEM((A_HEADS, A_DK, A_DK), F32),
            pltpu.VMEM((8, A_WIDTH), F32),
            pltpu.VMEM((CLASS_PAD, 128), F32),
            pltpu.VMEM((ROW_TILES, 2 * tb, 128), F32),
        ],
        compiler_params=_params(),
        name="layer0_mixer",
    )(h, gmix, win, lb, onorm, convw, wout, gffn, wrt)


def _layer1_mixer(h, sinks, gmix, win, bin_, wout, bout, gffn, wrt):
    bsz, seq, _ = h.shape
    tb = L1_BLOCK
    assert seq % tb == 0
    shapes, ospecs = _route_out(bsz, seq, tb)
    qw = Q_HEADS * HEAD_DIM
    vw = KV_HEADS * HEAD_DIM
    wqt, wk, wvt = win[:, :qw].T, win[:, qw:qw + vw], win[:, qw + vw:].T
    bqt, bk, bvt = bin_[:, :qw].reshape(qw, 1), bin_[:, qw:qw + vw], bin_[:, qw + vw:].reshape(vw, 1)
    grid_spec = pltpu.PrefetchScalarGridSpec(
        num_scalar_prefetch=1,
        grid=(bsz, seq // tb),
        in_specs=[
            pl.BlockSpec((1, tb, D_MODEL), lambda b, t, *_: (b, t, 0)),
            _const_spec((1, D_MODEL)),
            _const_spec((qw, D_MODEL)),
            _const_spec((qw, 1)),
            _const_spec((D_MODEL, vw)),
            _const_spec((1, vw)),
            _const_spec((vw, D_MODEL)),
            _const_spec((vw, 1)),
            _const_spec((D_MODEL, D_MODEL)),
            _const_spec((1, D_MODEL)),
            _const_spec((1, D_MODEL)),
            _const_spec((D_MODEL, 128)),
        ],
        out_specs=ospecs,
        scratch_shapes=[
            pltpu.VMEM((qw, tb), MXU),
            pltpu.VMEM((KV_HEADS, tb + WINDOW, HEAD_DIM), MXU),
            pltpu.VMEM((vw, tb + WINDOW), MXU),
            pltpu.VMEM((tb, D_MODEL), F32),
            pltpu.VMEM((CLASS_PAD, 128), F32),
            pltpu.VMEM((ROW_TILES, 2 * tb, 128), F32),
        ],
    )
    return pl.pallas_call(
        _l1_kernel, grid_spec=grid_spec, out_shape=shapes, compiler_params=_params(),
        name="layer1_mixer",
    )(sinks, h, gmix, wqt, bqt, wk, bk, wvt, bvt, wout, bout, gffn, wrt)


def _moe(plan, xext, layer, wg, wu, wd):
    pos, e1, e2, nused, pad0, pad1, n_rows = plan
    tm = MOE_TILE
    n_tiles = n_rows // tm

    def wspec(shape, which):
        return pl.BlockSpec((None, None) + shape,
                            lambda i, e1, e2, *_: (layer, (e1, e2)[which][i], 0, 0))

    grid_spec = pltpu.PrefetchScalarGridSpec(
        num_scalar_prefetch=6,
        grid=(n_tiles,),
        in_specs=[
            pl.BlockSpec(memory_space=pl.ANY),
            wspec((D_MODEL, D_EXPERT), 0), wspec((D_MODEL, D_EXPERT), 0), wspec((D_EXPERT, D_MODEL), 0),
            wspec((D_MODEL, D_EXPERT), 1), wspec((D_MODEL, D_EXPERT), 1), wspec((D_EXPERT, D_MODEL), 1),
        ],
        out_specs=pl.BlockSpec((tm, D_MODEL), lambda i, *_: (i, 0)),
        scratch_shapes=[
            pltpu.SMEM((n_rows,), I32),
            pltpu.VMEM((MOE_BUFFERS, tm, ROW_WORDS), U32),
            pltpu.SemaphoreType.DMA((MOE_BUFFERS,)),
            pltpu.VMEM((ROW_TILES, 2 * tm, 128), F32),
        ],
    )
    return pl.pallas_call(
        _moe_kernel, grid_spec=grid_spec,
        out_shape=jax.ShapeDtypeStruct((n_rows, D_MODEL), F32),
        compiler_params=pltpu.CompilerParams(dimension_semantics=("arbitrary",),
                                             vmem_limit_bytes=VMEM_LIMIT),
        name="moe_experts",
    )(e1, e2, nused, pos, pad0, pad1, xext, wg, wu, wd, wg, wu, wd)


def _combine_ple(pos, h1, p, layer, ys, gple, wgate, wproj, gfin, final):
    n = h1.shape[0]
    blk = MOVE_BLOCK
    assert n % blk == 0
    grid_spec = pltpu.PrefetchScalarGridSpec(
        num_scalar_prefetch=1,
        grid=(n // blk,),
        in_specs=[
            pl.BlockSpec((blk, D_MODEL), lambda i, *_: (i, 0)),
            pl.BlockSpec((None, blk, PLE_DIM), lambda i, *_: (layer, i, 0)),
            pl.BlockSpec(memory_space=pl.ANY),
            _const_spec((1, D_MODEL)),
            _const_spec((D_MODEL, D_MODEL)),
            _const_spec((PLE_DIM, D_MODEL)),
            _const_spec((1, D_MODEL)),
        ],
        out_specs=pl.BlockSpec((blk, D_MODEL), lambda i, *_: (i, 0)),
        scratch_shapes=[pltpu.VMEM((2, blk, D_MODEL), F32), pltpu.SemaphoreType.DMA((2,))],
    )
    return pl.pallas_call(
        functools.partial(_ple_kernel, final=final), grid_spec=grid_spec,
        out_shape=jax.ShapeDtypeStruct((n, D_MODEL), F32),
        compiler_params=pltpu.CompilerParams(dimension_semantics=("arbitrary",),
                                             vmem_limit_bytes=VMEM_LIMIT),
        name="combine_ple",
    )(pos, h1, p, ys, gple, wgate, wproj, gfin)


def _routing_plan(cls, rank, counts, n_tokens):
    tm = MOE_TILE
    n_tiles = n_tokens // tm + N_CLASS
    cnt = counts[:N_CLASS, 0].astype(I32)
    tiles_per_class = (cnt + tm - 1) // tm
    tile_end = jnp.cumsum(tiles_per_class)
    class_row0 = (tile_end - tiles_per_class) * tm
    pos = jnp.take(class_row0, cls.reshape(-1)) + rank.reshape(-1)
    nused = tile_end[-1]
    tidx = jnp.minimum(jnp.arange(n_tiles, dtype=I32), nused - 1)
    tcls = jnp.sum((tidx[:, None] >= tile_end[None, :]).astype(I32), axis=1)
    tcls = jnp.minimum(tcls, N_CLASS - 1)
    lo = jnp.array([a for a, _ in PAIRS], I32)
    hi = jnp.array([b for _, b in PAIRS], I32)
    grp = tcls // len(PAIRS)
    e1 = grp * PER_GROUP + jnp.take(lo, tcls % len(PAIRS))
    e2 = grp * PER_GROUP + jnp.take(hi, tcls % len(PAIRS))
    pad0 = class_row0 + cnt
    pad1 = tile_end * tm
    return (pos.astype(I32), e1.astype(I32), e2.astype(I32), nused.reshape(1).astype(I32),
            pad0.astype(I32), pad1.astype(I32), n_tiles * tm)


def _ffn_and_ple(h1, xext, cls, rank, counts, p, layer, wg, wu, wd, gple, wgate, wproj, gfin,
                 final):
    bsz, seq, _ = h1.shape
    n = bsz * seq
    plan = _routing_plan(cls, rank, counts, n)
    ys = _moe(plan, xext, layer, wg, wu, wd)
    out = _combine_ple(plan[0], h1.reshape(n, D_MODEL), p.reshape(p.shape[0], n, PLE_DIM), layer,
                       ys, gple, wgate, wproj, gfin, final)
    return out.reshape(bsz, seq, D_MODEL)


def kernel(x, p, norm_mix, norm_ffn, norm_ple, norm_final, w_in_ab, hgrn_lb_logits, hgrn_out_norm, conv_w, w_out_ab, w_in_c, b_in_c, sinks, w_out_c, b_out_c, w_router, w_gate_e, w_up_e, w_down_e, w_ple_gate, w_ple_proj):
    depth = p.shape[0]
    lower_bounds = jnp.cumsum(jax.nn.softmax(hgrn_lb_logits.astype(F32), axis=0), axis=0)
    wr32 = w_router.astype(F32)
    wr_hi = wr32.astype(BF16)
    wr_lo = (wr32 - wr_hi.astype(F32)).astype(BF16)
    wrt = jnp.concatenate(
        [wr_hi, wr_lo, jnp.zeros((D_MODEL, 128 - 2 * N_EXPERTS), BF16)], axis=1)
    gfin = norm_final.reshape(1, D_MODEL)
    wg, wu, wd = w_gate_e.astype(MXU), w_up_e.astype(MXU), w_down_e.astype(MXU)
    h = x
    for i in range(depth):
        j = i // 2
        gmix = norm_mix[i].reshape(1, D_MODEL)
        gffn = norm_ffn[i].reshape(1, D_MODEL)
        if i % 2 == 0:
            h1, xext, cls, rank, counts = _layer0_mixer(
                h, gmix, w_in_ab[j].astype(MXU), lower_bounds[i].reshape(1, A_WIDTH),
                hgrn_out_norm[j].reshape(1, A_DK), conv_w[j], w_out_ab[j].astype(MXU), gffn, wrt)
        else:
            h1, xext, cls, rank, counts = _layer1_mixer(
                h, sinks[j].astype(F32), gmix, w_in_c[j].astype(MXU), b_in_c[j].reshape(1, C_IN),
                w_out_c[j].astype(MXU), b_out_c[j].reshape(1, D_MODEL), gffn, wrt)
        h = _ffn_and_ple(
            h1, xext, cls, rank, counts, p, i, wg, wu, wd,
            norm_ple[i].reshape(1, D_MODEL), w_ple_gate[i].astype(MXU), w_ple_proj[i].astype(MXU),
            gfin, final=(i == depth - 1))
    return h
```

```python
import functools

import jax
import jax.numpy as jnp
from jax import lax
from jax.experimental import pallas as pl
from jax.experimental.pallas import tpu as pltpu

F32 = jnp.float32
BF16 = jnp.bfloat16
MXU = jnp.bfloat16
I32 = jnp.int32
U32 = jnp.uint32

D_MODEL = 1024
EPS = 1e-6
A_WIDTH = 512
A_HEADS = 4
A_DK = 128
A_CHUNK = 32
AB_IN = 3584
HEAD_DIM = 64
Q_HEADS = 16
KV_HEADS = 2
GROUP = 8
WINDOW = 128
C_IN = 1280
N_EXPERTS = 16
N_GROUPS = 4
PER_GROUP = 4
D_EXPERT = 512
PLE_DIM = 256

PAIRS = ((0, 1), (0, 2), (0, 3), (1, 2), (1, 3), (2, 3))
N_CLASS = N_GROUPS * len(PAIRS)
CLASS_PAD = 32
X_WORDS = D_MODEL // 2
ROW_WORDS = X_WORDS + 128
ROW_TILES = ROW_WORDS // 128

L0_BLOCK = 256
L1_BLOCK = 512
MOE_TILE = 256
MOE_BUFFERS = 3
MOVE_BLOCK = 512
VMEM_LIMIT = 56 * 1024 * 1024

NT_DIMS = (((1,), (1,)), ((), ()))
TN_DIMS = (((0,), (0,)), ((), ()))


def _rms(x, g):
    ms = jnp.mean(x * x, axis=-1, keepdims=True)
    return x * lax.rsqrt(ms + EPS) * g


def _sigmoid(x):
    return 1.0 / (1.0 + jnp.exp(-x))


def _route_tail(h1, first, gffn_ref, wr_ref, xext_ref, cls_ref, rank_ref, cnt_ref, run_scr,
                pack_scr):
    tb = h1.shape[0]

    @pl.when(first)
    def _():
        run_scr[...] = jnp.zeros_like(run_scr)

    xn = _rms(h1, gffn_ref[...])
    x_hi = xn.astype(BF16)
    x_lo = (xn - x_hi.astype(F32)).astype(BF16)
    prod = (jnp.dot(x_hi, wr_ref[...], preferred_element_type=F32)
            + jnp.dot(x_lo, wr_ref[...], preferred_element_type=F32)).T
    logits = prod[0:N_EXPERTS, :] + prod[N_EXPERTS:2 * N_EXPERTS, :]
    mx = jnp.max(logits, axis=0, keepdims=True)
    ex = jnp.exp(logits - mx)
    sc = ex / jnp.sum(ex, axis=0, keepdims=True)
    rows = [sc[i:i + 1, :] for i in range(N_EXPERTS)]

    gscore = []
    for g in range(N_GROUPS):
        v = rows[PER_GROUP * g:PER_GROUP * (g + 1)]
        best = v[0] + v[1]
        for (i, j) in PAIRS[1:]:
            best = jnp.maximum(best, v[i] + v[j])
        gscore.append(best)
    gmax = jnp.maximum(jnp.maximum(gscore[0], gscore[1]), jnp.maximum(gscore[2], gscore[3]))
    sel = jnp.where(gscore[0] >= gmax, 0, jnp.where(gscore[1] >= gmax, 1,
                                                    jnp.where(gscore[2] >= gmax, 2, 3))).astype(I32)
    v = [jnp.where(sel == 0, rows[i], jnp.where(sel == 1, rows[4 + i],
                                                jnp.where(sel == 2, rows[8 + i], rows[12 + i])))
         for i in range(PER_GROUP)]
    chosen = []
    for i in range(PER_GROUP):
        r = jnp.zeros_like(sel)
        for j in range(PER_GROUP):
            if j == i:
                continue
            ahead = (v[j] >= v[i]) if j < i else (v[j] > v[i])
            r = r + jnp.where(ahead, 1, 0).astype(I32)
        chosen.append(r < 2)
    vsum = jnp.zeros_like(v[0])
    code = jnp.zeros_like(sel)
    for i in range(PER_GROUP):
        vsum = vsum + jnp.where(chosen[i], v[i], 0.0)
        code = code + jnp.where(chosen[i], 1 << i, 0).astype(I32)
    wts = [jnp.where(chosen[i], v[i] / vsum, 0.0) for i in range(PER_GROUP)]
    pair = jnp.where(code == 3, 0, jnp.where(code == 5, 1, jnp.where(code == 9, 2,
                     jnp.where(code == 6, 3, jnp.where(code == 10, 4, 5))))).astype(I32)
    cls = sel * len(PAIRS) + pair

    erow = lax.broadcasted_iota(I32, (128, tb), 0)
    comb_t = jnp.zeros((128, tb), F32)
    for i in range(PER_GROUP):
        comb_t = comb_t + jnp.where(erow == sel * PER_GROUP + i, wts[i], 0.0)
    comb = comb_t.T

    comb_hi = comb.astype(BF16).astype(F32)
    halves = [(xn[:, j * 128:(j + 1) * 128], xn[:, X_WORDS + j * 128:X_WORDS + (j + 1) * 128])
              for j in range(X_WORDS // 128)] + [(comb_hi, comb - comb_hi)]
    for j, (first_half, second_half) in enumerate(halves):
        pack_scr[j, pl.ds(0, tb, stride=2), :] = first_half
        pack_scr[j, pl.ds(1, tb, stride=2), :] = second_half
        xext_ref[:, j * 128:(j + 1) * 128] = pltpu.bitcast(pack_scr[j].astype(BF16), U32)

    crow = lax.broadcasted_iota(I32, (CLASS_PAD, tb), 0)
    onehot = jnp.where(crow == cls, 1.0, 0.0)
    ii = lax.broadcasted_iota(I32, (tb, tb), 0)
    jj = lax.broadcasted_iota(I32, (tb, tb), 1)
    upper = jnp.where(ii < jj, 1.0, 0.0).astype(MXU)
    prefix = jnp.dot(onehot.astype(MXU), upper, preferred_element_type=F32)
    run = run_scr[...]
    rank = jnp.sum(onehot * (prefix + run[:, 0:1]), axis=0, keepdims=True)
    run_new = run + jnp.sum(onehot, axis=1, keepdims=True)
    run_scr[...] = run_new
    cnt_ref[...] = run_new
    cls_ref[0] = cls
    rank_ref[0] = rank.astype(I32)


def _l0_kernel(h_ref, gmix_ref, win_ref, lb_ref, onorm_ref, convw_ref, wout_ref, gffn_ref, wr_ref,
               h1_ref, xext_ref, cls_ref, rank_ref, cnt_ref,
               z_scr, cat_scr, st_scr, carry_scr, run_scr, pack_scr):
    b = pl.program_id(0)
    t = pl.program_id(1)
    tb = h_ref.shape[1]

    @pl.when(t == 0)
    def _():
        st_scr[...] = jnp.zeros_like(st_scr)
        carry_scr[...] = jnp.zeros_like(carry_scr)

    h = h_ref[0]
    u = _rms(h, gmix_ref[...]).astype(MXU)
    z_scr[...] = jnp.dot(u, win_ref[...], preferred_element_type=F32)

    gate_b = z_scr[:, 4 * A_WIDTH:5 * A_WIDTH]
    cb = z_scr[:, 5 * A_WIDTH:6 * A_WIDTH] * z_scr[:, 6 * A_WIDTH:7 * A_WIDTH]
    row = lax.broadcasted_iota(I32, cb.shape, 0)
    prev1 = carry_scr[7:8, :]
    prev2 = carry_scr[6:7, :]
    m1 = jnp.where(row == 0, prev1, pltpu.roll(cb, 1, axis=0))
    m2 = jnp.where(row == 0, prev2, jnp.where(row == 1, prev1, pltpu.roll(cb, 2, axis=0)))
    cw = convw_ref[...]
    cat_scr[:, A_WIDTH:] = gate_b * (cw[0:1, :] * m2 + cw[1:2, :] * m1 + cw[2:3, :] * cb)
    carry_scr[...] = cb[tb - 8:tb, :]

    nc = tb // A_CHUNK
    lb = lb_ref[...]
    onorm = onorm_ref[...]
    zq = z_scr[:, 0:A_WIDTH]
    zf = z_scr[:, A_WIDTH:2 * A_WIDTH]
    zi = z_scr[:, 2 * A_WIDTH:3 * A_WIDTH]
    zg = z_scr[:, 3 * A_WIDTH:4 * A_WIDTH]
    f = lb + (1.0 - lb) * _sigmoid(zf)
    logf = jnp.log(f)
    ri = lax.broadcasted_iota(I32, (tb, tb), 0)
    rj = lax.broadcasted_iota(I32, (tb, tb), 1)
    same_chunk = (ri // A_CHUNK) == (rj // A_CHUNK)
    causal = same_chunk & (ri >= rj)
    tri = jnp.where(causal, 1.0, 0.0).astype(BF16)
    p1 = logf.astype(BF16)
    r1 = logf - p1.astype(F32)
    p2 = r1.astype(BF16)
    p3 = (r1 - p2.astype(F32)).astype(BF16)
    pieces = jnp.dot(tri, jnp.concatenate([p1, p2, p3], axis=1), preferred_element_type=F32)
    bcum = (pieces[:, 0:A_WIDTH] + pieces[:, A_WIDTH:2 * A_WIDTH]) + pieces[:, 2 * A_WIDTH:]
    blast3 = bcum.reshape(nc, A_CHUNK, A_WIDTH)[:, A_CHUNK - 1:A_CHUNK, :]
    blast = jnp.broadcast_to(blast3, (nc, A_CHUNK, A_WIDTH)).reshape(tb, A_WIDTH)
    k = 1.0 - f
    q_dec = ((zq * _sigmoid(zq)) * jnp.exp(bcum)).astype(MXU)
    k_inv = (k * jnp.exp(-bcum)).astype(MXU)
    k_tail = (k * jnp.exp(blast - bcum)).astype(MXU)
    decay = jnp.exp(blast3)
    gate = _sigmoid(zg)
    vals = zi.astype(MXU)
    in_block = (lax.broadcasted_iota(I32, (tb, nc * A_DK), 0) // A_CHUNK
                == lax.broadcasted_iota(I32, (tb, nc * A_DK), 1) // A_DK)
    zero = jnp.zeros((), MXU)
    for hd in range(A_HEADS):
        sl = slice(hd * A_DK, (hd + 1) * A_DK)
        qd = q_dec[:, sl]
        vv = vals[:, sl]
        scores = lax.dot_general(qd, k_inv[:, sl], NT_DIMS, preferred_element_type=F32)
        scores = jnp.where(causal, scores, 0.0).astype(MXU)
        o = jnp.dot(scores, vv, preferred_element_type=F32)
        kt_blocks = jnp.where(in_block, jnp.tile(k_tail[:, sl], (1, nc)), zero)
        kv_all = lax.dot_general(vv, kt_blocks, TN_DIMS, preferred_element_type=F32)
        st = st_scr[hd]
        states = []
        for c in range(nc):
            cs = slice(c * A_DK, (c + 1) * A_DK)
            states.append(st.astype(MXU))
            st = st * decay[c, :, sl] + kv_all[:, cs]
        st_scr[hd] = st
        qd_blocks = jnp.where(in_block, jnp.tile(qd, (1, nc)), zero)
        o = o + lax.dot_general(qd_blocks, jnp.concatenate(states, axis=1), NT_DIMS,
                                preferred_element_type=F32)
        cat_scr[:, sl] = _rms(o, onorm) * gate[:, sl]

    h1 = h + jnp.dot(cat_scr[...].astype(MXU), wout_ref[...], preferred_element_type=F32)
    h1_ref[0] = h1
    _route_tail(h1, (b == 0) & (t == 0), gffn_ref, wr_ref, xext_ref, cls_ref, rank_ref, cnt_ref,
                run_scr, pack_scr)


def _l1_kernel(sinks_ref, h_ref, gmix_ref, wqt_ref, bqt_ref, wk_ref, bk_ref, wvt_ref, bvt_ref,
               wout_ref, bout_ref, gffn_ref, wr_ref,
               h1_ref, xext_ref, cls_ref, rank_ref, cnt_ref,
               qt_scr, k_scr, vt_scr, att_scr, run_scr, pack_scr):
    b = pl.program_id(0)
    t = pl.program_id(1)
    tb = h_ref.shape[1]

    @pl.when(t == 0)
    def _():
        k_scr[:, 0:WINDOW, :] = jnp.zeros((KV_HEADS, WINDOW, HEAD_DIM), k_scr.dtype)
        vt_scr[:, 0:WINDOW] = jnp.zeros((KV_HEADS * HEAD_DIM, WINDOW), vt_scr.dtype)

    h = h_ref[0]
    u = _rms(h, gmix_ref[...]).astype(MXU)
    scale = HEAD_DIM ** -0.5
    qt_scr[...] = ((lax.dot_general(wqt_ref[...], u, NT_DIMS, preferred_element_type=F32)
                    + bqt_ref[...]) * scale).astype(qt_scr.dtype)
    kz = (jnp.dot(u, wk_ref[...], preferred_element_type=F32) + bk_ref[...]).astype(k_scr.dtype)
    for kk in range(KV_HEADS):
        k_scr[kk, WINDOW:, :] = kz[:, kk * HEAD_DIM:(kk + 1) * HEAD_DIM]
    vt_scr[:, WINDOW:] = (lax.dot_general(wvt_ref[...], u, NT_DIMS, preferred_element_type=F32)
                          + bvt_ref[...]).astype(vt_scr.dtype)

    gw = GROUP * WINDOW
    kj = lax.broadcasted_iota(I32, (2 * WINDOW, gw), 0)
    qi = lax.broadcasted_iota(I32, (2 * WINDOW, gw), 1) % WINDOW
    for n in range(tb // WINDOW):
        rs = slice(n * WINDOW, (n + 1) * WINDOW)
        win = slice(n * WINDOW, (n + 2) * WINDOW)
        first_key = 0 if n > 0 else jnp.where(t > 0, 0, WINDOW)
        mask = (kj > jnp.maximum(qi, first_key - 1)) & (kj <= qi + WINDOW)
        for kk in range(KV_HEADS):
            heads = range(kk * GROUP, (kk + 1) * GROUP)
            keys = k_scr[kk, win, :]
            vals_t = vt_scr[kk * HEAD_DIM:(kk + 1) * HEAD_DIM, win]
            q_t = jnp.concatenate(
                [qt_scr[hd * HEAD_DIM:(hd + 1) * HEAD_DIM, rs] for hd in heads], axis=1)
            sink = jnp.concatenate(
                [jnp.full((1, WINDOW), sinks_ref[hd], F32) for hd in heads], axis=1)
            s = jnp.dot(keys, q_t, preferred_element_type=F32)
            s = jnp.where(mask, s, -jnp.inf)
            m = jnp.maximum(jnp.max(s, axis=0, keepdims=True), sink)
            e = jnp.exp(s - m)
            denom = jnp.sum(e, axis=0, keepdims=True) + jnp.exp(sink - m)
            o_t = jnp.dot(vals_t, e.astype(MXU), preferred_element_type=F32) / denom
            for g2 in range(GROUP // 2):
                pair = jnp.concatenate([o_t[:, (2 * g2) * WINDOW:(2 * g2 + 1) * WINDOW],
                                        o_t[:, (2 * g2 + 1) * WINDOW:(2 * g2 + 2) * WINDOW]], axis=0)
                h0 = (kk * GROUP + 2 * g2) * HEAD_DIM
                att_scr[rs, h0:h0 + 2 * HEAD_DIM] = pair.T
    k_scr[:, 0:WINDOW, :] = k_scr[:, tb:tb + WINDOW, :]
    vt_scr[:, 0:WINDOW] = vt_scr[:, tb:tb + WINDOW]

    h1 = (h + jnp.dot(att_scr[...].astype(MXU), wout_ref[...], preferred_element_type=F32)
          + bout_ref[...])
    h1_ref[0] = h1
    _route_tail(h1, (b == 0) & (t == 0), gffn_ref, wr_ref, xext_ref, cls_ref, rank_ref, cnt_ref,
                run_scr, pack_scr)


def _row_copy(src, src_row, dst, dst_row, sem):
    return pltpu.make_async_copy(src.at[pl.ds(src_row, 1)], dst.at[pl.ds(dst_row, 1)], sem)


ISSUE_UNROLL = 8
DMA_THREADS = 2


def _moe_kernel(e1_ref, e2_ref, nused_ref, pos_ref, pad0_ref, pad1_ref,
                x_ref, wg1_ref, wu1_ref, wd1_ref, wg2_ref, wu2_ref, wd2_ref,
                y_ref, src_ref, xbuf, sem, unpack_scr):
    i = pl.program_id(0)
    nbuf, tm = xbuf.shape[0], xbuf.shape[1]
    n_tokens = pos_ref.shape[0]
    nused = nused_ref[0]
    slot = i % nbuf
    ahead = nbuf - 1

    def wait(buf):
        pltpu.make_async_copy(x_ref.at[pl.ds(0, tm)], xbuf.at[buf], sem.at[buf]).wait()

    def tile_rows(tile):
        return jnp.minimum(tile, nused - 1) * tm

    @pl.when(i == 0)
    def _():
        def pad_class(c, carry):
            def pad_row(r, carry):
                src_ref[r] = 0
                return carry
            return lax.fori_loop(pad0_ref[c], pad1_ref[c], pad_row, carry)

        lax.fori_loop(0, N_CLASS, pad_class, 0)

        def invert(g, carry):
            t0 = g * ISSUE_UNROLL
            for k in range(ISSUE_UNROLL):
                src_ref[pos_ref[t0 + k]] = t0 + k
            return carry

        lax.fori_loop(0, n_tokens // ISSUE_UNROLL, invert, 0)

        for first in range(ahead):
            row0 = tile_rows(first)

            def issue(g, carry, first=first, row0=row0):
                r0 = pl.multiple_of(g * ISSUE_UNROLL, ISSUE_UNROLL)
                group = xbuf.at[first, pl.ds(r0, ISSUE_UNROLL)]
                for k in range(ISSUE_UNROLL):
                    _row_copy(x_ref, src_ref[row0 + r0 + k], group, k,
                              sem.at[first]).start(priority=k % DMA_THREADS)
                return carry

            lax.fori_loop(0, tm // ISSUE_UNROLL, issue, 0)

    @pl.when(i < nused)
    def _():
        wait(slot)
        nchunk = ROW_TILES
        for j in range(nchunk):
            unpack_scr[j] = pltpu.bitcast(xbuf[slot, :, j * 128:(j + 1) * 128], BF16).astype(F32)
        nxt = tile_rows(i + ahead)
        nslot = (i + ahead) % nbuf
        for r in range(tm):
            group = xbuf.at[nslot, pl.ds((r // 8) * 8, 8)]
            _row_copy(x_ref, src_ref[nxt + r], group, r % 8,
                      sem.at[nslot]).start(priority=r % DMA_THREADS)

        xa = [unpack_scr[j, pl.ds(0, tm, stride=2), :] for j in range(nchunk)]
        xb = [unpack_scr[j, pl.ds(1, tm, stride=2), :] for j in range(nchunk)]
        x = jnp.concatenate(xa[:-1] + xb[:-1], axis=1).astype(MXU)
        comb = xa[-1] + xb[-1]
        lane = lax.broadcasted_iota(I32, comb.shape, 1)

        def expert(e, wg_ref, wu_ref, wd_ref):
            cw = jnp.sum(jnp.where(lane == e, comb, 0.0), axis=1, keepdims=True)
            gate = jnp.dot(x, wg_ref[...], preferred_element_type=F32)
            up = jnp.dot(x, wu_ref[...], preferred_element_type=F32)
            hidden = (gate * _sigmoid(gate)) * up
            return cw * jnp.dot(hidden.astype(MXU), wd_ref[...], preferred_element_type=F32)

        y_ref[...] = (expert(e1_ref[i], wg1_ref, wu1_ref, wd1_ref)
                      + expert(e2_ref[i], wg2_ref, wu2_ref, wd2_ref))

    @pl.when(i == nused - 1)
    def _():
        for k in range(1, nbuf):
            wait((i + k) % nbuf)

    @pl.when(i >= nused)
    def _():
        y_ref[...] = jnp.zeros_like(y_ref)


def _ple_kernel(pos_ref, h1_ref, p_ref, y_ref, gple_ref, wgate_ref, wproj_ref, gfin_ref,
                out_ref, ybuf, sem, *, final):
    rows = h1_ref.shape[0]
    i = pl.program_id(0)
    nsteps = pl.num_programs(0)
    slot = i % 2

    def gather(step, buf):
        def issue(g, carry):
            r0 = pl.multiple_of(g * ISSUE_UNROLL, ISSUE_UNROLL)
            group = ybuf.at[buf, pl.ds(r0, ISSUE_UNROLL)]
            for k in range(ISSUE_UNROLL):
                _row_copy(y_ref, pos_ref[step * rows + r0 + k], group, k,
                          sem.at[buf]).start(priority=k % DMA_THREADS)
            return carry

        lax.fori_loop(0, rows // ISSUE_UNROLL, issue, 0)

    def wait(buf):
        pltpu.make_async_copy(y_ref.at[pl.ds(0, rows)], ybuf.at[buf], sem.at[buf]).wait()

    @pl.when(i == 0)
    def _():
        gather(0, 0)

    wait(slot)
    h2 = h1_ref[...] + ybuf[slot]
    nxt = jnp.minimum(i + 1, nsteps - 1)
    for r in range(rows):
        group = ybuf.at[1 - slot, pl.ds((r // 8) * 8, 8)]
        _row_copy(y_ref, pos_ref[nxt * rows + r], group, r % 8,
                  sem.at[1 - slot]).start(priority=r % DMA_THREADS)

    proj = jnp.dot(p_ref[...].astype(MXU), wproj_ref[...], preferred_element_type=F32)
    gate = _sigmoid(jnp.dot(_rms(h2, gple_ref[...]).astype(MXU), wgate_ref[...],
                            preferred_element_type=F32))
    h3 = h2 + gate * proj
    if final:
        h3 = _rms(h3, gfin_ref[...])
    out_ref[...] = h3

    @pl.when(i == nsteps - 1)
    def _():
        wait(1 - slot)


def _const_spec(shape):
    nd = len(shape)
    return pl.BlockSpec(shape, lambda *_: (0,) * nd)


def _route_out(bsz, seq, tb):
    nt = seq // tb
    n = bsz * seq
    shapes = [
        jax.ShapeDtypeStruct((bsz, seq, D_MODEL), F32),
        jax.ShapeDtypeStruct((n, ROW_WORDS), U32),
        jax.ShapeDtypeStruct((bsz * nt, 1, tb), I32),
        jax.ShapeDtypeStruct((bsz * nt, 1, tb), I32),
        jax.ShapeDtypeStruct((CLASS_PAD, 128), F32),
    ]
    specs = [
        pl.BlockSpec((1, tb, D_MODEL), lambda b, t, *_: (b, t, 0)),
        pl.BlockSpec((tb, ROW_WORDS), lambda b, t, *_: (b * nt + t, 0)),
        pl.BlockSpec((1, 1, tb), lambda b, t, *_: (b * nt + t, 0, 0)),
        pl.BlockSpec((1, 1, tb), lambda b, t, *_: (b * nt + t, 0, 0)),
        pl.BlockSpec((CLASS_PAD, 128), lambda b, t, *_: (0, 0)),
    ]
    return shapes, specs


def _params():
    return pltpu.CompilerParams(dimension_semantics=("arbitrary", "arbitrary"),
                                vmem_limit_bytes=VMEM_LIMIT)


def _layer0_mixer(h, gmix, win, lb, onorm, convw, wout, gffn, wrt):
    bsz, seq, _ = h.shape
    tb = L0_BLOCK
    assert seq % tb == 0
    shapes, ospecs = _route_out(bsz, seq, tb)
    return pl.pallas_call(
        _l0_kernel,
        grid=(bsz, seq // tb),
        in_specs=[
            pl.BlockSpec((1, tb, D_MODEL), lambda b, t: (b, t, 0)),
            _const_spec((1, D_MODEL)),
            _const_spec((D_MODEL, AB_IN)),
            _const_spec((1, A_WIDTH)),
            _const_spec((1, A_DK)),
            _const_spec((3, A_WIDTH)),
            _const_spec((D_MODEL, D_MODEL)),
            _const_spec((1, D_MODEL)),
            _const_spec((D_MODEL, 128)),
        ],
        out_specs=ospecs,
        out_shape=shapes,
        scratch_shapes=[
            pltpu.VMEM((tb, AB_IN), F32),
            pltpu.VMEM((tb, D_MODEL), F32),
            pltpu.VMEM((A_HEADS, A_DK, A_DK), F32),
            pltpu.VMEM((8, A_WIDTH), F32),
            pltpu.VMEM((CLASS_PAD, 128), F32),
            pltpu.VMEM((ROW_TILES, 2 * tb, 128), F32),
        ],
        compiler_params=_params(),
        name="layer0_mixer",
    )(h, gmix, win, lb, onorm, convw, wout, gffn, wrt)


def _layer1_mixer(h, sinks, gmix, win, bin_, wout, bout, gffn, wrt):
    bsz, seq, _ = h.shape
    tb = L1_BLOCK
    assert seq % tb == 0
    shapes, ospecs = _route_out(bsz, seq, tb)
    qw = Q_HEADS * HEAD_DIM
    vw = KV_HEADS * HEAD_DIM
    wqt, wk, wvt = win[:, :qw].T, win[:, qw:qw + vw], win[:, qw + vw:].T
    bqt, bk, bvt = bin_[:, :qw].reshape(qw, 1), bin_[:, qw:qw + vw], bin_[:, qw + vw:].reshape(vw, 1)
    grid_spec = pltpu.PrefetchScalarGridSpec(
        num_scalar_prefetch=1,
        grid=(bsz, seq // tb),
        in_specs=[
            pl.BlockSpec((1, tb, D_MODEL), lambda b, t, *_: (b, t, 0)),
            _const_spec((1, D_MODEL)),
            _const_spec((qw, D_MODEL)),
            _const_spec((qw, 1)),
            _const_spec((D_MODEL, vw)),
            _const_spec((1, vw)),
            _const_spec((vw, D_MODEL)),
            _const_spec((vw, 1)),
            _const_spec((D_MODEL, D_MODEL)),
            _const_spec((1, D_MODEL)),
            _const_spec((1, D_MODEL)),
            _const_spec((D_MODEL, 128)),
        ],
        out_specs=ospecs,
        scratch_shapes=[
            pltpu.VMEM((qw, tb), MXU),
            pltpu.VMEM((KV_HEADS, tb + WINDOW, HEAD_DIM), MXU),
            pltpu.VMEM((vw, tb + WINDOW), MXU),
            pltpu.VMEM((tb, D_MODEL), F32),
            pltpu.VMEM((CLASS_PAD, 128), F32),
            pltpu.VMEM((ROW_TILES, 2 * tb, 128), F32),
        ],
    )
    return pl.pallas_call(
        _l1_kernel, grid_spec=grid_spec, out_shape=shapes, compiler_params=_params(),
        name="layer1_mixer",
    )(sinks, h, gmix, wqt, bqt, wk, bk, wvt, bvt, wout, bout, gffn, wrt)


def _moe(plan, xext, layer, wg, wu, wd):
    pos, e1, e2, nused, pad0, pad1, n_rows = plan
    tm = MOE_TILE
    n_tiles = n_rows // tm

    def wspec(shape, which):
        return pl.BlockSpec((None, None) + shape,
                            lambda i, e1, e2, *_: (layer, (e1, e2)[which][i], 0, 0))

    grid_spec = pltpu.PrefetchScalarGridSpec(
        num_scalar_prefetch=6,
        grid=(n_tiles,),
        in_specs=[
            pl.BlockSpec(memory_space=pl.ANY),
            wspec((D_MODEL, D_EXPERT), 0), wspec((D_MODEL, D_EXPERT), 0), wspec((D_EXPERT, D_MODEL), 0),
            wspec((D_MODEL, D_EXPERT), 1), wspec((D_MODEL, D_EXPERT), 1), wspec((D_EXPERT, D_MODEL), 1),
        ],
        out_specs=pl.BlockSpec((tm, D_MODEL), lambda i, *_: (i, 0)),
        scratch_shapes=[
            pltpu.SMEM((n_rows,), I32),
            pltpu.VMEM((MOE_BUFFERS, tm, ROW_WORDS), U32),
            pltpu.SemaphoreType.DMA((MOE_BUFFERS,)),
            pltpu.VMEM((ROW_TILES, 2 * tm, 128), F32),
        ],
    )
    return pl.pallas_call(
        _moe_kernel, grid_spec=grid_spec,
        out_shape=jax.ShapeDtypeStruct((n_rows, D_MODEL), F32),
        compiler_params=pltpu.CompilerParams(dimension_semantics=("arbitrary",),
                                             vmem_limit_bytes=VMEM_LIMIT),
        name="moe_experts",
    )(e1, e2, nused, pos, pad0, pad1, xext, wg, wu, wd, wg, wu, wd)


def _combine_ple(pos, h1, p, layer, ys, gple, wgate, wproj, gfin, final):
    n = h1.shape[0]
    blk = MOVE_BLOCK
    assert n % blk == 0
    grid_spec = pltpu.PrefetchScalarGridSpec(
        num_scalar_prefetch=1,
        grid=(n // blk,),
        in_specs=[
            pl.BlockSpec((blk, D_MODEL), lambda i, *_: (i, 0)),
            pl.BlockSpec((None, blk, PLE_DIM), lambda i, *_: (layer, i, 0)),
            pl.BlockSpec(memory_space=pl.ANY),
            _const_spec((1, D_MODEL)),
            _const_spec((D_MODEL, D_MODEL)),
            _const_spec((PLE_DIM, D_MODEL)),
            _const_spec((1, D_MODEL)),
        ],
        out_specs=pl.BlockSpec((blk, D_MODEL), lambda i, *_: (i, 0)),
        scratch_shapes=[pltpu.VMEM((2, blk, D_MODEL), F32), pltpu.SemaphoreType.DMA((2,))],
    )
    return pl.pallas_call(
        functools.partial(_ple_kernel, final=final), grid_spec=grid_spec,
        out_shape=jax.ShapeDtypeStruct((n, D_MODEL), F32),
        compiler_params=pltpu.CompilerParams(dimension_semantics=("arbitrary",),
                                             vmem_limit_bytes=VMEM_LIMIT),
        name="combine_ple",
    )(pos, h1, p, ys, gple, wgate, wproj, gfin)


def _routing_plan(cls, rank, counts, n_tokens):
    tm = MOE_TILE
    n_tiles = n_tokens // tm + N_CLASS
    cnt = counts[:N_CLASS, 0].astype(I32)
    tiles_per_class = (cnt + tm - 1) // tm
    tile_end = jnp.cumsum(tiles_per_class)
    class_row0 = (tile_end - tiles_per_class) * tm
    pos = jnp.take(class_row0, cls.reshape(-1)) + rank.reshape(-1)
    nused = tile_end[-1]
    tidx = jnp.minimum(jnp.arange(n_tiles, dtype=I32), nused - 1)
    tcls = jnp.sum((tidx[:, None] >= tile_end[None, :]).astype(I32), axis=1)
    tcls = jnp.minimum(tcls, N_CLASS - 1)
    lo = jnp.array([a for a, _ in PAIRS], I32)
    hi = jnp.array([b for _, b in PAIRS], I32)
    grp = tcls // len(PAIRS)
    e1 = grp * PER_GROUP + jnp.take(lo, tcls % len(PAIRS))
    e2 = grp * PER_GROUP + jnp.take(hi, tcls % len(PAIRS))
    pad0 = class_row0 + cnt
    pad1 = tile_end * tm
    return (pos.astype(I32), e1.astype(I32), e2.astype(I32), nused.reshape(1).astype(I32),
            pad0.astype(I32), pad1.astype(I32), n_tiles * tm)


def _ffn_and_ple(h1, xext, cls, rank, counts, p, layer, wg, wu, wd, gple, wgate, wproj, gfin,
                 final):
    bsz, seq, _ = h1.shape
    n = bsz * seq
    plan = _routing_plan(cls, rank, counts, n)
    ys = _moe(plan, xext, layer, wg, wu, wd)
    out = _combine_ple(plan[0], h1.reshape(n, D_MODEL), p.reshape(p.shape[0], n, PLE_DIM), layer,
                       ys, gple, wgate, wproj, gfin, final)
    return out.reshape(bsz, seq, D_MODEL)


def kernel(x, p, norm_mix, norm_ffn, norm_ple, norm_final, w_in_ab, hgrn_lb_logits, hgrn_out_norm, conv_w, w_out_ab, w_in_c, b_in_c, sinks, w_out_c, b_out_c, w_router, w_gate_e, w_up_e, w_down_e, w_ple_gate, w_ple_proj):
    depth = p.shape[0]
    lower_bounds = jnp.cumsum(jax.nn.softmax(hgrn_lb_logits.astype(F32), axis=0), axis=0)
    wr32 = w_router.astype(F32)
    wr_hi = wr32.astype(BF16)
    wr_lo = (wr32 - wr_hi.astype(F32)).astype(BF16)
    wrt = jnp.concatenate(
        [wr_hi, wr_lo, jnp.zeros((D_MODEL, 128 - 2 * N_EXPERTS), BF16)], axis=1)
    gfin = norm_final.reshape(1, D_MODEL)
    wg, wu, wd = w_gate_e.astype(MXU), w_up_e.astype(MXU), w_down_e.astype(MXU)
    h = x
    for i in range(depth):
        j = i // 2
        gmix = norm_mix[i].reshape(1, D_MODEL)
        gffn = norm_ffn[i].reshape(1, D_MODEL)
        if i % 2 == 0:
            h1, xext, cls, rank, counts = _layer0_mixer(
                h, gmix, w_in_ab[j].astype(MXU), lower_bounds[i].reshape(1, A_WIDTH),
                hgrn_out_norm[j].reshape(1, A_DK), conv_w[j], w_out_ab[j].astype(MXU), gffn, wrt)
        else:
            h1, xext, cls, rank, counts = _layer1_mixer(
                h, sinks[j].astype(F32), gmix, w_in_c[j].astype(MXU), b_in_c[j].reshape(1, C_IN),
                w_out_c[j].astype(MXU), b_out_c[j].reshape(1, D_MODEL), gffn, wrt)
        h = _ffn_and_ple(
            h1, xext, cls, rank, counts, p, i, wg, wu, wd,
            norm_ple[i].reshape(1, D_MODEL), w_ple_gate[i].astype(MXU), w_ple_proj[i].astype(MXU),
            gfin, final=(i == depth - 1))
    return h
```

```python
import functools

import jax
import jax.numpy as jnp
from jax import lax
from jax.experimental import pallas as pl
from jax.experimental.pallas import tpu as pltpu

F32 = jnp.float32
BF16 = jnp.bfloat16
MXU = jnp.bfloat16
I32 = jnp.int32

D_MODEL = 1024
EPS = 1e-6
A_WIDTH = 512
A_HEADS = 4
A_DK = 128
A_CHUNK = 32
AB_IN = 3584
HEAD_DIM = 64
Q_HEADS = 16
KV_HEADS = 2
GROUP = 8
WINDOW = 128
C_IN = 1280
N_EXPERTS = 16
N_GROUPS = 4
PER_GROUP = 4
D_EXPERT = 512
PLE_DIM = 256

PAIRS = ((0, 1), (0, 2), (0, 3), (1, 2), (1, 3), (2, 3))
N_CLASS = N_GROUPS * len(PAIRS)
CLASS_PAD = 32
ROW_WORDS = D_MODEL + 128

L0_BLOCK = 256
L1_BLOCK = 512
MOE_TILE = 256
MOE_BUFFERS = 3
MOVE_BLOCK = 512
VMEM_LIMIT = 56 * 1024 * 1024

NT_DIMS = (((1,), (1,)), ((), ()))
TN_DIMS = (((0,), (0,)), ((), ()))


def _rms(x, g):
    ms = jnp.mean(x * x, axis=-1, keepdims=True)
    return x * lax.rsqrt(ms + EPS) * g


def _sigmoid(x):
    return 1.0 / (1.0 + jnp.exp(-x))


def _route_tail(h1, first, gffn_ref, wr_ref, xext_ref, cls_ref, rank_ref, cnt_ref, run_scr):
    tb = h1.shape[0]

    @pl.when(first)
    def _():
        run_scr[...] = jnp.zeros_like(run_scr)

    xn = _rms(h1, gffn_ref[...])
    x_hi = xn.astype(BF16)
    x_lo = (xn - x_hi.astype(F32)).astype(BF16)
    prod = (jnp.dot(x_hi, wr_ref[...], preferred_element_type=F32)
            + jnp.dot(x_lo, wr_ref[...], preferred_element_type=F32)).T
    logits = prod[0:N_EXPERTS, :] + prod[N_EXPERTS:2 * N_EXPERTS, :]
    mx = jnp.max(logits, axis=0, keepdims=True)
    ex = jnp.exp(logits - mx)
    sc = ex / jnp.sum(ex, axis=0, keepdims=True)
    rows = [sc[i:i + 1, :] for i in range(N_EXPERTS)]

    gscore = []
    for g in range(N_GROUPS):
        v = rows[PER_GROUP * g:PER_GROUP * (g + 1)]
        best = v[0] + v[1]
        for (i, j) in PAIRS[1:]:
            best = jnp.maximum(best, v[i] + v[j])
        gscore.append(best)
    gmax = jnp.maximum(jnp.maximum(gscore[0], gscore[1]), jnp.maximum(gscore[2], gscore[3]))
    sel = jnp.where(gscore[0] >= gmax, 0, jnp.where(gscore[1] >= gmax, 1,
                                                    jnp.where(gscore[2] >= gmax, 2, 3))).astype(I32)
    v = [jnp.where(sel == 0, rows[i], jnp.where(sel == 1, rows[4 + i],
                                                jnp.where(sel == 2, rows[8 + i], rows[12 + i])))
         for i in range(PER_GROUP)]
    chosen = []
    for i in range(PER_GROUP):
        r = jnp.zeros_like(sel)
        for j in range(PER_GROUP):
            if j == i:
                continue
            ahead = (v[j] >= v[i]) if j < i else (v[j] > v[i])
            r = r + jnp.where(ahead, 1, 0).astype(I32)
        chosen.append(r < 2)
    vsum = jnp.zeros_like(v[0])
    code = jnp.zeros_like(sel)
    for i in range(PER_GROUP):
        vsum = vsum + jnp.where(chosen[i], v[i], 0.0)
        code = code + jnp.where(chosen[i], 1 << i, 0).astype(I32)
    wts = [jnp.where(chosen[i], v[i] / vsum, 0.0) for i in range(PER_GROUP)]
    pair = jnp.where(code == 3, 0, jnp.where(code == 5, 1, jnp.where(code == 9, 2,
                     jnp.where(code == 6, 3, jnp.where(code == 10, 4, 5))))).astype(I32)
    cls = sel * len(PAIRS) + pair

    erow = lax.broadcasted_iota(I32, (128, tb), 0)
    comb_t = jnp.zeros((128, tb), F32)
    for i in range(PER_GROUP):
        comb_t = comb_t + jnp.where(erow == sel * PER_GROUP + i, wts[i], 0.0)
    comb = comb_t.T

    xext_ref[:, :D_MODEL] = xn
    xext_ref[:, D_MODEL:] = comb

    crow = lax.broadcasted_iota(I32, (CLASS_PAD, tb), 0)
    onehot = jnp.where(crow == cls, 1.0, 0.0)
    ii = lax.broadcasted_iota(I32, (tb, tb), 0)
    jj = lax.broadcasted_iota(I32, (tb, tb), 1)
    upper = jnp.where(ii < jj, 1.0, 0.0).astype(MXU)
    prefix = jnp.dot(onehot.astype(MXU), upper, preferred_element_type=F32)
    run = run_scr[...]
    rank = jnp.sum(onehot * (prefix + run[:, 0:1]), axis=0, keepdims=True)
    run_new = run + jnp.sum(onehot, axis=1, keepdims=True)
    run_scr[...] = run_new
    cnt_ref[...] = run_new
    cls_ref[0] = cls
    rank_ref[0] = rank.astype(I32)


def _l0_kernel(h_ref, gmix_ref, win_ref, lb_ref, onorm_ref, convw_ref, wout_ref, gffn_ref, wr_ref,
               h1_ref, xext_ref, cls_ref, rank_ref, cnt_ref,
               z_scr, cat_scr, st_scr, carry_scr, run_scr):
    b = pl.program_id(0)
    t = pl.program_id(1)
    tb = h_ref.shape[1]

    @pl.when(t == 0)
    def _():
        st_scr[...] = jnp.zeros_like(st_scr)
        carry_scr[...] = jnp.zeros_like(carry_scr)

    h = h_ref[0]
    u = _rms(h, gmix_ref[...]).astype(MXU)
    z_scr[...] = jnp.dot(u, win_ref[...], preferred_element_type=F32)

    gate_b = z_scr[:, 4 * A_WIDTH:5 * A_WIDTH]
    cb = z_scr[:, 5 * A_WIDTH:6 * A_WIDTH] * z_scr[:, 6 * A_WIDTH:7 * A_WIDTH]
    row = lax.broadcasted_iota(I32, cb.shape, 0)
    prev1 = carry_scr[7:8, :]
    prev2 = carry_scr[6:7, :]
    m1 = jnp.where(row == 0, prev1, pltpu.roll(cb, 1, axis=0))
    m2 = jnp.where(row == 0, prev2, jnp.where(row == 1, prev1, pltpu.roll(cb, 2, axis=0)))
    cw = convw_ref[...]
    cat_scr[:, A_WIDTH:] = gate_b * (cw[0:1, :] * m2 + cw[1:2, :] * m1 + cw[2:3, :] * cb)
    carry_scr[...] = cb[tb - 8:tb, :]

    nc = tb // A_CHUNK
    lb = lb_ref[...]
    onorm = onorm_ref[...]
    zq = z_scr[:, 0:A_WIDTH]
    zf = z_scr[:, A_WIDTH:2 * A_WIDTH]
    zi = z_scr[:, 2 * A_WIDTH:3 * A_WIDTH]
    zg = z_scr[:, 3 * A_WIDTH:4 * A_WIDTH]
    f = lb + (1.0 - lb) * _sigmoid(zf)
    logf = jnp.log(f)
    ri = lax.broadcasted_iota(I32, (tb, tb), 0)
    rj = lax.broadcasted_iota(I32, (tb, tb), 1)
    same_chunk = (ri // A_CHUNK) == (rj // A_CHUNK)
    causal = same_chunk & (ri >= rj)
    tri = jnp.where(causal, 1.0, 0.0).astype(BF16)
    p1 = logf.astype(BF16)
    r1 = logf - p1.astype(F32)
    p2 = r1.astype(BF16)
    p3 = (r1 - p2.astype(F32)).astype(BF16)
    pieces = jnp.dot(tri, jnp.concatenate([p1, p2, p3], axis=1), preferred_element_type=F32)
    bcum = (pieces[:, 0:A_WIDTH] + pieces[:, A_WIDTH:2 * A_WIDTH]) + pieces[:, 2 * A_WIDTH:]
    blast3 = bcum.reshape(nc, A_CHUNK, A_WIDTH)[:, A_CHUNK - 1:A_CHUNK, :]
    blast = jnp.broadcast_to(blast3, (nc, A_CHUNK, A_WIDTH)).reshape(tb, A_WIDTH)
    k = 1.0 - f
    q_dec = ((zq * _sigmoid(zq)) * jnp.exp(bcum)).astype(MXU)
    k_inv = (k * jnp.exp(-bcum)).astype(MXU)
    k_tail = (k * jnp.exp(blast - bcum)).astype(MXU)
    decay = jnp.exp(blast3)
    gate = _sigmoid(zg)
    vals = zi.astype(MXU)
    in_block = (lax.broadcasted_iota(I32, (tb, nc * A_DK), 0) // A_CHUNK
                == lax.broadcasted_iota(I32, (tb, nc * A_DK), 1) // A_DK)
    zero = jnp.zeros((), MXU)
    for hd in range(A_HEADS):
        sl = slice(hd * A_DK, (hd + 1) * A_DK)
        qd = q_dec[:, sl]
        vv = vals[:, sl]
        scores = lax.dot_general(qd, k_inv[:, sl], NT_DIMS, preferred_element_type=F32)
        scores = jnp.where(causal, scores, 0.0).astype(MXU)
        o = jnp.dot(scores, vv, preferred_element_type=F32)
        kt_blocks = jnp.where(in_block, jnp.tile(k_tail[:, sl], (1, nc)), zero)
        kv_all = lax.dot_general(vv, kt_blocks, TN_DIMS, preferred_element_type=F32)
        st = st_scr[hd]
        states = []
        for c in range(nc):
            cs = slice(c * A_DK, (c + 1) * A_DK)
            states.append(st.astype(MXU))
            st = st * decay[c, :, sl] + kv_all[:, cs]
        st_scr[hd] = st
        qd_blocks = jnp.where(in_block, jnp.tile(qd, (1, nc)), zero)
        o = o + lax.dot_general(qd_blocks, jnp.concatenate(states, axis=1), NT_DIMS,
                                preferred_element_type=F32)
        cat_scr[:, sl] = _rms(o, onorm) * gate[:, sl]

    h1 = h + jnp.dot(cat_scr[...].astype(MXU), wout_ref[...], preferred_element_type=F32)
    h1_ref[0] = h1
    _route_tail(h1, (b == 0) & (t == 0), gffn_ref, wr_ref, xext_ref, cls_ref, rank_ref, cnt_ref,
                run_scr)


def _l1_kernel(sinks_ref, h_ref, gmix_ref, wqt_ref, bqt_ref, wk_ref, bk_ref, wvt_ref, bvt_ref,
               wout_ref, bout_ref, gffn_ref, wr_ref,
               h1_ref, xext_ref, cls_ref, rank_ref, cnt_ref,
               qt_scr, k_scr, vt_scr, att_scr, run_scr):
    b = pl.program_id(0)
    t = pl.program_id(1)
    tb = h_ref.shape[1]

    @pl.when(t == 0)
    def _():
        k_scr[:, 0:WINDOW, :] = jnp.zeros((KV_HEADS, WINDOW, HEAD_DIM), k_scr.dtype)
        vt_scr[:, 0:WINDOW] = jnp.zeros((KV_HEADS * HEAD_DIM, WINDOW), vt_scr.dtype)

    h = h_ref[0]
    u = _rms(h, gmix_ref[...]).astype(MXU)
    scale = HEAD_DIM ** -0.5
    qt_scr[...] = ((lax.dot_general(wqt_ref[...], u, NT_DIMS, preferred_element_type=F32)
                    + bqt_ref[...]) * scale).astype(qt_scr.dtype)
    kz = (jnp.dot(u, wk_ref[...], preferred_element_type=F32) + bk_ref[...]).astype(k_scr.dtype)
    for kk in range(KV_HEADS):
        k_scr[kk, WINDOW:, :] = kz[:, kk * HEAD_DIM:(kk + 1) * HEAD_DIM]
    vt_scr[:, WINDOW:] = (lax.dot_general(wvt_ref[...], u, NT_DIMS, preferred_element_type=F32)
                          + bvt_ref[...]).astype(vt_scr.dtype)

    kj = lax.broadcasted_iota(I32, (2 * WINDOW, WINDOW), 0)
    qi = lax.broadcasted_iota(I32, (2 * WINDOW, WINDOW), 1)
    band = (kj > qi) & (kj <= qi + WINDOW)
    first_key = jnp.where(t > 0, 0, WINDOW)

    def all_heads(keep):
        return jnp.tile(jnp.where(keep, 1.0, 0.0), (1, GROUP)) > 0.5

    mask_first = all_heads(band & (kj >= first_key))
    mask_rest = all_heads(band)
    for n in range(tb // WINDOW):
        rs = slice(n * WINDOW, (n + 1) * WINDOW)
        win = slice(n * WINDOW, (n + 2) * WINDOW)
        mask = mask_first if n == 0 else mask_rest
        for kk in range(KV_HEADS):
            heads = range(kk * GROUP, (kk + 1) * GROUP)
            keys = k_scr[kk, win, :]
            vals_t = vt_scr[kk * HEAD_DIM:(kk + 1) * HEAD_DIM, win]
            q_t = jnp.concatenate(
                [qt_scr[hd * HEAD_DIM:(hd + 1) * HEAD_DIM, rs] for hd in heads], axis=1)
            sink = jnp.concatenate(
                [jnp.full((1, WINDOW), sinks_ref[hd], F32) for hd in heads], axis=1)
            s = jnp.dot(keys, q_t, preferred_element_type=F32)
            s = jnp.where(mask, s, -jnp.inf)
            m = jnp.maximum(jnp.max(s, axis=0, keepdims=True), sink)
            e = jnp.exp(s - m)
            denom = jnp.sum(e, axis=0, keepdims=True) + jnp.exp(sink - m)
            o_t = jnp.dot(vals_t, e.astype(MXU), preferred_element_type=F32) / denom
            for g2 in range(GROUP // 2):
                pair = jnp.concatenate([o_t[:, (2 * g2) * WINDOW:(2 * g2 + 1) * WINDOW],
                                        o_t[:, (2 * g2 + 1) * WINDOW:(2 * g2 + 2) * WINDOW]], axis=0)
                h0 = (kk * GROUP + 2 * g2) * HEAD_DIM
                att_scr[rs, h0:h0 + 2 * HEAD_DIM] = pair.T
    k_scr[:, 0:WINDOW, :] = k_scr[:, tb:tb + WINDOW, :]
    vt_scr[:, 0:WINDOW] = vt_scr[:, tb:tb + WINDOW]

    h1 = (h + jnp.dot(att_scr[...].astype(MXU), wout_ref[...], preferred_element_type=F32)
          + bout_ref[...])
    h1_ref[0] = h1
    _route_tail(h1, (b == 0) & (t == 0), gffn_ref, wr_ref, xext_ref, cls_ref, rank_ref, cnt_ref,
                run_scr)


def _row_copy(src, src_row, dst, dst_row, sem):
    return pltpu.make_async_copy(src.at[pl.ds(src_row, 1)], dst.at[pl.ds(dst_row, 1)], sem)


ISSUE_UNROLL = 8
DMA_THREADS = 2


def _moe_kernel(e1_ref, e2_ref, nused_ref, pos_ref, pad0_ref, pad1_ref,
                x_ref, wg1_ref, wu1_ref, wd1_ref, wg2_ref, wu2_ref, wd2_ref,
                y_ref, src_ref, xbuf, sem, xin_scr, comb_scr):
    i = pl.program_id(0)
    nbuf, tm = xbuf.shape[0], xbuf.shape[1]
    n_tokens = pos_ref.shape[0]
    nused = nused_ref[0]
    slot = i % nbuf
    ahead = nbuf - 1

    def wait(buf):
        pltpu.make_async_copy(x_ref.at[pl.ds(0, tm)], xbuf.at[buf], sem.at[buf]).wait()

    def tile_rows(tile):
        return jnp.minimum(tile, nused - 1) * tm

    @pl.when(i == 0)
    def _():
        def pad_class(c, carry):
            def pad_row(r, carry):
                src_ref[r] = 0
                return carry
            return lax.fori_loop(pad0_ref[c], pad1_ref[c], pad_row, carry)

        lax.fori_loop(0, N_CLASS, pad_class, 0)

        def invert(g, carry):
            t0 = g * ISSUE_UNROLL
            for k in range(ISSUE_UNROLL):
                src_ref[pos_ref[t0 + k]] = t0 + k
            return carry

        lax.fori_loop(0, n_tokens // ISSUE_UNROLL, invert, 0)

        for first in range(ahead):
            row0 = tile_rows(first)

            def issue(g, carry, first=first, row0=row0):
                r0 = pl.multiple_of(g * ISSUE_UNROLL, ISSUE_UNROLL)
                group = xbuf.at[first, pl.ds(r0, ISSUE_UNROLL)]
                for k in range(ISSUE_UNROLL):
                    _row_copy(x_ref, src_ref[row0 + r0 + k], group, k,
                              sem.at[first]).start(priority=k % DMA_THREADS)
                return carry

            lax.fori_loop(0, tm // ISSUE_UNROLL, issue, 0)

    @pl.when(i < nused)
    def _():
        wait(slot)
        xin_scr[...] = xbuf[slot, :, :D_MODEL].astype(MXU)
        comb_scr[...] = xbuf[slot, :, D_MODEL:]
        nxt = tile_rows(i + ahead)
        nslot = (i + ahead) % nbuf
        for r in range(tm):
            group = xbuf.at[nslot, pl.ds((r // 8) * 8, 8)]
            _row_copy(x_ref, src_ref[nxt + r], group, r % 8,
                      sem.at[nslot]).start(priority=r % DMA_THREADS)

        x = xin_scr[...]
        comb = comb_scr[...]
        lane = lax.broadcasted_iota(I32, comb.shape, 1)

        def expert(e, wg_ref, wu_ref, wd_ref):
            cw = jnp.sum(jnp.where(lane == e, comb, 0.0), axis=1, keepdims=True)
            gate = jnp.dot(x, wg_ref[...], preferred_element_type=F32)
            up = jnp.dot(x, wu_ref[...], preferred_element_type=F32)
            hidden = (gate * _sigmoid(gate)) * up
            return cw * jnp.dot(hidden.astype(MXU), wd_ref[...], preferred_element_type=F32)

        y_ref[...] = (expert(e1_ref[i], wg1_ref, wu1_ref, wd1_ref)
                      + expert(e2_ref[i], wg2_ref, wu2_ref, wd2_ref))

    @pl.when(i == nused - 1)
    def _():
        for k in range(1, nbuf):
            wait((i + k) % nbuf)

    @pl.when(i >= nused)
    def _():
        y_ref[...] = jnp.zeros_like(y_ref)


def _ple_kernel(pos_ref, h1_ref, p_ref, y_ref, gple_ref, wgate_ref, wproj_ref, gfin_ref,
                out_ref, ybuf, sem, *, final):
    rows = h1_ref.shape[0]
    i = pl.program_id(0)
    nsteps = pl.num_programs(0)
    slot = i % 2

    def gather(step, buf):
        def issue(g, carry):
            r0 = pl.multiple_of(g * ISSUE_UNROLL, ISSUE_UNROLL)
            group = ybuf.at[buf, pl.ds(r0, ISSUE_UNROLL)]
            for k in range(ISSUE_UNROLL):
                _row_copy(y_ref, pos_ref[step * rows + r0 + k], group, k,
                          sem.at[buf]).start(priority=k % DMA_THREADS)
            return carry

        lax.fori_loop(0, rows // ISSUE_UNROLL, issue, 0)

    def wait(buf):
        pltpu.make_async_copy(y_ref.at[pl.ds(0, rows)], ybuf.at[buf], sem.at[buf]).wait()

    @pl.when(i == 0)
    def _():
        gather(0, 0)

    wait(slot)
    h2 = h1_ref[...] + ybuf[slot]
    nxt = jnp.minimum(i + 1, nsteps - 1)
    for r in range(rows):
        group = ybuf.at[1 - slot, pl.ds((r // 8) * 8, 8)]
        _row_copy(y_ref, pos_ref[nxt * rows + r], group, r % 8,
                  sem.at[1 - slot]).start(priority=r % DMA_THREADS)

    proj = jnp.dot(p_ref[...].astype(MXU), wproj_ref[...], preferred_element_type=F32)
    gate = _sigmoid(jnp.dot(_rms(h2, gple_ref[...]).astype(MXU), wgate_ref[...],
                            preferred_element_type=F32))
    h3 = h2 + gate * proj
    if final:
        h3 = _rms(h3, gfin_ref[...])
    out_ref[...] = h3

    @pl.when(i == nsteps - 1)
    def _():
        wait(1 - slot)


def _const_spec(shape):
    nd = len(shape)
    return pl.BlockSpec(shape, lambda *_: (0,) * nd)


def _route_out(bsz, seq, tb):
    nt = seq // tb
    n = bsz * seq
    shapes = [
        jax.ShapeDtypeStruct((bsz, seq, D_MODEL), F32),
        jax.ShapeDtypeStruct((n, ROW_WORDS), F32),
        jax.ShapeDtypeStruct((bsz * nt, 1, tb), I32),
        jax.ShapeDtypeStruct((bsz * nt, 1, tb), I32),
        jax.ShapeDtypeStruct((CLASS_PAD, 128), F32),
    ]
    specs = [
        pl.BlockSpec((1, tb, D_MODEL), lambda b, t, *_: (b, t, 0)),
        pl.BlockSpec((tb, ROW_WORDS), lambda b, t, *_: (b * nt + t, 0)),
        pl.BlockSpec((1, 1, tb), lambda b, t, *_: (b * nt + t, 0, 0)),
        pl.BlockSpec((1, 1, tb), lambda b, t, *_: (b * nt + t, 0, 0)),
        pl.BlockSpec((CLASS_PAD, 128), lambda b, t, *_: (0, 0)),
    ]
    return shapes, specs


def _params():
    return pltpu.CompilerParams(dimension_semantics=("arbitrary", "arbitrary"),
                                vmem_limit_bytes=VMEM_LIMIT)


def _layer0_mixer(h, gmix, win, lb, onorm, convw, wout, gffn, wrt):
    bsz, seq, _ = h.shape
    tb = L0_BLOCK
    assert seq % tb == 0
    shapes, ospecs = _route_out(bsz, seq, tb)
    return pl.pallas_call(
        _l0_kernel,
        grid=(bsz, seq // tb),
        in_specs=[
            pl.BlockSpec((1, tb, D_MODEL), lambda b, t: (b, t, 0)),
            _const_spec((1, D_MODEL)),
            _const_spec((D_MODEL, AB_IN)),
            _const_spec((1, A_WIDTH)),
            _const_spec((1, A_DK)),
            _const_spec((3, A_WIDTH)),
            _const_spec((D_MODEL, D_MODEL)),
            _const_spec((1, D_MODEL)),
            _const_spec((D_MODEL, 128)),
        ],
        out_specs=ospecs,
        out_shape=shapes,
        scratch_shapes=[
            pltpu.VMEM((tb, AB_IN), F32),
            pltpu.VMEM((tb, D_MODEL), F32),
            pltpu.VMEM((A_HEADS, A_DK, A_DK), F32),
            pltpu.VMEM((8, A_WIDTH), F32),
            pltpu.VMEM((CLASS_PAD, 128), F32),
        ],
        compiler_params=_params(),
        name="layer0_mixer",
    )(h, gmix, win, lb, onorm, convw, wout, gffn, wrt)


def _layer1_mixer(h, sinks, gmix, win, bin_, wout, bout, gffn, wrt):
    bsz, seq, _ = h.shape
    tb = L1_BLOCK
    assert seq % tb == 0
    shapes, ospecs = _route_out(bsz, seq, tb)
    qw = Q_HEADS * HEAD_DIM
    vw = KV_HEADS * HEAD_DIM
    wqt, wk, wvt = win[:, :qw].T, win[:, qw:qw + vw], win[:, qw + vw:].T
    bqt, bk, bvt = bin_[:, :qw].reshape(qw, 1), bin_[:, qw:qw + vw], bin_[:, qw + vw:].reshape(vw, 1)
    grid_spec = pltpu.PrefetchScalarGridSpec(
        num_scalar_prefetch=1,
        grid=(bsz, seq // tb),
        in_specs=[
            pl.BlockSpec((1, tb, D_MODEL), lambda b, t, *_: (b, t, 0)),
            _const_spec((1, D_MODEL)),
            _const_spec((qw, D_MODEL)),
            _const_spec((qw, 1)),
            _const_spec((D_MODEL, vw)),
            _const_spec((1, vw)),
            _const_spec((vw, D_MODEL)),
            _const_spec((vw, 1)),
            _const_spec((D_MODEL, D_MODEL)),
            _const_spec((1, D_MODEL)),
            _const_spec((1, D_MODEL)),
            _const_spec((D_MODEL, 128)),
        ],
        out_specs=ospecs,
        scratch_shapes=[
            pltpu.VMEM((qw, tb), MXU),
            pltpu.VMEM((KV_HEADS, tb + WINDOW, HEAD_DIM), MXU),
            pltpu.VMEM((vw, tb + WINDOW), MXU),
            pltpu.VMEM((tb, D_MODEL), F32),
            pltpu.VMEM((CLASS_PAD, 128), F32),
        ],
    )
    return pl.pallas_call(
        _l1_kernel, grid_spec=grid_spec, out_shape=shapes, compiler_params=_params(),
        name="layer1_mixer",
    )(sinks, h, gmix, wqt, bqt, wk, bk, wvt, bvt, wout, bout, gffn, wrt)


def _moe(plan, xext, layer, wg, wu, wd):
    pos, e1, e2, nused, pad0, pad1, n_rows = plan
    tm = MOE_TILE
    n_tiles = n_rows // tm

    def wspec(shape, which):
        return pl.BlockSpec((None, None) + shape,
                            lambda i, e1, e2, *_: (layer, (e1, e2)[which][i], 0, 0))

    grid_spec = pltpu.PrefetchScalarGridSpec(
        num_scalar_prefetch=6,
        grid=(n_tiles,),
        in_specs=[
            pl.BlockSpec(memory_space=pl.ANY),
            wspec((D_MODEL, D_EXPERT), 0), wspec((D_MODEL, D_EXPERT), 0), wspec((D_EXPERT, D_MODEL), 0),
            wspec((D_MODEL, D_EXPERT), 1), wspec((D_MODEL, D_EXPERT), 1), wspec((D_EXPERT, D_MODEL), 1),
        ],
        out_specs=pl.BlockSpec((tm, D_MODEL), lambda i, *_: (i, 0)),
        scratch_shapes=[
            pltpu.SMEM((n_rows,), I32),
            pltpu.VMEM((MOE_BUFFERS, tm, ROW_WORDS), F32),
            pltpu.SemaphoreType.DMA((MOE_BUFFERS,)),
            pltpu.VMEM((tm, D_MODEL), MXU),
            pltpu.VMEM((tm, ROW_WORDS - D_MODEL), F32),
        ],
    )
    return pl.pallas_call(
        _moe_kernel, grid_spec=grid_spec,
        out_shape=jax.ShapeDtypeStruct((n_rows, D_MODEL), F32),
        compiler_params=pltpu.CompilerParams(dimension_semantics=("arbitrary",),
                                             vmem_limit_bytes=VMEM_LIMIT),
        name="moe_experts",
    )(e1, e2, nused, pos, pad0, pad1, xext, wg, wu, wd, wg, wu, wd)


def _combine_ple(pos, h1, p, layer, ys, gple, wgate, wproj, gfin, final):
    n = h1.shape[0]
    blk = MOVE_BLOCK
    assert n % blk == 0
    grid_spec = pltpu.PrefetchScalarGridSpec(
        num_scalar_prefetch=1,
        grid=(n // blk,),
        in_specs=[
            pl.BlockSpec((blk, D_MODEL), lambda i, *_: (i, 0)),
            pl.BlockSpec((None, blk, PLE_DIM), lambda i, *_: (layer, i, 0)),
            pl.BlockSpec(memory_space=pl.ANY),
            _const_spec((1, D_MODEL)),
            _const_spec((D_MODEL, D_MODEL)),
            _const_spec((PLE_DIM, D_MODEL)),
            _const_spec((1, D_MODEL)),
        ],
        out_specs=pl.BlockSpec((blk, D_MODEL), lambda i, *_: (i, 0)),
        scratch_shapes=[pltpu.VMEM((2, blk, D_MODEL), F32), pltpu.SemaphoreType.DMA((2,))],
    )
    return pl.pallas_call(
        functools.partial(_ple_kernel, final=final), grid_spec=grid_spec,
        out_shape=jax.ShapeDtypeStruct((n, D_MODEL), F32),
        compiler_params=pltpu.CompilerParams(dimension_semantics=("arbitrary",),
                                             vmem_limit_bytes=VMEM_LIMIT),
        name="combine_ple",
    )(pos, h1, p, ys, gple, wgate, wproj, gfin)


def _routing_plan(cls, rank, counts, n_tokens):
    tm = MOE_TILE
    n_tiles = n_tokens // tm + N_CLASS
    cnt = counts[:N_CLASS, 0].astype(I32)
    tiles_per_class = (cnt + tm - 1) // tm
    tile_end = jnp.cumsum(tiles_per_class)
    class_row0 = (tile_end - tiles_per_class) * tm
    pos = jnp.take(class_row0, cls.reshape(-1)) + rank.reshape(-1)
    nused = tile_end[-1]
    tidx = jnp.minimum(jnp.arange(n_tiles, dtype=I32), nused - 1)
    tcls = jnp.sum((tidx[:, None] >= tile_end[None, :]).astype(I32), axis=1)
    tcls = jnp.minimum(tcls, N_CLASS - 1)
    lo = jnp.array([a for a, _ in PAIRS], I32)
    hi = jnp.array([b for _, b in PAIRS], I32)
    grp = tcls // len(PAIRS)
    e1 = grp * PER_GROUP + jnp.take(lo, tcls % len(PAIRS))
    e2 = grp * PER_GROUP + jnp.take(hi, tcls % len(PAIRS))
    pad0 = class_row0 + cnt
    pad1 = tile_end * tm
    return (pos.astype(I32), e1.astype(I32), e2.astype(I32), nused.reshape(1).astype(I32),
            pad0.astype(I32), pad1.astype(I32), n_tiles * tm)


def _ffn_and_ple(h1, xext, cls, rank, counts, p, layer, wg, wu, wd, gple, wgate, wproj, gfin,
                 final):
    bsz, seq, _ = h1.shape
    n = bsz * seq
    plan = _routing_plan(cls, rank, counts, n)
    ys = _moe(plan, xext, layer, wg, wu, wd)
    out = _combine_ple(plan[0], h1.reshape(n, D_MODEL), p.reshape(p.shape[0], n, PLE_DIM), layer,
                       ys, gple, wgate, wproj, gfin, final)
    return out.reshape(bsz, seq, D_MODEL)


def kernel(x, p, norm_mix, norm_ffn, norm_ple, norm_final, w_in_ab, hgrn_lb_logits, hgrn_out_norm, conv_w, w_out_ab, w_in_c, b_in_c, sinks, w_out_c, b_out_c, w_router, w_gate_e, w_up_e, w_down_e, w_ple_gate, w_ple_proj):
    depth = p.shape[0]
    lower_bounds = jnp.cumsum(jax.nn.softmax(hgrn_lb_logits.astype(F32), axis=0), axis=0)
    wr32 = w_router.astype(F32)
    wr_hi = wr32.astype(BF16)
    wr_lo = (wr32 - wr_hi.astype(F32)).astype(BF16)
    wrt = jnp.concatenate(
        [wr_hi, wr_lo, jnp.zeros((D_MODEL, 128 - 2 * N_EXPERTS), BF16)], axis=1)
    gfin = norm_final.reshape(1, D_MODEL)
    wg, wu, wd = w_gate_e.astype(MXU), w_up_e.astype(MXU), w_down_e.astype(MXU)
    h = x
    for i in range(depth):
        j = i // 2
        gmix = norm_mix[i].reshape(1, D_MODEL)
        gffn = norm_ffn[i].reshape(1, D_MODEL)
        if i % 2 == 0:
            h1, xext, cls, rank, counts = _layer0_mixer(
                h, gmix, w_in_ab[j].astype(MXU), lower_bounds[i].reshape(1, A_WIDTH),
                hgrn_out_norm[j].reshape(1, A_DK), conv_w[j], w_out_ab[j].astype(MXU), gffn, wrt)
        else:
            h1, xext, cls, rank, counts = _layer1_mixer(
                h, sinks[j].astype(F32), gmix, w_in_c[j].astype(MXU), b_in_c[j].reshape(1, C_IN),
                w_out_c[j].astype(MXU), b_out_c[j].reshape(1, D_MODEL), gffn, wrt)
        h = _ffn_and_ple(
            h1, xext, cls, rank, counts, p, i, wg, wu, wd,
            norm_ple[i].reshape(1, D_MODEL), w_ple_gate[i].astype(MXU), w_ple_proj[i].astype(MXU),
            gfin, final=(i == depth - 1))
    return h
```

```python
import functools

import jax
import jax.numpy as jnp
from jax import lax
from jax.experimental import pallas as pl
from jax.experimental.pallas import tpu as pltpu

F32 = jnp.float32
BF16 = jnp.bfloat16
MXU = jnp.bfloat16
I32 = jnp.int32

D_MODEL = 1024
EPS = 1e-6
A_WIDTH = 512
A_HEADS = 4
A_DK = 128
A_CHUNK = 32
AB_IN = 3584
HEAD_DIM = 64
Q_HEADS = 16
KV_HEADS = 2
GROUP = 8
WINDOW = 128
C_IN = 1280
N_EXPERTS = 16
N_GROUPS = 4
PER_GROUP = 4
D_EXPERT = 512
PLE_DIM = 256

PAIRS = ((0, 1), (0, 2), (0, 3), (1, 2), (1, 3), (2, 3))
N_CLASS = N_GROUPS * len(PAIRS)
CLASS_PAD = 32
ROW_WORDS = D_MODEL + 128

L0_BLOCK = 256
L1_BLOCK = 512
MOE_TILE = 256
MOE_BUFFERS = 2
MOVE_BLOCK = 512
VMEM_LIMIT = 56 * 1024 * 1024

NT_DIMS = (((1,), (1,)), ((), ()))
TN_DIMS = (((0,), (0,)), ((), ()))


def _rms(x, g):
    ms = jnp.mean(x * x, axis=-1, keepdims=True)
    return x * lax.rsqrt(ms + EPS) * g


def _sigmoid(x):
    return 1.0 / (1.0 + jnp.exp(-x))


def _route_tail(h1, first, gffn_ref, wr_ref, xext_ref, cls_ref, rank_ref, cnt_ref, run_scr):
    tb = h1.shape[0]

    @pl.when(first)
    def _():
        run_scr[...] = jnp.zeros_like(run_scr)

    xn = _rms(h1, gffn_ref[...])
    x_hi = xn.astype(BF16)
    x_lo = (xn - x_hi.astype(F32)).astype(BF16)
    prod = (jnp.dot(x_hi, wr_ref[...], preferred_element_type=F32)
            + jnp.dot(x_lo, wr_ref[...], preferred_element_type=F32)).T
    logits = prod[0:N_EXPERTS, :] + prod[N_EXPERTS:2 * N_EXPERTS, :]
    mx = jnp.max(logits, axis=0, keepdims=True)
    ex = jnp.exp(logits - mx)
    sc = ex / jnp.sum(ex, axis=0, keepdims=True)
    rows = [sc[i:i + 1, :] for i in range(N_EXPERTS)]

    gscore = []
    for g in range(N_GROUPS):
        v = rows[PER_GROUP * g:PER_GROUP * (g + 1)]
        best = v[0] + v[1]
        for (i, j) in PAIRS[1:]:
            best = jnp.maximum(best, v[i] + v[j])
        gscore.append(best)
    gmax = jnp.maximum(jnp.maximum(gscore[0], gscore[1]), jnp.maximum(gscore[2], gscore[3]))
    sel = jnp.where(gscore[0] >= gmax, 0, jnp.where(gscore[1] >= gmax, 1,
                                                    jnp.where(gscore[2] >= gmax, 2, 3))).astype(I32)
    v = [jnp.where(sel == 0, rows[i], jnp.where(sel == 1, rows[4 + i],
                                                jnp.where(sel == 2, rows[8 + i], rows[12 + i])))
         for i in range(PER_GROUP)]
    chosen = []
    for i in range(PER_GROUP):
        r = jnp.zeros_like(sel)
        for j in range(PER_GROUP):
            if j == i:
                continue
            ahead = (v[j] >= v[i]) if j < i else (v[j] > v[i])
            r = r + jnp.where(ahead, 1, 0).astype(I32)
        chosen.append(r < 2)
    vsum = jnp.zeros_like(v[0])
    code = jnp.zeros_like(sel)
    for i in range(PER_GROUP):
        vsum = vsum + jnp.where(chosen[i], v[i], 0.0)
        code = code + jnp.where(chosen[i], 1 << i, 0).astype(I32)
    wts = [jnp.where(chosen[i], v[i] / vsum, 0.0) for i in range(PER_GROUP)]
    pair = jnp.where(code == 3, 0, jnp.where(code == 5, 1, jnp.where(code == 9, 2,
                     jnp.where(code == 6, 3, jnp.where(code == 10, 4, 5))))).astype(I32)
    cls = sel * len(PAIRS) + pair

    erow = lax.broadcasted_iota(I32, (128, tb), 0)
    comb_t = jnp.zeros((128, tb), F32)
    for i in range(PER_GROUP):
        comb_t = comb_t + jnp.where(erow == sel * PER_GROUP + i, wts[i], 0.0)
    comb = comb_t.T

    xext_ref[:, :D_MODEL] = xn
    xext_ref[:, D_MODEL:] = comb

    crow = lax.broadcasted_iota(I32, (CLASS_PAD, tb), 0)
    onehot = jnp.where(crow == cls, 1.0, 0.0)
    ii = lax.broadcasted_iota(I32, (tb, tb), 0)
    jj = lax.broadcasted_iota(I32, (tb, tb), 1)
    upper = jnp.where(ii < jj, 1.0, 0.0).astype(MXU)
    prefix = jnp.dot(onehot.astype(MXU), upper, preferred_element_type=F32)
    run = run_scr[...]
    rank = jnp.sum(onehot * (prefix + run[:, 0:1]), axis=0, keepdims=True)
    run_new = run + jnp.sum(onehot, axis=1, keepdims=True)
    run_scr[...] = run_new
    cnt_ref[...] = run_new
    cls_ref[0] = cls
    rank_ref[0] = rank.astype(I32)


def _l0_kernel(h_ref, gmix_ref, win_ref, lb_ref, onorm_ref, convw_ref, wout_ref, gffn_ref, wr_ref,
               h1_ref, xext_ref, cls_ref, rank_ref, cnt_ref,
               z_scr, cat_scr, st_scr, carry_scr, run_scr):
    b = pl.program_id(0)
    t = pl.program_id(1)
    tb = h_ref.shape[1]

    @pl.when(t == 0)
    def _():
        st_scr[...] = jnp.zeros_like(st_scr)
        carry_scr[...] = jnp.zeros_like(carry_scr)

    h = h_ref[0]
    u = _rms(h, gmix_ref[...]).astype(MXU)
    z_scr[...] = jnp.dot(u, win_ref[...], preferred_element_type=F32)

    gate_b = z_scr[:, 4 * A_WIDTH:5 * A_WIDTH]
    cb = z_scr[:, 5 * A_WIDTH:6 * A_WIDTH] * z_scr[:, 6 * A_WIDTH:7 * A_WIDTH]
    row = lax.broadcasted_iota(I32, cb.shape, 0)
    prev1 = carry_scr[7:8, :]
    prev2 = carry_scr[6:7, :]
    m1 = jnp.where(row == 0, prev1, pltpu.roll(cb, 1, axis=0))
    m2 = jnp.where(row == 0, prev2, jnp.where(row == 1, prev1, pltpu.roll(cb, 2, axis=0)))
    cw = convw_ref[...]
    cat_scr[:, A_WIDTH:] = gate_b * (cw[0:1, :] * m2 + cw[1:2, :] * m1 + cw[2:3, :] * cb)
    carry_scr[...] = cb[tb - 8:tb, :]

    nc = tb // A_CHUNK
    lb = lb_ref[...]
    onorm = onorm_ref[...]
    zq = z_scr[:, 0:A_WIDTH]
    zf = z_scr[:, A_WIDTH:2 * A_WIDTH]
    zi = z_scr[:, 2 * A_WIDTH:3 * A_WIDTH]
    zg = z_scr[:, 3 * A_WIDTH:4 * A_WIDTH]
    f = lb + (1.0 - lb) * _sigmoid(zf)
    logf = jnp.log(f)
    ri = lax.broadcasted_iota(I32, (tb, tb), 0)
    rj = lax.broadcasted_iota(I32, (tb, tb), 1)
    same_chunk = (ri // A_CHUNK) == (rj // A_CHUNK)
    causal = same_chunk & (ri >= rj)
    tri = jnp.where(causal, 1.0, 0.0).astype(BF16)
    p1 = logf.astype(BF16)
    r1 = logf - p1.astype(F32)
    p2 = r1.astype(BF16)
    p3 = (r1 - p2.astype(F32)).astype(BF16)
    pieces = jnp.dot(tri, jnp.concatenate([p1, p2, p3], axis=1), preferred_element_type=F32)
    bcum = (pieces[:, 0:A_WIDTH] + pieces[:, A_WIDTH:2 * A_WIDTH]) + pieces[:, 2 * A_WIDTH:]
    blast3 = bcum.reshape(nc, A_CHUNK, A_WIDTH)[:, A_CHUNK - 1:A_CHUNK, :]
    blast = jnp.broadcast_to(blast3, (nc, A_CHUNK, A_WIDTH)).reshape(tb, A_WIDTH)
    k = 1.0 - f
    q_dec = ((zq * _sigmoid(zq)) * jnp.exp(bcum)).astype(MXU)
    k_inv = (k * jnp.exp(-bcum)).astype(MXU)
    k_tail = (k * jnp.exp(blast - bcum)).astype(MXU)
    decay = jnp.exp(blast3)
    gate = _sigmoid(zg)
    vals = zi.astype(MXU)
    in_block = (lax.broadcasted_iota(I32, (tb, nc * A_DK), 0) // A_CHUNK
                == lax.broadcasted_iota(I32, (tb, nc * A_DK), 1) // A_DK)
    zero = jnp.zeros((), MXU)
    for hd in range(A_HEADS):
        sl = slice(hd * A_DK, (hd + 1) * A_DK)
        qd = q_dec[:, sl]
        vv = vals[:, sl]
        scores = lax.dot_general(qd, k_inv[:, sl], NT_DIMS, preferred_element_type=F32)
        scores = jnp.where(causal, scores, 0.0).astype(MXU)
        o = jnp.dot(scores, vv, preferred_element_type=F32)
        kt_blocks = jnp.where(in_block, jnp.tile(k_tail[:, sl], (1, nc)), zero)
        kv_all = lax.dot_general(vv, kt_blocks, TN_DIMS, preferred_element_type=F32)
        st = st_scr[hd]
        states = []
        for c in range(nc):
            cs = slice(c * A_DK, (c + 1) * A_DK)
            states.append(st.astype(MXU))
            st = st * decay[c, :, sl] + kv_all[:, cs]
        st_scr[hd] = st
        qd_blocks = jnp.where(in_block, jnp.tile(qd, (1, nc)), zero)
        o = o + lax.dot_general(qd_blocks, jnp.concatenate(states, axis=1), NT_DIMS,
                                preferred_element_type=F32)
        cat_scr[:, sl] = _rms(o, onorm) * gate[:, sl]

    h1 = h + jnp.dot(cat_scr[...].astype(MXU), wout_ref[...], preferred_element_type=F32)
    h1_ref[0] = h1
    _route_tail(h1, (b == 0) & (t == 0), gffn_ref, wr_ref, xext_ref, cls_ref, rank_ref, cnt_ref,
                run_scr)


def _l1_kernel(sinks_ref, h_ref, gmix_ref, wqt_ref, bqt_ref, wk_ref, bk_ref, wvt_ref, bvt_ref,
               wout_ref, bout_ref, gffn_ref, wr_ref,
               h1_ref, xext_ref, cls_ref, rank_ref, cnt_ref,
               qt_scr, k_scr, vt_scr, att_scr, run_scr):
    b = pl.program_id(0)
    t = pl.program_id(1)
    tb = h_ref.shape[1]

    @pl.when(t == 0)
    def _():
        k_scr[:, 0:WINDOW, :] = jnp.zeros((KV_HEADS, WINDOW, HEAD_DIM), k_scr.dtype)
        vt_scr[:, 0:WINDOW] = jnp.zeros((KV_HEADS * HEAD_DIM, WINDOW), vt_scr.dtype)

    h = h_ref[0]
    u = _rms(h, gmix_ref[...]).astype(MXU)
    scale = HEAD_DIM ** -0.5
    qt_scr[...] = ((lax.dot_general(wqt_ref[...], u, NT_DIMS, preferred_element_type=F32)
                    + bqt_ref[...]) * scale).astype(qt_scr.dtype)
    kz = (jnp.dot(u, wk_ref[...], preferred_element_type=F32) + bk_ref[...]).astype(k_scr.dtype)
    for kk in range(KV_HEADS):
        k_scr[kk, WINDOW:, :] = kz[:, kk * HEAD_DIM:(kk + 1) * HEAD_DIM]
    vt_scr[:, WINDOW:] = (lax.dot_general(wvt_ref[...], u, NT_DIMS, preferred_element_type=F32)
                          + bvt_ref[...]).astype(vt_scr.dtype)

    kj = lax.broadcasted_iota(I32, (2 * WINDOW, WINDOW), 0)
    qi = lax.broadcasted_iota(I32, (2 * WINDOW, WINDOW), 1)
    band = (kj > qi) & (kj <= qi + WINDOW)
    first_key = jnp.where(t > 0, 0, WINDOW)

    def all_heads(keep):
        return jnp.tile(jnp.where(keep, 1.0, 0.0), (1, GROUP)) > 0.5

    mask_first = all_heads(band & (kj >= first_key))
    mask_rest = all_heads(band)
    for n in range(tb // WINDOW):
        rs = slice(n * WINDOW, (n + 1) * WINDOW)
        win = slice(n * WINDOW, (n + 2) * WINDOW)
        mask = mask_first if n == 0 else mask_rest
        for kk in range(KV_HEADS):
            heads = range(kk * GROUP, (kk + 1) * GROUP)
            keys = k_scr[kk, win, :]
            vals_t = vt_scr[kk * HEAD_DIM:(kk + 1) * HEAD_DIM, win]
            q_t = jnp.concatenate(
                [qt_scr[hd * HEAD_DIM:(hd + 1) * HEAD_DIM, rs] for hd in heads], axis=1)
            sink = jnp.concatenate(
                [jnp.full((1, WINDOW), sinks_ref[hd], F32) for hd in heads], axis=1)
            s = jnp.dot(keys, q_t, preferred_element_type=F32)
            s = jnp.where(mask, s, -jnp.inf)
            m = jnp.maximum(jnp.max(s, axis=0, keepdims=True), sink)
            e = jnp.exp(s - m)
            denom = jnp.sum(e, axis=0, keepdims=True) + jnp.exp(sink - m)
            o_t = jnp.dot(vals_t, e.astype(MXU), preferred_element_type=F32) / denom
            for g2 in range(GROUP // 2):
                pair = jnp.concatenate([o_t[:, (2 * g2) * WINDOW:(2 * g2 + 1) * WINDOW],
                                        o_t[:, (2 * g2 + 1) * WINDOW:(2 * g2 + 2) * WINDOW]], axis=0)
                h0 = (kk * GROUP + 2 * g2) * HEAD_DIM
                att_scr[rs, h0:h0 + 2 * HEAD_DIM] = pair.T
    k_scr[:, 0:WINDOW, :] = k_scr[:, tb:tb + WINDOW, :]
    vt_scr[:, 0:WINDOW] = vt_scr[:, tb:tb + WINDOW]

    h1 = (h + jnp.dot(att_scr[...].astype(MXU), wout_ref[...], preferred_element_type=F32)
          + bout_ref[...])
    h1_ref[0] = h1
    _route_tail(h1, (b == 0) & (t == 0), gffn_ref, wr_ref, xext_ref, cls_ref, rank_ref, cnt_ref,
                run_scr)


def _row_copy(src, src_row, dst, dst_row, sem):
    return pltpu.make_async_copy(src.at[pl.ds(src_row, 1)], dst.at[pl.ds(dst_row, 1)], sem)


ISSUE_UNROLL = 8
DMA_THREADS = 2


def _moe_kernel(e1_ref, e2_ref, nused_ref, pos_ref, pad0_ref, pad1_ref,
                x_ref, wg1_ref, wu1_ref, wd1_ref, wg2_ref, wu2_ref, wd2_ref,
                y_ref, src_ref, xbuf, sem, xin_scr, comb_scr):
    i = pl.program_id(0)
    nbuf, tm = xbuf.shape[0], xbuf.shape[1]
    n_tokens = pos_ref.shape[0]
    nused = nused_ref[0]
    slot = i % nbuf
    ahead = nbuf - 1

    def wait(buf):
        pltpu.make_async_copy(x_ref.at[pl.ds(0, tm)], xbuf.at[buf], sem.at[buf]).wait()

    def tile_rows(tile):
        return jnp.minimum(tile, nused - 1) * tm

    @pl.when(i == 0)
    def _():
        def pad_class(c, carry):
            def pad_row(r, carry):
                src_ref[r] = 0
                return carry
            return lax.fori_loop(pad0_ref[c], pad1_ref[c], pad_row, carry)

        lax.fori_loop(0, N_CLASS, pad_class, 0)

        def invert(g, carry):
            t0 = g * ISSUE_UNROLL
            for k in range(ISSUE_UNROLL):
                src_ref[pos_ref[t0 + k]] = t0 + k
            return carry

        lax.fori_loop(0, n_tokens // ISSUE_UNROLL, invert, 0)

        for first in range(ahead):
            row0 = tile_rows(first)

            def issue(g, carry, first=first, row0=row0):
                r0 = pl.multiple_of(g * ISSUE_UNROLL, ISSUE_UNROLL)
                group = xbuf.at[first, pl.ds(r0, ISSUE_UNROLL)]
                for k in range(ISSUE_UNROLL):
                    _row_copy(x_ref, src_ref[row0 + r0 + k], group, k,
                              sem.at[first]).start(priority=k % DMA_THREADS)
                return carry

            lax.fori_loop(0, tm // ISSUE_UNROLL, issue, 0)

    @pl.when(i < nused)
    def _():
        wait(slot)
        xin_scr[...] = xbuf[slot, :, :D_MODEL].astype(MXU)
        comb_scr[...] = xbuf[slot, :, D_MODEL:]
        nxt = tile_rows(i + ahead)
        nslot = (i + ahead) % nbuf

        def issue(g, carry):
            r0 = pl.multiple_of(g * ISSUE_UNROLL, ISSUE_UNROLL)
            group = xbuf.at[nslot, pl.ds(r0, ISSUE_UNROLL)]
            for k in range(ISSUE_UNROLL):
                _row_copy(x_ref, src_ref[nxt + r0 + k], group, k,
                          sem.at[nslot]).start(priority=k % DMA_THREADS)
            return carry

        lax.fori_loop(0, tm // ISSUE_UNROLL, issue, 0)

        x = xin_scr[...]
        comb = comb_scr[...]
        lane = lax.broadcasted_iota(I32, comb.shape, 1)

        def expert(e, wg_ref, wu_ref, wd_ref):
            cw = jnp.sum(jnp.where(lane == e, comb, 0.0), axis=1, keepdims=True)
            gate = jnp.dot(x, wg_ref[...], preferred_element_type=F32)
            up = jnp.dot(x, wu_ref[...], preferred_element_type=F32)
            hidden = (gate * _sigmoid(gate)) * up
            return cw * jnp.dot(hidden.astype(MXU), wd_ref[...], preferred_element_type=F32)

        y_ref[...] = (expert(e1_ref[i], wg1_ref, wu1_ref, wd1_ref)
                      + expert(e2_ref[i], wg2_ref, wu2_ref, wd2_ref))

    @pl.when(i == nused - 1)
    def _():
        for k in range(1, nbuf):
            wait((i + k) % nbuf)

    @pl.when(i >= nused)
    def _():
        y_ref[...] = jnp.zeros_like(y_ref)


def _ple_kernel(pos_ref, h1_ref, p_ref, y_ref, gple_ref, wgate_ref, wproj_ref, gfin_ref,
                out_ref, ybuf, sem, *, final):
    rows = h1_ref.shape[0]
    i = pl.program_id(0)
    nsteps = pl.num_programs(0)
    slot = i % 2

    def gather(step, buf):
        def issue(g, carry):
            r0 = pl.multiple_of(g * ISSUE_UNROLL, ISSUE_UNROLL)
            group = ybuf.at[buf, pl.ds(r0, ISSUE_UNROLL)]
            for k in range(ISSUE_UNROLL):
                _row_copy(y_ref, pos_ref[step * rows + r0 + k], group, k,
                          sem.at[buf]).start(priority=k % DMA_THREADS)
            return carry

        lax.fori_loop(0, rows // ISSUE_UNROLL, issue, 0)

    def wait(buf):
        pltpu.make_async_copy(y_ref.at[pl.ds(0, rows)], ybuf.at[buf], sem.at[buf]).wait()

    @pl.when(i == 0)
    def _():
        gather(0, 0)

    wait(slot)
    h2 = h1_ref[...] + ybuf[slot]
    nxt = jnp.minimum(i + 1, nsteps - 1)
    for r in range(rows):
        group = ybuf.at[1 - slot, pl.ds((r // 8) * 8, 8)]
        _row_copy(y_ref, pos_ref[nxt * rows + r], group, r % 8,
                  sem.at[1 - slot]).start(priority=r % DMA_THREADS)

    proj = jnp.dot(p_ref[...].astype(MXU), wproj_ref[...], preferred_element_type=F32)
    gate = _sigmoid(jnp.dot(_rms(h2, gple_ref[...]).astype(MXU), wgate_ref[...],
                            preferred_element_type=F32))
    h3 = h2 + gate * proj
    if final:
        h3 = _rms(h3, gfin_ref[...])
    out_ref[...] = h3

    @pl.when(i == nsteps - 1)
    def _():
        wait(1 - slot)


def _const_spec(shape):
    nd = len(shape)
    return pl.BlockSpec(shape, lambda *_: (0,) * nd)


def _route_out(bsz, seq, tb):
    nt = seq // tb
    n = bsz * seq
    shapes = [
        jax.ShapeDtypeStruct((bsz, seq, D_MODEL), F32),
        jax.ShapeDtypeStruct((n, ROW_WORDS), F32),
        jax.ShapeDtypeStruct((bsz * nt, 1, tb), I32),
        jax.ShapeDtypeStruct((bsz * nt, 1, tb), I32),
        jax.ShapeDtypeStruct((CLASS_PAD, 128), F32),
    ]
    specs = [
        pl.BlockSpec((1, tb, D_MODEL), lambda b, t, *_: (b, t, 0)),
        pl.BlockSpec((tb, ROW_WORDS), lambda b, t, *_: (b * nt + t, 0)),
        pl.BlockSpec((1, 1, tb), lambda b, t, *_: (b * nt + t, 0, 0)),
        pl.BlockSpec((1, 1, tb), lambda b, t, *_: (b * nt + t, 0, 0)),
        pl.BlockSpec((CLASS_PAD, 128), lambda b, t, *_: (0, 0)),
    ]
    return shapes, specs


def _params():
    return pltpu.CompilerParams(dimension_semantics=("arbitrary", "arbitrary"),
                                vmem_limit_bytes=VMEM_LIMIT)


def _layer0_mixer(h, gmix, win, lb, onorm, convw, wout, gffn, wrt):
    bsz, seq, _ = h.shape
    tb = L0_BLOCK
    assert seq % tb == 0
    shapes, ospecs = _route_out(bsz, seq, tb)
    return pl.pallas_call(
        _l0_kernel,
        grid=(bsz, seq // tb),
        in_specs=[
            pl.BlockSpec((1, tb, D_MODEL), lambda b, t: (b, t, 0)),
            _const_spec((1, D_MODEL)),
            _const_spec((D_MODEL, AB_IN)),
            _const_spec((1, A_WIDTH)),
            _const_spec((1, A_DK)),
            _const_spec((3, A_WIDTH)),
            _const_spec((D_MODEL, D_MODEL)),
            _const_spec((1, D_MODEL)),
            _const_spec((D_MODEL, 128)),
        ],
        out_specs=ospecs,
        out_shape=shapes,
        scratch_shapes=[
            pltpu.VMEM((tb, AB_IN), F32),
            pltpu.VMEM((tb, D_MODEL), F32),
            pltpu.VMEM((A_HEADS, A_DK, A_DK), F32),
            pltpu.VMEM((8, A_WIDTH), F32),
            pltpu.VMEM((CLASS_PAD, 128), F32),
        ],
        compiler_params=_params(),
        name="layer0_mixer",
    )(h, gmix, win, lb, onorm, convw, wout, gffn, wrt)


def _layer1_mixer(h, sinks, gmix, win, bin_, wout, bout, gffn, wrt):
    bsz, seq, _ = h.shape
    tb = L1_BLOCK
    assert seq % tb == 0
    shapes, ospecs = _route_out(bsz, seq, tb)
    qw = Q_HEADS * HEAD_DIM
    vw = KV_HEADS * HEAD_DIM
    wqt, wk, wvt = win[:, :qw].T, win[:, qw:qw + vw], win[:, qw + vw:].T
    bqt, bk, bvt = bin_[:, :qw].reshape(qw, 1), bin_[:, qw:qw + vw], bin_[:, qw + vw:].reshape(vw, 1)
    grid_spec = pltpu.PrefetchScalarGridSpec(
        num_scalar_prefetch=1,
        grid=(bsz, seq // tb),
        in_specs=[
            pl.BlockSpec((1, tb, D_MODEL), lambda b, t, *_: (b, t, 0)),
            _const_spec((1, D_MODEL)),
            _const_spec((qw, D_MODEL)),
            _const_spec((qw, 1)),
            _const_spec((D_MODEL, vw)),
            _const_spec((1, vw)),
            _const_spec((vw, D_MODEL)),
            _const_spec((vw, 1)),
            _const_spec((D_MODEL, D_MODEL)),
            _const_spec((1, D_MODEL)),
            _const_spec((1, D_MODEL)),
            _const_spec((D_MODEL, 128)),
        ],
        out_specs=ospecs,
        scratch_shapes=[
            pltpu.VMEM((qw, tb), MXU),
            pltpu.VMEM((KV_HEADS, tb + WINDOW, HEAD_DIM), MXU),
            pltpu.VMEM((vw, tb + WINDOW), MXU),
            pltpu.VMEM((tb, D_MODEL), F32),
            pltpu.VMEM((CLASS_PAD, 128), F32),
        ],
    )
    return pl.pallas_call(
        _l1_kernel, grid_spec=grid_spec, out_shape=shapes, compiler_params=_params(),
        name="layer1_mixer",
    )(sinks, h, gmix, wqt, bqt, wk, bk, wvt, bvt, wout, bout, gffn, wrt)


def _moe(plan, xext, layer, wg, wu, wd):
    pos, e1, e2, nused, pad0, pad1, n_rows = plan
    tm = MOE_TILE
    n_tiles = n_rows // tm

    def wspec(shape, which):
        return pl.BlockSpec((None, None) + shape,
                            lambda i, e1, e2, *_: (layer, (e1, e2)[which][i], 0, 0))

    grid_spec = pltpu.PrefetchScalarGridSpec(
        num_scalar_prefetch=6,
        grid=(n_tiles,),
        in_specs=[
            pl.BlockSpec(memory_space=pl.ANY),
            wspec((D_MODEL, D_EXPERT), 0), wspec((D_MODEL, D_EXPERT), 0), wspec((D_EXPERT, D_MODEL), 0),
            wspec((D_MODEL, D_EXPERT), 1), wspec((D_MODEL, D_EXPERT), 1), wspec((D_EXPERT, D_MODEL), 1),
        ],
        out_specs=pl.BlockSpec((tm, D_MODEL), lambda i, *_: (i, 0)),
        scratch_shapes=[
            pltpu.SMEM((n_rows,), I32),
            pltpu.VMEM((MOE_BUFFERS, tm, ROW_WORDS), F32),
            pltpu.SemaphoreType.DMA((MOE_BUFFERS,)),
            pltpu.VMEM((tm, D_MODEL), MXU),
            pltpu.VMEM((tm, ROW_WORDS - D_MODEL), F32),
        ],
    )
    return pl.pallas_call(
        _moe_kernel, grid_spec=grid_spec,
        out_shape=jax.ShapeDtypeStruct((n_rows, D_MODEL), F32),
        compiler_params=pltpu.CompilerParams(dimension_semantics=("arbitrary",),
                                             vmem_limit_bytes=VMEM_LIMIT),
        name="moe_experts",
    )(e1, e2, nused, pos, pad0, pad1, xext, wg, wu, wd, wg, wu, wd)


def _combine_ple(pos, h1, p, layer, ys, gple, wgate, wproj, gfin, final):
    n = h1.shape[0]
    blk = MOVE_BLOCK
    assert n % blk == 0
    grid_spec = pltpu.PrefetchScalarGridSpec(
        num_scalar_prefetch=1,
        grid=(n // blk,),
        in_specs=[
            pl.BlockSpec((blk, D_MODEL), lambda i, *_: (i, 0)),
            pl.BlockSpec((None, blk, PLE_DIM), lambda i, *_: (layer, i, 0)),
            pl.BlockSpec(memory_space=pl.ANY),
            _const_spec((1, D_MODEL)),
            _const_spec((D_MODEL, D_MODEL)),
            _const_spec((PLE_DIM, D_MODEL)),
            _const_spec((1, D_MODEL)),
        ],
        out_specs=pl.BlockSpec((blk, D_MODEL), lambda i, *_: (i, 0)),
        scratch_shapes=[pltpu.VMEM((2, blk, D_MODEL), F32), pltpu.SemaphoreType.DMA((2,))],
    )
    return pl.pallas_call(
        functools.partial(_ple_kernel, final=final), grid_spec=grid_spec,
        out_shape=jax.ShapeDtypeStruct((n, D_MODEL), F32),
        compiler_params=pltpu.CompilerParams(dimension_semantics=("arbitrary",),
                                             vmem_limit_bytes=VMEM_LIMIT),
        name="combine_ple",
    )(pos, h1, p, ys, gple, wgate, wproj, gfin)


def _routing_plan(cls, rank, counts, n_tokens):
    tm = MOE_TILE
    n_tiles = n_tokens // tm + N_CLASS
    cnt = counts[:N_CLASS, 0].astype(I32)
    tiles_per_class = (cnt + tm - 1) // tm
    tile_end = jnp.cumsum(tiles_per_class)
    class_row0 = (tile_end - tiles_per_class) * tm
    pos = jnp.take(class_row0, cls.reshape(-1)) + rank.reshape(-1)
    nused = tile_end[-1]
    tidx = jnp.minimum(jnp.arange(n_tiles, dtype=I32), nused - 1)
    tcls = jnp.sum((tidx[:, None] >= tile_end[None, :]).astype(I32), axis=1)
    tcls = jnp.minimum(tcls, N_CLASS - 1)
    lo = jnp.array([a for a, _ in PAIRS], I32)
    hi = jnp.array([b for _, b in PAIRS], I32)
    grp = tcls // len(PAIRS)
    e1 = grp * PER_GROUP + jnp.take(lo, tcls % len(PAIRS))
    e2 = grp * PER_GROUP + jnp.take(hi, tcls % len(PAIRS))
    pad0 = class_row0 + cnt
    pad1 = tile_end * tm
    return (pos.astype(I32), e1.astype(I32), e2.astype(I32), nused.reshape(1).astype(I32),
            pad0.astype(I32), pad1.astype(I32), n_tiles * tm)


def _ffn_and_ple(h1, xext, cls, rank, counts, p, layer, wg, wu, wd, gple, wgate, wproj, gfin,
                 final):
    bsz, seq, _ = h1.shape
    n = bsz * seq
    plan = _routing_plan(cls, rank, counts, n)
    ys = _moe(plan, xext, layer, wg, wu, wd)
    out = _combine_ple(plan[0], h1.reshape(n, D_MODEL), p.reshape(p.shape[0], n, PLE_DIM), layer,
                       ys, gple, wgate, wproj, gfin, final)
    return out.reshape(bsz, seq, D_MODEL)


def kernel(x, p, norm_mix, norm_ffn, norm_ple, norm_final, w_in_ab, hgrn_lb_logits, hgrn_out_norm, conv_w, w_out_ab, w_in_c, b_in_c, sinks, w_out_c, b_out_c, w_router, w_gate_e, w_up_e, w_down_e, w_ple_gate, w_ple_proj):
    depth = p.shape[0]
    lower_bounds = jnp.cumsum(jax.nn.softmax(hgrn_lb_logits.astype(F32), axis=0), axis=0)
    wr32 = w_router.astype(F32)
    wr_hi = wr32.astype(BF16)
    wr_lo = (wr32 - wr_hi.astype(F32)).astype(BF16)
    wrt = jnp.concatenate(
        [wr_hi, wr_lo, jnp.zeros((D_MODEL, 128 - 2 * N_EXPERTS), BF16)], axis=1)
    gfin = norm_final.reshape(1, D_MODEL)
    wg, wu, wd = w_gate_e.astype(MXU), w_up_e.astype(MXU), w_down_e.astype(MXU)
    h = x
    for i in range(depth):
        j = i // 2
        gmix = norm_mix[i].reshape(1, D_MODEL)
        gffn = norm_ffn[i].reshape(1, D_MODEL)
        if i % 2 == 0:
            h1, xext, cls, rank, counts = _layer0_mixer(
                h, gmix, w_in_ab[j].astype(MXU), lower_bounds[i].reshape(1, A_WIDTH),
                hgrn_out_norm[j].reshape(1, A_DK), conv_w[j], w_out_ab[j].astype(MXU), gffn, wrt)
        else:
            h1, xext, cls, rank, counts = _layer1_mixer(
                h, sinks[j].astype(F32), gmix, w_in_c[j].astype(MXU), b_in_c[j].reshape(1, C_IN),
                w_out_c[j].astype(MXU), b_out_c[j].reshape(1, D_MODEL), gffn, wrt)
        h = _ffn_and_ple(
            h1, xext, cls, rank, counts, p, i, wg, wu, wd,
            norm_ple[i].reshape(1, D_MODEL), w_ple_gate[i].astype(MXU), w_ple_proj[i].astype(MXU),
            gfin, final=(i == depth - 1))
    return h
```

```python
import functools

import jax
import jax.numpy as jnp
from jax import lax
from jax.experimental import pallas as pl
from jax.experimental.pallas import tpu as pltpu

F32 = jnp.float32
BF16 = jnp.bfloat16
MXU = jnp.bfloat16
I32 = jnp.int32

D_MODEL = 1024
EPS = 1e-6
A_WIDTH = 512
A_HEADS = 4
A_DK = 128
A_CHUNK = 32
AB_IN = 3584
HEAD_DIM = 64
Q_HEADS = 16
KV_HEADS = 2
GROUP = 8
WINDOW = 128
C_IN = 1280
N_EXPERTS = 16
N_GROUPS = 4
PER_GROUP = 4
D_EXPERT = 512
PLE_DIM = 256

PAIRS = ((0, 1), (0, 2), (0, 3), (1, 2), (1, 3), (2, 3))
N_CLASS = N_GROUPS * len(PAIRS)
CLASS_PAD = 32
ROW_WORDS = D_MODEL + 128

L0_BLOCK = 256
L1_BLOCK = 512
MOE_TILE = 256
MOE_BUFFERS = 3
MOVE_BLOCK = 512
VMEM_LIMIT = 56 * 1024 * 1024

NT_DIMS = (((1,), (1,)), ((), ()))
TN_DIMS = (((0,), (0,)), ((), ()))


def _rms(x, g):
    ms = jnp.mean(x * x, axis=-1, keepdims=True)
    return x * lax.rsqrt(ms + EPS) * g


def _sigmoid(x):
    return 1.0 / (1.0 + jnp.exp(-x))


def _route_tail(h1, first, gffn_ref, wr_ref, xext_ref, cls_ref, rank_ref, cnt_ref, run_scr):
    tb = h1.shape[0]

    @pl.when(first)
    def _():
        run_scr[...] = jnp.zeros_like(run_scr)

    xn = _rms(h1, gffn_ref[...])
    x_hi = xn.astype(BF16)
    x_lo = (xn - x_hi.astype(F32)).astype(BF16)
    prod = (jnp.dot(x_hi, wr_ref[...], preferred_element_type=F32)
            + jnp.dot(x_lo, wr_ref[...], preferred_element_type=F32)).T
    logits = prod[0:N_EXPERTS, :] + prod[N_EXPERTS:2 * N_EXPERTS, :]
    mx = jnp.max(logits, axis=0, keepdims=True)
    ex = jnp.exp(logits - mx)
    sc = ex / jnp.sum(ex, axis=0, keepdims=True)
    rows = [sc[i:i + 1, :] for i in range(N_EXPERTS)]

    gscore = []
    for g in range(N_GROUPS):
        v = rows[PER_GROUP * g:PER_GROUP * (g + 1)]
        best = v[0] + v[1]
        for (i, j) in PAIRS[1:]:
            best = jnp.maximum(best, v[i] + v[j])
        gscore.append(best)
    gmax = jnp.maximum(jnp.maximum(gscore[0], gscore[1]), jnp.maximum(gscore[2], gscore[3]))
    sel = jnp.where(gscore[0] >= gmax, 0, jnp.where(gscore[1] >= gmax, 1,
                                                    jnp.where(gscore[2] >= gmax, 2, 3))).astype(I32)
    v = [jnp.where(sel == 0, rows[i], jnp.where(sel == 1, rows[4 + i],
                                                jnp.where(sel == 2, rows[8 + i], rows[12 + i])))
         for i in range(PER_GROUP)]
    chosen = []
    for i in range(PER_GROUP):
        r = jnp.zeros_like(sel)
        for j in range(PER_GROUP):
            if j == i:
                continue
            ahead = (v[j] >= v[i]) if j < i else (v[j] > v[i])
            r = r + jnp.where(ahead, 1, 0).astype(I32)
        chosen.append(r < 2)
    vsum = jnp.zeros_like(v[0])
    code = jnp.zeros_like(sel)
    for i in range(PER_GROUP):
        vsum = vsum + jnp.where(chosen[i], v[i], 0.0)
        code = code + jnp.where(chosen[i], 1 << i, 0).astype(I32)
    wts = [jnp.where(chosen[i], v[i] / vsum, 0.0) for i in range(PER_GROUP)]
    pair = jnp.where(code == 3, 0, jnp.where(code == 5, 1, jnp.where(code == 9, 2,
                     jnp.where(code == 6, 3, jnp.where(code == 10, 4, 5))))).astype(I32)
    cls = sel * len(PAIRS) + pair

    erow = lax.broadcasted_iota(I32, (128, tb), 0)
    comb_t = jnp.zeros((128, tb), F32)
    for i in range(PER_GROUP):
        comb_t = comb_t + jnp.where(erow == sel * PER_GROUP + i, wts[i], 0.0)
    comb = comb_t.T

    xext_ref[:, :D_MODEL] = xn
    xext_ref[:, D_MODEL:] = comb

    crow = lax.broadcasted_iota(I32, (CLASS_PAD, tb), 0)
    onehot = jnp.where(crow == cls, 1.0, 0.0)
    ii = lax.broadcasted_iota(I32, (tb, tb), 0)
    jj = lax.broadcasted_iota(I32, (tb, tb), 1)
    upper = jnp.where(ii < jj, 1.0, 0.0).astype(MXU)
    prefix = jnp.dot(onehot.astype(MXU), upper, preferred_element_type=F32)
    run = run_scr[...]
    rank = jnp.sum(onehot * (prefix + run[:, 0:1]), axis=0, keepdims=True)
    run_new = run + jnp.sum(onehot, axis=1, keepdims=True)
    run_scr[...] = run_new
    cnt_ref[...] = run_new
    cls_ref[0] = cls
    rank_ref[0] = rank.astype(I32)


def _l0_kernel(h_ref, gmix_ref, win_ref, lb_ref, onorm_ref, convw_ref, wout_ref, gffn_ref, wr_ref,
               h1_ref, xext_ref, cls_ref, rank_ref, cnt_ref,
               z_scr, cat_scr, st_scr, carry_scr, run_scr):
    b = pl.program_id(0)
    t = pl.program_id(1)
    tb = h_ref.shape[1]

    @pl.when(t == 0)
    def _():
        st_scr[...] = jnp.zeros_like(st_scr)
        carry_scr[...] = jnp.zeros_like(carry_scr)

    h = h_ref[0]
    u = _rms(h, gmix_ref[...]).astype(MXU)
    z_scr[...] = jnp.dot(u, win_ref[...], preferred_element_type=F32)

    gate_b = z_scr[:, 4 * A_WIDTH:5 * A_WIDTH]
    cb = z_scr[:, 5 * A_WIDTH:6 * A_WIDTH] * z_scr[:, 6 * A_WIDTH:7 * A_WIDTH]
    row = lax.broadcasted_iota(I32, cb.shape, 0)
    prev1 = carry_scr[7:8, :]
    prev2 = carry_scr[6:7, :]
    m1 = jnp.where(row == 0, prev1, pltpu.roll(cb, 1, axis=0))
    m2 = jnp.where(row == 0, prev2, jnp.where(row == 1, prev1, pltpu.roll(cb, 2, axis=0)))
    cw = convw_ref[...]
    cat_scr[:, A_WIDTH:] = gate_b * (cw[0:1, :] * m2 + cw[1:2, :] * m1 + cw[2:3, :] * cb)
    carry_scr[...] = cb[tb - 8:tb, :]

    nc = tb // A_CHUNK
    lb = lb_ref[...]
    onorm = onorm_ref[...]
    zq = z_scr[:, 0:A_WIDTH]
    zf = z_scr[:, A_WIDTH:2 * A_WIDTH]
    zi = z_scr[:, 2 * A_WIDTH:3 * A_WIDTH]
    zg = z_scr[:, 3 * A_WIDTH:4 * A_WIDTH]
    f = lb + (1.0 - lb) * _sigmoid(zf)
    logf = jnp.log(f)
    ri = lax.broadcasted_iota(I32, (tb, tb), 0)
    rj = lax.broadcasted_iota(I32, (tb, tb), 1)
    same_chunk = (ri // A_CHUNK) == (rj // A_CHUNK)
    causal = same_chunk & (ri >= rj)
    tri = jnp.where(causal, 1.0, 0.0).astype(BF16)
    p1 = logf.astype(BF16)
    r1 = logf - p1.astype(F32)
    p2 = r1.astype(BF16)
    p3 = (r1 - p2.astype(F32)).astype(BF16)
    pieces = jnp.dot(tri, jnp.concatenate([p1, p2, p3], axis=1), preferred_element_type=F32)
    bcum = (pieces[:, 0:A_WIDTH] + pieces[:, A_WIDTH:2 * A_WIDTH]) + pieces[:, 2 * A_WIDTH:]
    blast3 = bcum.reshape(nc, A_CHUNK, A_WIDTH)[:, A_CHUNK - 1:A_CHUNK, :]
    blast = jnp.broadcast_to(blast3, (nc, A_CHUNK, A_WIDTH)).reshape(tb, A_WIDTH)
    k = 1.0 - f
    q_dec = ((zq * _sigmoid(zq)) * jnp.exp(bcum)).astype(MXU)
    k_inv = (k * jnp.exp(-bcum)).astype(MXU)
    k_tail = (k * jnp.exp(blast - bcum)).astype(MXU)
    decay = jnp.exp(blast3)
    gate = _sigmoid(zg)
    vals = zi.astype(MXU)
    in_block = (lax.broadcasted_iota(I32, (tb, nc * A_DK), 0) // A_CHUNK
                == lax.broadcasted_iota(I32, (tb, nc * A_DK), 1) // A_DK)
    zero = jnp.zeros((), MXU)
    for hd in range(A_HEADS):
        sl = slice(hd * A_DK, (hd + 1) * A_DK)
        qd = q_dec[:, sl]
        vv = vals[:, sl]
        scores = lax.dot_general(qd, k_inv[:, sl], NT_DIMS, preferred_element_type=F32)
        scores = jnp.where(causal, scores, 0.0).astype(MXU)
        o = jnp.dot(scores, vv, preferred_element_type=F32)
        kt_blocks = jnp.where(in_block, jnp.tile(k_tail[:, sl], (1, nc)), zero)
        kv_all = lax.dot_general(vv, kt_blocks, TN_DIMS, preferred_element_type=F32)
        st = st_scr[hd]
        states = []
        for c in range(nc):
            cs = slice(c * A_DK, (c + 1) * A_DK)
            states.append(st.astype(MXU))
            st = st * decay[c, :, sl] + kv_all[:, cs]
        st_scr[hd] = st
        qd_blocks = jnp.where(in_block, jnp.tile(qd, (1, nc)), zero)
        o = o + lax.dot_general(qd_blocks, jnp.concatenate(states, axis=1), NT_DIMS,
                                preferred_element_type=F32)
        cat_scr[:, sl] = _rms(o, onorm) * gate[:, sl]

    h1 = h + jnp.dot(cat_scr[...].astype(MXU), wout_ref[...], preferred_element_type=F32)
    h1_ref[0] = h1
    _route_tail(h1, (b == 0) & (t == 0), gffn_ref, wr_ref, xext_ref, cls_ref, rank_ref, cnt_ref,
                run_scr)


def _l1_kernel(sinks_ref, h_ref, gmix_ref, wqt_ref, bqt_ref, wk_ref, bk_ref, wvt_ref, bvt_ref,
               wout_ref, bout_ref, gffn_ref, wr_ref,
               h1_ref, xext_ref, cls_ref, rank_ref, cnt_ref,
               qt_scr, k_scr, vt_scr, att_scr, run_scr):
    b = pl.program_id(0)
    t = pl.program_id(1)
    tb = h_ref.shape[1]

    @pl.when(t == 0)
    def _():
        k_scr[:, 0:WINDOW, :] = jnp.zeros((KV_HEADS, WINDOW, HEAD_DIM), k_scr.dtype)
        vt_scr[:, 0:WINDOW] = jnp.zeros((KV_HEADS * HEAD_DIM, WINDOW), vt_scr.dtype)

    h = h_ref[0]
    u = _rms(h, gmix_ref[...]).astype(MXU)
    scale = HEAD_DIM ** -0.5
    qt_scr[...] = ((lax.dot_general(wqt_ref[...], u, NT_DIMS, preferred_element_type=F32)
                    + bqt_ref[...]) * scale).astype(qt_scr.dtype)
    kz = (jnp.dot(u, wk_ref[...], preferred_element_type=F32) + bk_ref[...]).astype(k_scr.dtype)
    for kk in range(KV_HEADS):
        k_scr[kk, WINDOW:, :] = kz[:, kk * HEAD_DIM:(kk + 1) * HEAD_DIM]
    vt_scr[:, WINDOW:] = (lax.dot_general(wvt_ref[...], u, NT_DIMS, preferred_element_type=F32)
                          + bvt_ref[...]).astype(vt_scr.dtype)

    kj = lax.broadcasted_iota(I32, (2 * WINDOW, WINDOW), 0)
    qi = lax.broadcasted_iota(I32, (2 * WINDOW, WINDOW), 1)
    band = (kj > qi) & (kj <= qi + WINDOW)
    first_key = jnp.where(t > 0, 0, WINDOW)

    def all_heads(keep):
        return jnp.tile(jnp.where(keep, 1.0, 0.0), (1, GROUP)) > 0.5

    mask_first = all_heads(band & (kj >= first_key))
    mask_rest = all_heads(band)
    for n in range(tb // WINDOW):
        rs = slice(n * WINDOW, (n + 1) * WINDOW)
        win = slice(n * WINDOW, (n + 2) * WINDOW)
        mask = mask_first if n == 0 else mask_rest
        for kk in range(KV_HEADS):
            heads = range(kk * GROUP, (kk + 1) * GROUP)
            keys = k_scr[kk, win, :]
            vals_t = vt_scr[kk * HEAD_DIM:(kk + 1) * HEAD_DIM, win]
            q_t = jnp.concatenate(
                [qt_scr[hd * HEAD_DIM:(hd + 1) * HEAD_DIM, rs] for hd in heads], axis=1)
            sink = jnp.concatenate(
                [jnp.full((1, WINDOW), sinks_ref[hd], F32) for hd in heads], axis=1)
            s = jnp.dot(keys, q_t, preferred_element_type=F32)
            s = jnp.where(mask, s, -jnp.inf)
            m = jnp.maximum(jnp.max(s, axis=0, keepdims=True), sink)
            e = jnp.exp(s - m)
            denom = jnp.sum(e, axis=0, keepdims=True) + jnp.exp(sink - m)
            o_t = jnp.dot(vals_t, e.astype(MXU), preferred_element_type=F32) / denom
            for g2 in range(GROUP // 2):
                pair = jnp.concatenate([o_t[:, (2 * g2) * WINDOW:(2 * g2 + 1) * WINDOW],
                                        o_t[:, (2 * g2 + 1) * WINDOW:(2 * g2 + 2) * WINDOW]], axis=0)
                h0 = (kk * GROUP + 2 * g2) * HEAD_DIM
                att_scr[rs, h0:h0 + 2 * HEAD_DIM] = pair.T
    k_scr[:, 0:WINDOW, :] = k_scr[:, tb:tb + WINDOW, :]
    vt_scr[:, 0:WINDOW] = vt_scr[:, tb:tb + WINDOW]

    h1 = (h + jnp.dot(att_scr[...].astype(MXU), wout_ref[...], preferred_element_type=F32)
          + bout_ref[...])
    h1_ref[0] = h1
    _route_tail(h1, (b == 0) & (t == 0), gffn_ref, wr_ref, xext_ref, cls_ref, rank_ref, cnt_ref,
                run_scr)


def _row_copy(src, src_row, dst, dst_row, sem):
    return pltpu.make_async_copy(src.at[pl.ds(src_row, 1)], dst.at[pl.ds(dst_row, 1)], sem)


ISSUE_UNROLL = 8
GATHER_PRIORITY = 1


def _moe_kernel(e1_ref, e2_ref, nused_ref, pos_ref, pad0_ref, pad1_ref,
                x_ref, wg1_ref, wu1_ref, wd1_ref, wg2_ref, wu2_ref, wd2_ref,
                y_ref, src_ref, xbuf, sem, xin_scr, comb_scr):
    i = pl.program_id(0)
    nbuf, tm = xbuf.shape[0], xbuf.shape[1]
    n_tokens = pos_ref.shape[0]
    nused = nused_ref[0]
    slot = i % nbuf
    ahead = nbuf - 1

    def wait(buf):
        pltpu.make_async_copy(x_ref.at[pl.ds(0, tm)], xbuf.at[buf], sem.at[buf]).wait()

    def tile_rows(tile):
        return jnp.minimum(tile, nused - 1) * tm

    @pl.when(i == 0)
    def _():
        def pad_class(c, carry):
            def pad_row(r, carry):
                src_ref[r] = 0
                return carry
            return lax.fori_loop(pad0_ref[c], pad1_ref[c], pad_row, carry)

        lax.fori_loop(0, N_CLASS, pad_class, 0)

        def invert(g, carry):
            t0 = g * ISSUE_UNROLL
            for k in range(ISSUE_UNROLL):
                src_ref[pos_ref[t0 + k]] = t0 + k
            return carry

        lax.fori_loop(0, n_tokens // ISSUE_UNROLL, invert, 0)

        for first in range(ahead):
            row0 = tile_rows(first)

            def issue(g, carry, first=first, row0=row0):
                r0 = pl.multiple_of(g * ISSUE_UNROLL, ISSUE_UNROLL)
                group = xbuf.at[first, pl.ds(r0, ISSUE_UNROLL)]
                for k in range(ISSUE_UNROLL):
                    _row_copy(x_ref, src_ref[row0 + r0 + k], group, k,
                              sem.at[first]).start(priority=GATHER_PRIORITY)
                return carry

            lax.fori_loop(0, tm // ISSUE_UNROLL, issue, 0)

    @pl.when(i < nused)
    def _():
        wait(slot)
        xin_scr[...] = xbuf[slot, :, :D_MODEL].astype(MXU)
        comb_scr[...] = xbuf[slot, :, D_MODEL:]
        nxt = tile_rows(i + ahead)
        nslot = (i + ahead) % nbuf
        for r in range(tm):
            group = xbuf.at[nslot, pl.ds((r // 8) * 8, 8)]
            _row_copy(x_ref, src_ref[nxt + r], group, r % 8,
                      sem.at[nslot]).start(priority=GATHER_PRIORITY)

        x = xin_scr[...]
        comb = comb_scr[...]
        lane = lax.broadcasted_iota(I32, comb.shape, 1)

        def expert(e, wg_ref, wu_ref, wd_ref):
            cw = jnp.sum(jnp.where(lane == e, comb, 0.0), axis=1, keepdims=True)
            gate = jnp.dot(x, wg_ref[...], preferred_element_type=F32)
            up = jnp.dot(x, wu_ref[...], preferred_element_type=F32)
            hidden = (gate * _sigmoid(gate)) * up
            return cw * jnp.dot(hidden.astype(MXU), wd_ref[...], preferred_element_type=F32)

        y_ref[...] = (expert(e1_ref[i], wg1_ref, wu1_ref, wd1_ref)
                      + expert(e2_ref[i], wg2_ref, wu2_ref, wd2_ref))

    @pl.when(i == nused - 1)
    def _():
        for k in range(1, nbuf):
            wait((i + k) % nbuf)

    @pl.when(i >= nused)
    def _():
        y_ref[...] = jnp.zeros_like(y_ref)


def _ple_kernel(pos_ref, h1_ref, p_ref, y_ref, gple_ref, wgate_ref, wproj_ref, gfin_ref,
                out_ref, ybuf, sem, *, final):
    rows = h1_ref.shape[0]
    i = pl.program_id(0)
    nsteps = pl.num_programs(0)
    slot = i % 2

    def gather(step, buf):
        def issue(g, carry):
            r0 = pl.multiple_of(g * ISSUE_UNROLL, ISSUE_UNROLL)
            group = ybuf.at[buf, pl.ds(r0, ISSUE_UNROLL)]
            for k in range(ISSUE_UNROLL):
                _row_copy(y_ref, pos_ref[step * rows + r0 + k], group, k,
                          sem.at[buf]).start(priority=GATHER_PRIORITY)
            return carry

        lax.fori_loop(0, rows // ISSUE_UNROLL, issue, 0)

    def wait(buf):
        pltpu.make_async_copy(y_ref.at[pl.ds(0, rows)], ybuf.at[buf], sem.at[buf]).wait()

    @pl.when(i == 0)
    def _():
        gather(0, 0)

    wait(slot)
    h2 = h1_ref[...] + ybuf[slot]
    nxt = jnp.minimum(i + 1, nsteps - 1)
    for r in range(rows):
        group = ybuf.at[1 - slot, pl.ds((r // 8) * 8, 8)]
        _row_copy(y_ref, pos_ref[nxt * rows + r], group, r % 8,
                  sem.at[1 - slot]).start(priority=GATHER_PRIORITY)

    proj = jnp.dot(p_ref[...].astype(MXU), wproj_ref[...], preferred_element_type=F32)
    gate = _sigmoid(jnp.dot(_rms(h2, gple_ref[...]).astype(MXU), wgate_ref[...],
                            preferred_element_type=F32))
    h3 = h2 + gate * proj
    if final:
        h3 = _rms(h3, gfin_ref[...])
    out_ref[...] = h3

    @pl.when(i == nsteps - 1)
    def _():
        wait(1 - slot)


def _const_spec(shape):
    nd = len(shape)
    return pl.BlockSpec(shape, lambda *_: (0,) * nd)


def _route_out(bsz, seq, tb):
    nt = seq // tb
    n = bsz * seq
    shapes = [
        jax.ShapeDtypeStruct((bsz, seq, D_MODEL), F32),
        jax.ShapeDtypeStruct((n, ROW_WORDS), F32),
        jax.ShapeDtypeStruct((bsz * nt, 1, tb), I32),
        jax.ShapeDtypeStruct((bsz * nt, 1, tb), I32),
        jax.ShapeDtypeStruct((CLASS_PAD, 128), F32),
    ]
    specs = [
        pl.BlockSpec((1, tb, D_MODEL), lambda b, t, *_: (b, t, 0)),
        pl.BlockSpec((tb, ROW_WORDS), lambda b, t, *_: (b * nt + t, 0)),
        pl.BlockSpec((1, 1, tb), lambda b, t, *_: (b * nt + t, 0, 0)),
        pl.BlockSpec((1, 1, tb), lambda b, t, *_: (b * nt + t, 0, 0)),
        pl.BlockSpec((CLASS_PAD, 128), lambda b, t, *_: (0, 0)),
    ]
    return shapes, specs


def _params():
    return pltpu.CompilerParams(dimension_semantics=("arbitrary", "arbitrary"),
                                vmem_limit_bytes=VMEM_LIMIT)


def _layer0_mixer(h, gmix, win, lb, onorm, convw, wout, gffn, wrt):
    bsz, seq, _ = h.shape
    tb = L0_BLOCK
    assert seq % tb == 0
    shapes, ospecs = _route_out(bsz, seq, tb)
    return pl.pallas_call(
        _l0_kernel,
        grid=(bsz, seq // tb),
        in_specs=[
            pl.BlockSpec((1, tb, D_MODEL), lambda b, t: (b, t, 0)),
            _const_spec((1, D_MODEL)),
            _const_spec((D_MODEL, AB_IN)),
            _const_spec((1, A_WIDTH)),
            _const_spec((1, A_DK)),
            _const_spec((3, A_WIDTH)),
            _const_spec((D_MODEL, D_MODEL)),
            _const_spec((1, D_MODEL)),
            _const_spec((D_MODEL, 128)),
        ],
        out_specs=ospecs,
        out_shape=shapes,
        scratch_shapes=[
            pltpu.VMEM((tb, AB_IN), F32),
            pltpu.VMEM((tb, D_MODEL), F32),
            pltpu.VMEM((A_HEADS, A_DK, A_DK), F32),
            pltpu.VMEM((8, A_WIDTH), F32),
            pltpu.VMEM((CLASS_PAD, 128), F32),
        ],
        compiler_params=_params(),
        name="layer0_mixer",
    )(h, gmix, win, lb, onorm, convw, wout, gffn, wrt)


def _layer1_mixer(h, sinks, gmix, win, bin_, wout, bout, gffn, wrt):
    bsz, seq, _ = h.shape
    tb = L1_BLOCK
    assert seq % tb == 0
    shapes, ospecs = _route_out(bsz, seq, tb)
    qw = Q_HEADS * HEAD_DIM
    vw = KV_HEADS * HEAD_DIM
    wqt, wk, wvt = win[:, :qw].T, win[:, qw:qw + vw], win[:, qw + vw:].T
    bqt, bk, bvt = bin_[:, :qw].reshape(qw, 1), bin_[:, qw:qw + vw], bin_[:, qw + vw:].reshape(vw, 1)
    grid_spec = pltpu.PrefetchScalarGridSpec(
        num_scalar_prefetch=1,
        grid=(bsz, seq // tb),
        in_specs=[
            pl.BlockSpec((1, tb, D_MODEL), lambda b, t, *_: (b, t, 0)),
            _const_spec((1, D_MODEL)),
            _const_spec((qw, D_MODEL)),
            _const_spec((qw, 1)),
            _const_spec((D_MODEL, vw)),
            _const_spec((1, vw)),
            _const_spec((vw, D_MODEL)),
            _const_spec((vw, 1)),
            _const_spec((D_MODEL, D_MODEL)),
            _const_spec((1, D_MODEL)),
            _const_spec((1, D_MODEL)),
            _const_spec((D_MODEL, 128)),
        ],
        out_specs=ospecs,
        scratch_shapes=[
            pltpu.VMEM((qw, tb), MXU),
            pltpu.VMEM((KV_HEADS, tb + WINDOW, HEAD_DIM), MXU),
            pltpu.VMEM((vw, tb + WINDOW), MXU),
            pltpu.VMEM((tb, D_MODEL), F32),
            pltpu.VMEM((CLASS_PAD, 128), F32),
        ],
    )
    return pl.pallas_call(
        _l1_kernel, grid_spec=grid_spec, out_shape=shapes, compiler_params=_params(),
        name="layer1_mixer",
    )(sinks, h, gmix, wqt, bqt, wk, bk, wvt, bvt, wout, bout, gffn, wrt)


def _moe(plan, xext, layer, wg, wu, wd):
    pos, e1, e2, nused, pad0, pad1, n_rows = plan
    tm = MOE_TILE
    n_tiles = n_rows // tm

    def wspec(shape, which):
        return pl.BlockSpec((None, None) + shape,
                            lambda i, e1, e2, *_: (layer, (e1, e2)[which][i], 0, 0))

    grid_spec = pltpu.PrefetchScalarGridSpec(
        num_scalar_prefetch=6,
        grid=(n_tiles,),
        in_specs=[
            pl.BlockSpec(memory_space=pl.ANY),
            wspec((D_MODEL, D_EXPERT), 0), wspec((D_MODEL, D_EXPERT), 0), wspec((D_EXPERT, D_MODEL), 0),
            wspec((D_MODEL, D_EXPERT), 1), wspec((D_MODEL, D_EXPERT), 1), wspec((D_EXPERT, D_MODEL), 1),
        ],
        out_specs=pl.BlockSpec((tm, D_MODEL), lambda i, *_: (i, 0)),
        scratch_shapes=[
            pltpu.SMEM((n_rows,), I32),
            pltpu.VMEM((MOE_BUFFERS, tm, ROW_WORDS), F32),
            pltpu.SemaphoreType.DMA((MOE_BUFFERS,)),
            pltpu.VMEM((tm, D_MODEL), MXU),
            pltpu.VMEM((tm, ROW_WORDS - D_MODEL), F32),
        ],
    )
    return pl.pallas_call(
        _moe_kernel, grid_spec=grid_spec,
        out_shape=jax.ShapeDtypeStruct((n_rows, D_MODEL), F32),
        compiler_params=pltpu.CompilerParams(dimension_semantics=("arbitrary",),
                                             vmem_limit_bytes=VMEM_LIMIT),
        name="moe_experts",
    )(e1, e2, nused, pos, pad0, pad1, xext, wg, wu, wd, wg, wu, wd)


def _combine_ple(pos, h1, p, layer, ys, gple, wgate, wproj, gfin, final):
    n = h1.shape[0]
    blk = MOVE_BLOCK
    assert n % blk == 0
    grid_spec = pltpu.PrefetchScalarGridSpec(
        num_scalar_prefetch=1,
        grid=(n // blk,),
        in_specs=[
            pl.BlockSpec((blk, D_MODEL), lambda i, *_: (i, 0)),
            pl.BlockSpec((None, blk, PLE_DIM), lambda i, *_: (layer, i, 0)),
            pl.BlockSpec(memory_space=pl.ANY),
            _const_spec((1, D_MODEL)),
            _const_spec((D_MODEL, D_MODEL)),
            _const_spec((PLE_DIM, D_MODEL)),
            _const_spec((1, D_MODEL)),
        ],
        out_specs=pl.BlockSpec((blk, D_MODEL), lambda i, *_: (i, 0)),
        scratch_shapes=[pltpu.VMEM((2, blk, D_MODEL), F32), pltpu.SemaphoreType.DMA((2,))],
    )
    return pl.pallas_call(
        functools.partial(_ple_kernel, final=final), grid_spec=grid_spec,
        out_shape=jax.ShapeDtypeStruct((n, D_MODEL), F32),
        compiler_params=pltpu.CompilerParams(dimension_semantics=("arbitrary",),
                                             vmem_limit_bytes=VMEM_LIMIT),
        name="combine_ple",
    )(pos, h1, p, ys, gple, wgate, wproj, gfin)


def _routing_plan(cls, rank, counts, n_tokens):
    tm = MOE_TILE
    n_tiles = n_tokens // tm + N_CLASS
    cnt = counts[:N_CLASS, 0].astype(I32)
    tiles_per_class = (cnt + tm - 1) // tm
    tile_end = jnp.cumsum(tiles_per_class)
    class_row0 = (tile_end - tiles_per_class) * tm
    pos = jnp.take(class_row0, cls.reshape(-1)) + rank.reshape(-1)
    nused = tile_end[-1]
    tidx = jnp.minimum(jnp.arange(n_tiles, dtype=I32), nused - 1)
    tcls = jnp.sum((tidx[:, None] >= tile_end[None, :]).astype(I32), axis=1)
    tcls = jnp.minimum(tcls, N_CLASS - 1)
    lo = jnp.array([a for a, _ in PAIRS], I32)
    hi = jnp.array([b for _, b in PAIRS], I32)
    grp = tcls // len(PAIRS)
    e1 = grp * PER_GROUP + jnp.take(lo, tcls % len(PAIRS))
    e2 = grp * PER_GROUP + jnp.take(hi, tcls % len(PAIRS))
    pad0 = class_row0 + cnt
    pad1 = tile_end * tm
    return (pos.astype(I32), e1.astype(I32), e2.astype(I32), nused.reshape(1).astype(I32),
            pad0.astype(I32), pad1.astype(I32), n_tiles * tm)


def _ffn_and_ple(h1, xext, cls, rank, counts, p, layer, wg, wu, wd, gple, wgate, wproj, gfin,
                 final):
    bsz, seq, _ = h1.shape
    n = bsz * seq
    plan = _routing_plan(cls, rank, counts, n)
    ys = _moe(plan, xext, layer, wg, wu, wd)
    out = _combine_ple(plan[0], h1.reshape(n, D_MODEL), p.reshape(p.shape[0], n, PLE_DIM), layer,
                       ys, gple, wgate, wproj, gfin, final)
    return out.reshape(bsz, seq, D_MODEL)


def kernel(x, p, norm_mix, norm_ffn, norm_ple, norm_final, w_in_ab, hgrn_lb_logits, hgrn_out_norm, conv_w, w_out_ab, w_in_c, b_in_c, sinks, w_out_c, b_out_c, w_router, w_gate_e, w_up_e, w_down_e, w_ple_gate, w_ple_proj):
    depth = p.shape[0]
    lower_bounds = jnp.cumsum(jax.nn.softmax(hgrn_lb_logits.astype(F32), axis=0), axis=0)
    wr32 = w_router.astype(F32)
    wr_hi = wr32.astype(BF16)
    wr_lo = (wr32 - wr_hi.astype(F32)).astype(BF16)
    wrt = jnp.concatenate(
        [wr_hi, wr_lo, jnp.zeros((D_MODEL, 128 - 2 * N_EXPERTS), BF16)], axis=1)
    gfin = norm_final.reshape(1, D_MODEL)
    wg, wu, wd = w_gate_e.astype(MXU), w_up_e.astype(MXU), w_down_e.astype(MXU)
    h = x
    for i in range(depth):
        j = i // 2
        gmix = norm_mix[i].reshape(1, D_MODEL)
        gffn = norm_ffn[i].reshape(1, D_MODEL)
        if i % 2 == 0:
            h1, xext, cls, rank, counts = _layer0_mixer(
                h, gmix, w_in_ab[j].astype(MXU), lower_bounds[i].reshape(1, A_WIDTH),
                hgrn_out_norm[j].reshape(1, A_DK), conv_w[j], w_out_ab[j].astype(MXU), gffn, wrt)
        else:
            h1, xext, cls, rank, counts = _layer1_mixer(
                h, sinks[j].astype(F32), gmix, w_in_c[j].astype(MXU), b_in_c[j].reshape(1, C_IN),
                w_out_c[j].astype(MXU), b_out_c[j].reshape(1, D_MODEL), gffn, wrt)
        h = _ffn_and_ple(
            h1, xext, cls, rank, counts, p, i, wg, wu, wd,
            norm_ple[i].reshape(1, D_MODEL), w_ple_gate[i].astype(MXU), w_ple_proj[i].astype(MXU),
            gfin, final=(i == depth - 1))
    return h
```

```python
import functools

import jax
import jax.numpy as jnp
from jax import lax
from jax.experimental import pallas as pl
from jax.experimental.pallas import tpu as pltpu

F32 = jnp.float32
BF16 = jnp.bfloat16
MXU = jnp.bfloat16
I32 = jnp.int32

D_MODEL = 1024
EPS = 1e-6
A_WIDTH = 512
A_HEADS = 4
A_DK = 128
A_CHUNK = 32
AB_IN = 3584
HEAD_DIM = 64
Q_HEADS = 16
KV_HEADS = 2
GROUP = 8
WINDOW = 128
C_IN = 1280
N_EXPERTS = 16
N_GROUPS = 4
PER_GROUP = 4
D_EXPERT = 512
PLE_DIM = 256

PAIRS = ((0, 1), (0, 2), (0, 3), (1, 2), (1, 3), (2, 3))
N_CLASS = N_GROUPS * len(PAIRS)
CLASS_PAD = 32
ROW_WORDS = D_MODEL + 128

L0_BLOCK = 256
L1_BLOCK = 512
MOE_TILE = 256
MOVE_BLOCK = 512
VMEM_LIMIT = 56 * 1024 * 1024

NT_DIMS = (((1,), (1,)), ((), ()))
TN_DIMS = (((0,), (0,)), ((), ()))


def _rms(x, g):
    ms = jnp.mean(x * x, axis=-1, keepdims=True)
    return x * lax.rsqrt(ms + EPS) * g


def _sigmoid(x):
    return 1.0 / (1.0 + jnp.exp(-x))


def _route_tail(h1, first, gffn_ref, wr_ref, xext_ref, cls_ref, rank_ref, cnt_ref, run_scr):
    tb = h1.shape[0]

    @pl.when(first)
    def _():
        run_scr[...] = jnp.zeros_like(run_scr)

    xn = _rms(h1, gffn_ref[...])
    x_hi = xn.astype(BF16)
    x_lo = (xn - x_hi.astype(F32)).astype(BF16)
    prod = (jnp.dot(x_hi, wr_ref[...], preferred_element_type=F32)
            + jnp.dot(x_lo, wr_ref[...], preferred_element_type=F32)).T
    logits = prod[0:N_EXPERTS, :] + prod[N_EXPERTS:2 * N_EXPERTS, :]
    mx = jnp.max(logits, axis=0, keepdims=True)
    ex = jnp.exp(logits - mx)
    sc = ex / jnp.sum(ex, axis=0, keepdims=True)
    rows = [sc[i:i + 1, :] for i in range(N_EXPERTS)]

    gscore = []
    for g in range(N_GROUPS):
        v = rows[PER_GROUP * g:PER_GROUP * (g + 1)]
        best = v[0] + v[1]
        for (i, j) in PAIRS[1:]:
            best = jnp.maximum(best, v[i] + v[j])
        gscore.append(best)
    gmax = jnp.maximum(jnp.maximum(gscore[0], gscore[1]), jnp.maximum(gscore[2], gscore[3]))
    sel = jnp.where(gscore[0] >= gmax, 0, jnp.where(gscore[1] >= gmax, 1,
                                                    jnp.where(gscore[2] >= gmax, 2, 3))).astype(I32)
    v = [jnp.where(sel == 0, rows[i], jnp.where(sel == 1, rows[4 + i],
                                                jnp.where(sel == 2, rows[8 + i], rows[12 + i])))
         for i in range(PER_GROUP)]
    chosen = []
    for i in range(PER_GROUP):
        r = jnp.zeros_like(sel)
        for j in range(PER_GROUP):
            if j == i:
                continue
            ahead = (v[j] >= v[i]) if j < i else (v[j] > v[i])
            r = r + jnp.where(ahead, 1, 0).astype(I32)
        chosen.append(r < 2)
    vsum = jnp.zeros_like(v[0])
    code = jnp.zeros_like(sel)
    for i in range(PER_GROUP):
        vsum = vsum + jnp.where(chosen[i], v[i], 0.0)
        code = code + jnp.where(chosen[i], 1 << i, 0).astype(I32)
    wts = [jnp.where(chosen[i], v[i] / vsum, 0.0) for i in range(PER_GROUP)]
    pair = jnp.where(code == 3, 0, jnp.where(code == 5, 1, jnp.where(code == 9, 2,
                     jnp.where(code == 6, 3, jnp.where(code == 10, 4, 5))))).astype(I32)
    cls = sel * len(PAIRS) + pair

    erow = lax.broadcasted_iota(I32, (128, tb), 0)
    comb_t = jnp.zeros((128, tb), F32)
    for i in range(PER_GROUP):
        comb_t = comb_t + jnp.where(erow == sel * PER_GROUP + i, wts[i], 0.0)
    comb = comb_t.T

    xext_ref[:, :D_MODEL] = xn
    xext_ref[:, D_MODEL:] = comb

    crow = lax.broadcasted_iota(I32, (CLASS_PAD, tb), 0)
    onehot = jnp.where(crow == cls, 1.0, 0.0)
    ii = lax.broadcasted_iota(I32, (tb, tb), 0)
    jj = lax.broadcasted_iota(I32, (tb, tb), 1)
    upper = jnp.where(ii < jj, 1.0, 0.0).astype(MXU)
    prefix = jnp.dot(onehot.astype(MXU), upper, preferred_element_type=F32)
    run = run_scr[...]
    rank = jnp.sum(onehot * (prefix + run[:, 0:1]), axis=0, keepdims=True)
    run_new = run + jnp.sum(onehot, axis=1, keepdims=True)
    run_scr[...] = run_new
    cnt_ref[...] = run_new
    cls_ref[0] = cls
    rank_ref[0] = rank.astype(I32)


def _l0_kernel(h_ref, gmix_ref, win_ref, lb_ref, onorm_ref, convw_ref, wout_ref, gffn_ref, wr_ref,
               h1_ref, xext_ref, cls_ref, rank_ref, cnt_ref,
               z_scr, cat_scr, st_scr, carry_scr, run_scr):
    b = pl.program_id(0)
    t = pl.program_id(1)
    tb = h_ref.shape[1]

    @pl.when(t == 0)
    def _():
        st_scr[...] = jnp.zeros_like(st_scr)
        carry_scr[...] = jnp.zeros_like(carry_scr)

    h = h_ref[0]
    u = _rms(h, gmix_ref[...]).astype(MXU)
    z_scr[...] = jnp.dot(u, win_ref[...], preferred_element_type=F32)

    gate_b = z_scr[:, 4 * A_WIDTH:5 * A_WIDTH]
    cb = z_scr[:, 5 * A_WIDTH:6 * A_WIDTH] * z_scr[:, 6 * A_WIDTH:7 * A_WIDTH]
    row = lax.broadcasted_iota(I32, cb.shape, 0)
    prev1 = carry_scr[7:8, :]
    prev2 = carry_scr[6:7, :]
    m1 = jnp.where(row == 0, prev1, pltpu.roll(cb, 1, axis=0))
    m2 = jnp.where(row == 0, prev2, jnp.where(row == 1, prev1, pltpu.roll(cb, 2, axis=0)))
    cw = convw_ref[...]
    cat_scr[:, A_WIDTH:] = gate_b * (cw[0:1, :] * m2 + cw[1:2, :] * m1 + cw[2:3, :] * cb)
    carry_scr[...] = cb[tb - 8:tb, :]

    nc = tb // A_CHUNK
    lb = lb_ref[...]
    onorm = onorm_ref[...]
    zq = z_scr[:, 0:A_WIDTH]
    zf = z_scr[:, A_WIDTH:2 * A_WIDTH]
    zi = z_scr[:, 2 * A_WIDTH:3 * A_WIDTH]
    zg = z_scr[:, 3 * A_WIDTH:4 * A_WIDTH]
    f = lb + (1.0 - lb) * _sigmoid(zf)
    logf = jnp.log(f)
    ri = lax.broadcasted_iota(I32, (tb, tb), 0)
    rj = lax.broadcasted_iota(I32, (tb, tb), 1)
    same_chunk = (ri // A_CHUNK) == (rj // A_CHUNK)
    causal = same_chunk & (ri >= rj)
    tri = jnp.where(causal, 1.0, 0.0).astype(BF16)
    p1 = logf.astype(BF16)
    r1 = logf - p1.astype(F32)
    p2 = r1.astype(BF16)
    p3 = (r1 - p2.astype(F32)).astype(BF16)
    pieces = jnp.dot(tri, jnp.concatenate([p1, p2, p3], axis=1), preferred_element_type=F32)
    bcum = (pieces[:, 0:A_WIDTH] + pieces[:, A_WIDTH:2 * A_WIDTH]) + pieces[:, 2 * A_WIDTH:]
    blast3 = bcum.reshape(nc, A_CHUNK, A_WIDTH)[:, A_CHUNK - 1:A_CHUNK, :]
    blast = jnp.broadcast_to(blast3, (nc, A_CHUNK, A_WIDTH)).reshape(tb, A_WIDTH)
    k = 1.0 - f
    q_dec = ((zq * _sigmoid(zq)) * jnp.exp(bcum)).astype(MXU)
    k_inv = (k * jnp.exp(-bcum)).astype(MXU)
    k_tail = (k * jnp.exp(blast - bcum)).astype(MXU)
    decay = jnp.exp(blast3)
    gate = _sigmoid(zg)
    vals = zi.astype(MXU)
    in_block = (lax.broadcasted_iota(I32, (tb, nc * A_DK), 0) // A_CHUNK
                == lax.broadcasted_iota(I32, (tb, nc * A_DK), 1) // A_DK)
    zero = jnp.zeros((), MXU)
    for hd in range(A_HEADS):
        sl = slice(hd * A_DK, (hd + 1) * A_DK)
        qd = q_dec[:, sl]
        vv = vals[:, sl]
        scores = lax.dot_general(qd, k_inv[:, sl], NT_DIMS, preferred_element_type=F32)
        scores = jnp.where(causal, scores, 0.0).astype(MXU)
        o = jnp.dot(scores, vv, preferred_element_type=F32)
        kt_blocks = jnp.where(in_block, jnp.tile(k_tail[:, sl], (1, nc)), zero)
        kv_all = lax.dot_general(vv, kt_blocks, TN_DIMS, preferred_element_type=F32)
        st = st_scr[hd]
        states = []
        for c in range(nc):
            cs = slice(c * A_DK, (c + 1) * A_DK)
            states.append(st.astype(MXU))
            st = st * decay[c, :, sl] + kv_all[:, cs]
        st_scr[hd] = st
        qd_blocks = jnp.where(in_block, jnp.tile(qd, (1, nc)), zero)
        o = o + lax.dot_general(qd_blocks, jnp.concatenate(states, axis=1), NT_DIMS,
                                preferred_element_type=F32)
        cat_scr[:, sl] = _rms(o, onorm) * gate[:, sl]

    h1 = h + jnp.dot(cat_scr[...].astype(MXU), wout_ref[...], preferred_element_type=F32)
    h1_ref[0] = h1
    _route_tail(h1, (b == 0) & (t == 0), gffn_ref, wr_ref, xext_ref, cls_ref, rank_ref, cnt_ref,
                run_scr)


def _l1_kernel(sinks_ref, h_ref, gmix_ref, wqt_ref, bqt_ref, wk_ref, bk_ref, wvt_ref, bvt_ref,
               wout_ref, bout_ref, gffn_ref, wr_ref,
               h1_ref, xext_ref, cls_ref, rank_ref, cnt_ref,
               qt_scr, k_scr, vt_scr, att_scr, run_scr):
    b = pl.program_id(0)
    t = pl.program_id(1)
    tb = h_ref.shape[1]

    @pl.when(t == 0)
    def _():
        k_scr[:, 0:WINDOW, :] = jnp.zeros((KV_HEADS, WINDOW, HEAD_DIM), k_scr.dtype)
        vt_scr[:, 0:WINDOW] = jnp.zeros((KV_HEADS * HEAD_DIM, WINDOW), vt_scr.dtype)

    h = h_ref[0]
    u = _rms(h, gmix_ref[...]).astype(MXU)
    scale = HEAD_DIM ** -0.5
    qt_scr[...] = ((lax.dot_general(wqt_ref[...], u, NT_DIMS, preferred_element_type=F32)
                    + bqt_ref[...]) * scale).astype(qt_scr.dtype)
    kz = (jnp.dot(u, wk_ref[...], preferred_element_type=F32) + bk_ref[...]).astype(k_scr.dtype)
    for kk in range(KV_HEADS):
        k_scr[kk, WINDOW:, :] = kz[:, kk * HEAD_DIM:(kk + 1) * HEAD_DIM]
    vt_scr[:, WINDOW:] = (lax.dot_general(wvt_ref[...], u, NT_DIMS, preferred_element_type=F32)
                          + bvt_ref[...]).astype(vt_scr.dtype)

    kj = lax.broadcasted_iota(I32, (2 * WINDOW, WINDOW), 0)
    qi = lax.broadcasted_iota(I32, (2 * WINDOW, WINDOW), 1)
    band = (kj > qi) & (kj <= qi + WINDOW)
    first_key = jnp.where(t > 0, 0, WINDOW)

    def all_heads(keep):
        return jnp.tile(jnp.where(keep, 1.0, 0.0), (1, GROUP)) > 0.5

    mask_first = all_heads(band & (kj >= first_key))
    mask_rest = all_heads(band)
    for n in range(tb // WINDOW):
        rs = slice(n * WINDOW, (n + 1) * WINDOW)
        win = slice(n * WINDOW, (n + 2) * WINDOW)
        mask = mask_first if n == 0 else mask_rest
        for kk in range(KV_HEADS):
            heads = range(kk * GROUP, (kk + 1) * GROUP)
            keys = k_scr[kk, win, :]
            vals_t = vt_scr[kk * HEAD_DIM:(kk + 1) * HEAD_DIM, win]
            q_t = jnp.concatenate(
                [qt_scr[hd * HEAD_DIM:(hd + 1) * HEAD_DIM, rs] for hd in heads], axis=1)
            sink = jnp.concatenate(
                [jnp.full((1, WINDOW), sinks_ref[hd], F32) for hd in heads], axis=1)
            s = jnp.dot(keys, q_t, preferred_element_type=F32)
            s = jnp.where(mask, s, -jnp.inf)
            m = jnp.maximum(jnp.max(s, axis=0, keepdims=True), sink)
            e = jnp.exp(s - m)
            denom = jnp.sum(e, axis=0, keepdims=True) + jnp.exp(sink - m)
            o_t = jnp.dot(vals_t, e.astype(MXU), preferred_element_type=F32) / denom
            for g2 in range(GROUP // 2):
                pair = jnp.concatenate([o_t[:, (2 * g2) * WINDOW:(2 * g2 + 1) * WINDOW],
                                        o_t[:, (2 * g2 + 1) * WINDOW:(2 * g2 + 2) * WINDOW]], axis=0)
                h0 = (kk * GROUP + 2 * g2) * HEAD_DIM
                att_scr[rs, h0:h0 + 2 * HEAD_DIM] = pair.T
    k_scr[:, 0:WINDOW, :] = k_scr[:, tb:tb + WINDOW, :]
    vt_scr[:, 0:WINDOW] = vt_scr[:, tb:tb + WINDOW]

    h1 = (h + jnp.dot(att_scr[...].astype(MXU), wout_ref[...], preferred_element_type=F32)
          + bout_ref[...])
    h1_ref[0] = h1
    _route_tail(h1, (b == 0) & (t == 0), gffn_ref, wr_ref, xext_ref, cls_ref, rank_ref, cnt_ref,
                run_scr)


def _row_copy(src, src_row, dst, dst_row, sem):
    return pltpu.make_async_copy(src.at[pl.ds(src_row, 1)], dst.at[pl.ds(dst_row, 1)], sem)


ISSUE_UNROLL = 8
DMA_THREADS = 2


def _scatter_kernel(pos_ref, pad0_ref, pad1_ref, nused_ref, x_ref, out_ref, zero_scr, sem,
                    pad_sem):
    rows = x_ref.shape[0]
    tm = zero_scr.shape[0]
    i = pl.program_id(0)
    base = i * rows

    @pl.when(i == 0)
    def _():
        zero_scr[...] = jnp.zeros_like(zero_scr)

        def zero_tile(tile, carry):
            dst = out_ref.at[pl.ds(pl.multiple_of(tile * tm, tm), tm)]
            pltpu.make_async_copy(zero_scr, dst, pad_sem).start()
            pltpu.make_async_copy(zero_scr, dst, pad_sem).wait()
            return carry

        lax.fori_loop(nused_ref[0], out_ref.shape[0] // tm, zero_tile, 0)

        def pad_class(c, carry):
            def start(r, carry):
                _row_copy(zero_scr, 0, out_ref, r, pad_sem).start()
                return carry

            def finish(r, carry):
                _row_copy(zero_scr, 0, out_ref, r, pad_sem).wait()
                return carry

            lax.fori_loop(pad0_ref[c], pad1_ref[c], start, 0)
            return lax.fori_loop(pad0_ref[c], pad1_ref[c], finish, carry)

        lax.fori_loop(0, N_CLASS, pad_class, 0)

    def issue(g, carry):
        r0 = pl.multiple_of(g * ISSUE_UNROLL, ISSUE_UNROLL)
        group = x_ref.at[pl.ds(r0, ISSUE_UNROLL)]
        for k in range(ISSUE_UNROLL):
            _row_copy(group, k, out_ref, pos_ref[base + r0 + k],
                      sem).start(priority=k % DMA_THREADS)
        return carry

    lax.fori_loop(0, rows // ISSUE_UNROLL, issue, 0)
    pltpu.make_async_copy(x_ref, out_ref.at[pl.ds(0, rows)], sem).wait()


def _moe_kernel(e1_ref, e2_ref, nused_ref,
                x_ref, wg1_ref, wu1_ref, wd1_ref, wg2_ref, wu2_ref, wd2_ref, y_ref):
    i = pl.program_id(0)
    nused = nused_ref[0]

    @pl.when(i < nused)
    def _():
        x = x_ref[:, :D_MODEL].astype(MXU)
        comb = x_ref[:, D_MODEL:]
        lane = lax.broadcasted_iota(I32, comb.shape, 1)

        def expert(e, wg_ref, wu_ref, wd_ref):
            cw = jnp.sum(jnp.where(lane == e, comb, 0.0), axis=1, keepdims=True)
            gate = jnp.dot(x, wg_ref[...], preferred_element_type=F32)
            up = jnp.dot(x, wu_ref[...], preferred_element_type=F32)
            hidden = (gate * _sigmoid(gate)) * up
            return cw * jnp.dot(hidden.astype(MXU), wd_ref[...], preferred_element_type=F32)

        y_ref[...] = (expert(e1_ref[i], wg1_ref, wu1_ref, wd1_ref)
                      + expert(e2_ref[i], wg2_ref, wu2_ref, wd2_ref))

    @pl.when(i >= nused)
    def _():
        y_ref[...] = jnp.zeros_like(y_ref)


def _ple_kernel(pos_ref, h1_ref, p_ref, y_ref, gple_ref, wgate_ref, wproj_ref, gfin_ref,
                out_ref, ybuf, sem, *, final):
    rows = h1_ref.shape[0]
    i = pl.program_id(0)
    nsteps = pl.num_programs(0)
    slot = i % 2

    def gather(step, buf):
        def issue(g, carry):
            r0 = pl.multiple_of(g * ISSUE_UNROLL, ISSUE_UNROLL)
            group = ybuf.at[buf, pl.ds(r0, ISSUE_UNROLL)]
            for k in range(ISSUE_UNROLL):
                _row_copy(y_ref, pos_ref[step * rows + r0 + k], group, k,
                          sem.at[buf]).start(priority=k % DMA_THREADS)
            return carry

        lax.fori_loop(0, rows // ISSUE_UNROLL, issue, 0)

    def wait(buf):
        pltpu.make_async_copy(y_ref.at[pl.ds(0, rows)], ybuf.at[buf], sem.at[buf]).wait()

    @pl.when(i == 0)
    def _():
        gather(0, 0)

    wait(slot)
    h2 = h1_ref[...] + ybuf[slot]
    nxt = jnp.minimum(i + 1, nsteps - 1)
    for r in range(rows):
        group = ybuf.at[1 - slot, pl.ds((r // 8) * 8, 8)]
        _row_copy(y_ref, pos_ref[nxt * rows + r], group, r % 8,
                  sem.at[1 - slot]).start(priority=r % DMA_THREADS)

    proj = jnp.dot(p_ref[...].astype(MXU), wproj_ref[...], preferred_element_type=F32)
    gate = _sigmoid(jnp.dot(_rms(h2, gple_ref[...]).astype(MXU), wgate_ref[...],
                            preferred_element_type=F32))
    h3 = h2 + gate * proj
    if final:
        h3 = _rms(h3, gfin_ref[...])
    out_ref[...] = h3

    @pl.when(i == nsteps - 1)
    def _():
        wait(1 - slot)


def _const_spec(shape):
    nd = len(shape)
    return pl.BlockSpec(shape, lambda *_: (0,) * nd)


def _route_out(bsz, seq, tb):
    nt = seq // tb
    n = bsz * seq
    shapes = [
        jax.ShapeDtypeStruct((bsz, seq, D_MODEL), F32),
        jax.ShapeDtypeStruct((n, ROW_WORDS), F32),
        jax.ShapeDtypeStruct((bsz * nt, 1, tb), I32),
        jax.ShapeDtypeStruct((bsz * nt, 1, tb), I32),
        jax.ShapeDtypeStruct((CLASS_PAD, 128), F32),
    ]
    specs = [
        pl.BlockSpec((1, tb, D_MODEL), lambda b, t, *_: (b, t, 0)),
        pl.BlockSpec((tb, ROW_WORDS), lambda b, t, *_: (b * nt + t, 0)),
        pl.BlockSpec((1, 1, tb), lambda b, t, *_: (b * nt + t, 0, 0)),
        pl.BlockSpec((1, 1, tb), lambda b, t, *_: (b * nt + t, 0, 0)),
        pl.BlockSpec((CLASS_PAD, 128), lambda b, t, *_: (0, 0)),
    ]
    return shapes, specs


def _params():
    return pltpu.CompilerParams(dimension_semantics=("arbitrary", "arbitrary"),
                                vmem_limit_bytes=VMEM_LIMIT)


def _layer0_mixer(h, gmix, win, lb, onorm, convw, wout, gffn, wrt):
    bsz, seq, _ = h.shape
    tb = L0_BLOCK
    assert seq % tb == 0
    shapes, ospecs = _route_out(bsz, seq, tb)
    return pl.pallas_call(
        _l0_kernel,
        grid=(bsz, seq // tb),
        in_specs=[
            pl.BlockSpec((1, tb, D_MODEL), lambda b, t: (b, t, 0)),
            _const_spec((1, D_MODEL)),
            _const_spec((D_MODEL, AB_IN)),
            _const_spec((1, A_WIDTH)),
            _const_spec((1, A_DK)),
            _const_spec((3, A_WIDTH)),
            _const_spec((D_MODEL, D_MODEL)),
            _const_spec((1, D_MODEL)),
            _const_spec((D_MODEL, 128)),
        ],
        out_specs=ospecs,
        out_shape=shapes,
        scratch_shapes=[
            pltpu.VMEM((tb, AB_IN), F32),
            pltpu.VMEM((tb, D_MODEL), F32),
            pltpu.VMEM((A_HEADS, A_DK, A_DK), F32),
            pltpu.VMEM((8, A_WIDTH), F32),
            pltpu.VMEM((CLASS_PAD, 128), F32),
        ],
        compiler_params=_params(),
        name="layer0_mixer",
    )(h, gmix, win, lb, onorm, convw, wout, gffn, wrt)


def _layer1_mixer(h, sinks, gmix, win, bin_, wout, bout, gffn, wrt):
    bsz, seq, _ = h.shape
    tb = L1_BLOCK
    assert seq % tb == 0
    shapes, ospecs = _route_out(bsz, seq, tb)
    qw = Q_HEADS * HEAD_DIM
    vw = KV_HEADS * HEAD_DIM
    wqt, wk, wvt = win[:, :qw].T, win[:, qw:qw + vw], win[:, qw + vw:].T
    bqt, bk, bvt = bin_[:, :qw].reshape(qw, 1), bin_[:, qw:qw + vw], bin_[:, qw + vw:].reshape(vw, 1)
    grid_spec = pltpu.PrefetchScalarGridSpec(
        num_scalar_prefetch=1,
        grid=(bsz, seq // tb),
        in_specs=[
            pl.BlockSpec((1, tb, D_MODEL), lambda b, t, *_: (b, t, 0)),
            _const_spec((1, D_MODEL)),
            _const_spec((qw, D_MODEL)),
            _const_spec((qw, 1)),
            _const_spec((D_MODEL, vw)),
            _const_spec((1, vw)),
            _const_spec((vw, D_MODEL)),
            _const_spec((vw, 1)),
            _const_spec((D_MODEL, D_MODEL)),
            _const_spec((1, D_MODEL)),
            _const_spec((1, D_MODEL)),
            _const_spec((D_MODEL, 128)),
        ],
        out_specs=ospecs,
        scratch_shapes=[
            pltpu.VMEM((qw, tb), MXU),
            pltpu.VMEM((KV_HEADS, tb + WINDOW, HEAD_DIM), MXU),
            pltpu.VMEM((vw, tb + WINDOW), MXU),
            pltpu.VMEM((tb, D_MODEL), F32),
            pltpu.VMEM((CLASS_PAD, 128), F32),
        ],
    )
    return pl.pallas_call(
        _l1_kernel, grid_spec=grid_spec, out_shape=shapes, compiler_params=_params(),
        name="layer1_mixer",
    )(sinks, h, gmix, wqt, bqt, wk, bk, wvt, bvt, wout, bout, gffn, wrt)


def _dispatch(plan, xext):
    pos, _, _, nused, pad0, pad1, n_rows = plan
    n = xext.shape[0]
    blk = MOVE_BLOCK
    assert n % blk == 0
    grid_spec = pltpu.PrefetchScalarGridSpec(
        num_scalar_prefetch=4,
        grid=(n // blk,),
        in_specs=[pl.BlockSpec((blk, ROW_WORDS), lambda i, *_: (i, 0))],
        out_specs=pl.BlockSpec(memory_space=pl.ANY),
        scratch_shapes=[pltpu.VMEM((MOE_TILE, ROW_WORDS), F32), pltpu.SemaphoreType.DMA(()),
                        pltpu.SemaphoreType.DMA(())],
    )
    return pl.pallas_call(
        _scatter_kernel, grid_spec=grid_spec,
        out_shape=jax.ShapeDtypeStruct((n_rows, ROW_WORDS), F32),
        compiler_params=pltpu.CompilerParams(dimension_semantics=("arbitrary",),
                                             vmem_limit_bytes=VMEM_LIMIT),
        name="moe_dispatch",
    )(pos, pad0, pad1, nused, xext)


def _moe(plan, xs, layer, wg, wu, wd):
    _, e1, e2, nused, _, _, n_rows = plan
    tm = MOE_TILE
    n_tiles = n_rows // tm

    def wspec(shape, which):
        return pl.BlockSpec((None, None) + shape,
                            lambda i, e1, e2, *_: (layer, (e1, e2)[which][i], 0, 0))

    grid_spec = pltpu.PrefetchScalarGridSpec(
        num_scalar_prefetch=3,
        grid=(n_tiles,),
        in_specs=[
            pl.BlockSpec((tm, ROW_WORDS), lambda i, *_: (i, 0)),
            wspec((D_MODEL, D_EXPERT), 0), wspec((D_MODEL, D_EXPERT), 0), wspec((D_EXPERT, D_MODEL), 0),
            wspec((D_MODEL, D_EXPERT), 1), wspec((D_MODEL, D_EXPERT), 1), wspec((D_EXPERT, D_MODEL), 1),
        ],
        out_specs=pl.BlockSpec((tm, D_MODEL), lambda i, *_: (i, 0)),
    )
    return pl.pallas_call(
        _moe_kernel, grid_spec=grid_spec,
        out_shape=jax.ShapeDtypeStruct((n_rows, D_MODEL), F32),
        compiler_params=pltpu.CompilerParams(dimension_semantics=("arbitrary",),
                                             vmem_limit_bytes=VMEM_LIMIT),
        name="moe_experts",
    )(e1, e2, nused, xs, wg, wu, wd, wg, wu, wd)


def _combine_ple(pos, h1, p, layer, ys, gple, wgate, wproj, gfin, final):
    n = h1.shape[0]
    blk = MOVE_BLOCK
    assert n % blk == 0
    grid_spec = pltpu.PrefetchScalarGridSpec(
        num_scalar_prefetch=1,
        grid=(n // blk,),
        in_specs=[
            pl.BlockSpec((blk, D_MODEL), lambda i, *_: (i, 0)),
            pl.BlockSpec((None, blk, PLE_DIM), lambda i, *_: (layer, i, 0)),
            pl.BlockSpec(memory_space=pl.ANY),
            _const_spec((1, D_MODEL)),
            _const_spec((D_MODEL, D_MODEL)),
            _const_spec((PLE_DIM, D_MODEL)),
            _const_spec((1, D_MODEL)),
        ],
        out_specs=pl.BlockSpec((blk, D_MODEL), lambda i, *_: (i, 0)),
        scratch_shapes=[pltpu.VMEM((2, blk, D_MODEL), F32), pltpu.SemaphoreType.DMA((2,))],
    )
    return pl.pallas_call(
        functools.partial(_ple_kernel, final=final), grid_spec=grid_spec,
        out_shape=jax.ShapeDtypeStruct((n, D_MODEL), F32),
        compiler_params=pltpu.CompilerParams(dimension_semantics=("arbitrary",),
                                             vmem_limit_bytes=VMEM_LIMIT),
        name="combine_ple",
    )(pos, h1, p, ys, gple, wgate, wproj, gfin)


def _routing_plan(cls, rank, counts, n_tokens):
    tm = MOE_TILE
    n_tiles = n_tokens // tm + N_CLASS
    cnt = counts[:N_CLASS, 0].astype(I32)
    tiles_per_class = (cnt + tm - 1) // tm
    tile_end = jnp.cumsum(tiles_per_class)
    class_row0 = (tile_end - tiles_per_class) * tm
    pos = jnp.take(class_row0, cls.reshape(-1)) + rank.reshape(-1)
    nused = tile_end[-1]
    tidx = jnp.minimum(jnp.arange(n_tiles, dtype=I32), nused - 1)
    tcls = jnp.sum((tidx[:, None] >= tile_end[None, :]).astype(I32), axis=1)
    tcls = jnp.minimum(tcls, N_CLASS - 1)
    lo = jnp.array([a for a, _ in PAIRS], I32)
    hi = jnp.array([b for _, b in PAIRS], I32)
    grp = tcls // len(PAIRS)
    e1 = grp * PER_GROUP + jnp.take(lo, tcls % len(PAIRS))
    e2 = grp * PER_GROUP + jnp.take(hi, tcls % len(PAIRS))
    pad0 = class_row0 + cnt
    pad1 = tile_end * tm
    return (pos.astype(I32), e1.astype(I32), e2.astype(I32), nused.reshape(1).astype(I32),
            pad0.astype(I32), pad1.astype(I32), n_tiles * tm)


def _ffn_and_ple(h1, xext, cls, rank, counts, p, layer, wg, wu, wd, gple, wgate, wproj, gfin,
                 final):
    bsz, seq, _ = h1.shape
    n = bsz * seq
    plan = _routing_plan(cls, rank, counts, n)
    ys = _moe(plan, _dispatch(plan, xext), layer, wg, wu, wd)
    out = _combine_ple(plan[0], h1.reshape(n, D_MODEL), p.reshape(p.shape[0], n, PLE_DIM), layer,
                       ys, gple, wgate, wproj, gfin, final)
    return out.reshape(bsz, seq, D_MODEL)


def kernel(x, p, norm_mix, norm_ffn, norm_ple, norm_final, w_in_ab, hgrn_lb_logits, hgrn_out_norm, conv_w, w_out_ab, w_in_c, b_in_c, sinks, w_out_c, b_out_c, w_router, w_gate_e, w_up_e, w_down_e, w_ple_gate, w_ple_proj):
    depth = p.shape[0]
    lower_bounds = jnp.cumsum(jax.nn.softmax(hgrn_lb_logits.astype(F32), axis=0), axis=0)
    wr32 = w_router.astype(F32)
    wr_hi = wr32.astype(BF16)
    wr_lo = (wr32 - wr_hi.astype(F32)).astype(BF16)
    wrt = jnp.concatenate(
        [wr_hi, wr_lo, jnp.zeros((D_MODEL, 128 - 2 * N_EXPERTS), BF16)], axis=1)
    gfin = norm_final.reshape(1, D_MODEL)
    wg, wu, wd = w_gate_e.astype(MXU), w_up_e.astype(MXU), w_down_e.astype(MXU)
    h = x
    for i in range(depth):
        j = i // 2
        gmix = norm_mix[i].reshape(1, D_MODEL)
        gffn = norm_ffn[i].reshape(1, D_MODEL)
        if i % 2 == 0:
            h1, xext, cls, rank, counts = _layer0_mixer(
                h, gmix, w_in_ab[j].astype(MXU), lower_bounds[i].reshape(1, A_WIDTH),
                hgrn_out_norm[j].reshape(1, A_DK), conv_w[j], w_out_ab[j].astype(MXU), gffn, wrt)
        else:
            h1, xext, cls, rank, counts = _layer1_mixer(
                h, sinks[j].astype(F32), gmix, w_in_c[j].astype(MXU), b_in_c[j].reshape(1, C_IN),
                w_out_c[j].astype(MXU), b_out_c[j].reshape(1, D_MODEL), gffn, wrt)
        h = _ffn_and_ple(
            h1, xext, cls, rank, counts, p, i, wg, wu, wd,
            norm_ple[i].reshape(1, D_MODEL), w_ple_gate[i].astype(MXU), w_ple_proj[i].astype(MXU),
            gfin, final=(i == depth - 1))
    return h
```

```python
import functools

import jax
import jax.numpy as jnp
from jax import lax
from jax.experimental import pallas as pl
from jax.experimental.pallas import tpu as pltpu

F32 = jnp.float32
BF16 = jnp.bfloat16
MXU = jnp.bfloat16
I32 = jnp.int32
U32 = jnp.uint32

D_MODEL = 1024
EPS = 1e-6
A_WIDTH = 512
A_HEADS = 4
A_DK = 128
A_CHUNK = 32
AB_IN = 3584
HEAD_DIM = 64
Q_HEADS = 16
KV_HEADS = 2
GROUP = 8
WINDOW = 128
C_IN = 1280
N_EXPERTS = 16
N_GROUPS = 4
PER_GROUP = 4
D_EXPERT = 512
PLE_DIM = 256

PAIRS = ((0, 1), (0, 2), (0, 3), (1, 2), (1, 3), (2, 3))
N_CLASS = N_GROUPS * len(PAIRS)
CLASS_PAD = 32
X_WORDS = D_MODEL // 2
ROW_WORDS = X_WORDS + 128
ROW_TILES = ROW_WORDS // 128

L0_BLOCK = 256
L1_BLOCK = 512
MOE_TILE = 256
MOVE_BLOCK = 512
VMEM_LIMIT = 56 * 1024 * 1024

NT_DIMS = (((1,), (1,)), ((), ()))
TN_DIMS = (((0,), (0,)), ((), ()))


def _rms(x, g):
    ms = jnp.mean(x * x, axis=-1, keepdims=True)
    return x * lax.rsqrt(ms + EPS) * g


def _sigmoid(x):
    return 1.0 / (1.0 + jnp.exp(-x))


def _route_tail(h1, first, gffn_ref, wr_ref, xext_ref, cls_ref, rank_ref, cnt_ref, run_scr,
                pack_scr):
    tb = h1.shape[0]

    @pl.when(first)
    def _():
        run_scr[...] = jnp.zeros_like(run_scr)

    xn = _rms(h1, gffn_ref[...])
    x_hi = xn.astype(BF16)
    x_lo = (xn - x_hi.astype(F32)).astype(BF16)
    prod = (jnp.dot(x_hi, wr_ref[...], preferred_element_type=F32)
            + jnp.dot(x_lo, wr_ref[...], preferred_element_type=F32)).T
    logits = prod[0:N_EXPERTS, :] + prod[N_EXPERTS:2 * N_EXPERTS, :]
    mx = jnp.max(logits, axis=0, keepdims=True)
    ex = jnp.exp(logits - mx)
    sc = ex / jnp.sum(ex, axis=0, keepdims=True)
    rows = [sc[i:i + 1, :] for i in range(N_EXPERTS)]

    gscore = []
    for g in range(N_GROUPS):
        v = rows[PER_GROUP * g:PER_GROUP * (g + 1)]
        best = v[0] + v[1]
        for (i, j) in PAIRS[1:]:
            best = jnp.maximum(best, v[i] + v[j])
        gscore.append(best)
    gmax = jnp.maximum(jnp.maximum(gscore[0], gscore[1]), jnp.maximum(gscore[2], gscore[3]))
    sel = jnp.where(gscore[0] >= gmax, 0, jnp.where(gscore[1] >= gmax, 1,
                                                    jnp.where(gscore[2] >= gmax, 2, 3))).astype(I32)
    v = [jnp.where(sel == 0, rows[i], jnp.where(sel == 1, rows[4 + i],
                                                jnp.where(sel == 2, rows[8 + i], rows[12 + i])))
         for i in range(PER_GROUP)]
    chosen = []
    for i in range(PER_GROUP):
        r = jnp.zeros_like(sel)
        for j in range(PER_GROUP):
            if j == i:
                continue
            ahead = (v[j] >= v[i]) if j < i else (v[j] > v[i])
            r = r + jnp.where(ahead, 1, 0).astype(I32)
        chosen.append(r < 2)
    vsum = jnp.zeros_like(v[0])
    code = jnp.zeros_like(sel)
    for i in range(PER_GROUP):
        vsum = vsum + jnp.where(chosen[i], v[i], 0.0)
        code = code + jnp.where(chosen[i], 1 << i, 0).astype(I32)
    wts = [jnp.where(chosen[i], v[i] / vsum, 0.0) for i in range(PER_GROUP)]
    pair = jnp.where(code == 3, 0, jnp.where(code == 5, 1, jnp.where(code == 9, 2,
                     jnp.where(code == 6, 3, jnp.where(code == 10, 4, 5))))).astype(I32)
    cls = sel * len(PAIRS) + pair

    erow = lax.broadcasted_iota(I32, (128, tb), 0)
    comb_t = jnp.zeros((128, tb), F32)
    for i in range(PER_GROUP):
        comb_t = comb_t + jnp.where(erow == sel * PER_GROUP + i, wts[i], 0.0)
    comb = comb_t.T

    comb_hi = comb.astype(BF16).astype(F32)
    halves = [(xn[:, j * 128:(j + 1) * 128], xn[:, X_WORDS + j * 128:X_WORDS + (j + 1) * 128])
              for j in range(X_WORDS // 128)] + [(comb_hi, comb - comb_hi)]
    for j, (first_half, second_half) in enumerate(halves):
        pack_scr[j, pl.ds(0, tb, stride=2), :] = first_half
        pack_scr[j, pl.ds(1, tb, stride=2), :] = second_half
        xext_ref[:, j * 128:(j + 1) * 128] = pltpu.bitcast(pack_scr[j].astype(BF16), U32)

    crow = lax.broadcasted_iota(I32, (CLASS_PAD, tb), 0)
    onehot = jnp.where(crow == cls, 1.0, 0.0)
    ii = lax.broadcasted_iota(I32, (tb, tb), 0)
    jj = lax.broadcasted_iota(I32, (tb, tb), 1)
    upper = jnp.where(ii < jj, 1.0, 0.0).astype(MXU)
    prefix = jnp.dot(onehot.astype(MXU), upper, preferred_element_type=F32)
    run = run_scr[...]
    rank = jnp.sum(onehot * (prefix + run[:, 0:1]), axis=0, keepdims=True)
    run_new = run + jnp.sum(onehot, axis=1, keepdims=True)
    run_scr[...] = run_new
    cnt_ref[...] = run_new
    cls_ref[0] = cls
    rank_ref[0] = rank.astype(I32)


def _l0_kernel(h_ref, gmix_ref, win_ref, lb_ref, onorm_ref, convw_ref, wout_ref, gffn_ref, wr_ref,
               h1_ref, xext_ref, cls_ref, rank_ref, cnt_ref,
               z_scr, cat_scr, st_scr, carry_scr, run_scr, pack_scr):
    b = pl.program_id(0)
    t = pl.program_id(1)
    tb = h_ref.shape[1]

    @pl.when(t == 0)
    def _():
        st_scr[...] = jnp.zeros_like(st_scr)
        carry_scr[...] = jnp.zeros_like(carry_scr)

    h = h_ref[0]
    u = _rms(h, gmix_ref[...]).astype(MXU)
    z_scr[...] = jnp.dot(u, win_ref[...], preferred_element_type=F32)

    gate_b = z_scr[:, 4 * A_WIDTH:5 * A_WIDTH]
    cb = z_scr[:, 5 * A_WIDTH:6 * A_WIDTH] * z_scr[:, 6 * A_WIDTH:7 * A_WIDTH]
    row = lax.broadcasted_iota(I32, cb.shape, 0)
    prev1 = carry_scr[7:8, :]
    prev2 = carry_scr[6:7, :]
    m1 = jnp.where(row == 0, prev1, pltpu.roll(cb, 1, axis=0))
    m2 = jnp.where(row == 0, prev2, jnp.where(row == 1, prev1, pltpu.roll(cb, 2, axis=0)))
    cw = convw_ref[...]
    cat_scr[:, A_WIDTH:] = gate_b * (cw[0:1, :] * m2 + cw[1:2, :] * m1 + cw[2:3, :] * cb)
    carry_scr[...] = cb[tb - 8:tb, :]

    nc = tb // A_CHUNK
    lb = lb_ref[...]
    onorm = onorm_ref[...]
    zq = z_scr[:, 0:A_WIDTH]
    zf = z_scr[:, A_WIDTH:2 * A_WIDTH]
    zi = z_scr[:, 2 * A_WIDTH:3 * A_WIDTH]
    zg = z_scr[:, 3 * A_WIDTH:4 * A_WIDTH]
    f = lb + (1.0 - lb) * _sigmoid(zf)
    logf = jnp.log(f)
    ri = lax.broadcasted_iota(I32, (tb, tb), 0)
    rj = lax.broadcasted_iota(I32, (tb, tb), 1)
    same_chunk = (ri // A_CHUNK) == (rj // A_CHUNK)
    causal = same_chunk & (ri >= rj)
    tri = jnp.where(causal, 1.0, 0.0).astype(BF16)
    p1 = logf.astype(BF16)
    r1 = logf - p1.astype(F32)
    p2 = r1.astype(BF16)
    p3 = (r1 - p2.astype(F32)).astype(BF16)
    pieces = jnp.dot(tri, jnp.concatenate([p1, p2, p3], axis=1), preferred_element_type=F32)
    bcum = (pieces[:, 0:A_WIDTH] + pieces[:, A_WIDTH:2 * A_WIDTH]) + pieces[:, 2 * A_WIDTH:]
    blast3 = bcum.reshape(nc, A_CHUNK, A_WIDTH)[:, A_CHUNK - 1:A_CHUNK, :]
    blast = jnp.broadcast_to(blast3, (nc, A_CHUNK, A_WIDTH)).reshape(tb, A_WIDTH)
    k = 1.0 - f
    q_dec = ((zq * _sigmoid(zq)) * jnp.exp(bcum)).astype(MXU)
    k_inv = (k * jnp.exp(-bcum)).astype(MXU)
    k_tail = (k * jnp.exp(blast - bcum)).astype(MXU)
    decay = jnp.exp(blast3)
    gate = _sigmoid(zg)
    vals = zi.astype(MXU)
    in_block = (lax.broadcasted_iota(I32, (tb, nc * A_DK), 0) // A_CHUNK
                == lax.broadcasted_iota(I32, (tb, nc * A_DK), 1) // A_DK)
    zero = jnp.zeros((), MXU)
    for hd in range(A_HEADS):
        sl = slice(hd * A_DK, (hd + 1) * A_DK)
        qd = q_dec[:, sl]
        vv = vals[:, sl]
        scores = lax.dot_general(qd, k_inv[:, sl], NT_DIMS, preferred_element_type=F32)
        scores = jnp.where(causal, scores, 0.0).astype(MXU)
        o = jnp.dot(scores, vv, preferred_element_type=F32)
        kt_blocks = jnp.where(in_block, jnp.tile(k_tail[:, sl], (1, nc)), zero)
        kv_all = lax.dot_general(vv, kt_blocks, TN_DIMS, preferred_element_type=F32)
        st = st_scr[hd]
        states = []
        for c in range(nc):
            cs = slice(c * A_DK, (c + 1) * A_DK)
            states.append(st.astype(MXU))
            st = st * decay[c, :, sl] + kv_all[:, cs]
        st_scr[hd] = st
        qd_blocks = jnp.where(in_block, jnp.tile(qd, (1, nc)), zero)
        o = o + lax.dot_general(qd_blocks, jnp.concatenate(states, axis=1), NT_DIMS,
                                preferred_element_type=F32)
        cat_scr[:, sl] = _rms(o, onorm) * gate[:, sl]

    h1 = h + jnp.dot(cat_scr[...].astype(MXU), wout_ref[...], preferred_element_type=F32)
    h1_ref[0] = h1
    _route_tail(h1, (b == 0) & (t == 0), gffn_ref, wr_ref, xext_ref, cls_ref, rank_ref, cnt_ref,
                run_scr, pack_scr)


def _l1_kernel(sinks_ref, h_ref, gmix_ref, wqt_ref, bqt_ref, wk_ref, bk_ref, wvt_ref, bvt_ref,
               wout_ref, bout_ref, gffn_ref, wr_ref,
               h1_ref, xext_ref, cls_ref, rank_ref, cnt_ref,
               qt_scr, k_scr, vt_scr, att_scr, run_scr, pack_scr):
    b = pl.program_id(0)
    t = pl.program_id(1)
    tb = h_ref.shape[1]

    @pl.when(t == 0)
    def _():
        k_scr[:, 0:WINDOW, :] = jnp.zeros((KV_HEADS, WINDOW, HEAD_DIM), k_scr.dtype)
        vt_scr[:, 0:WINDOW] = jnp.zeros((KV_HEADS * HEAD_DIM, WINDOW), vt_scr.dtype)

    h = h_ref[0]
    u = _rms(h, gmix_ref[...]).astype(MXU)
    scale = HEAD_DIM ** -0.5
    qt_scr[...] = ((lax.dot_general(wqt_ref[...], u, NT_DIMS, preferred_element_type=F32)
                    + bqt_ref[...]) * scale).astype(qt_scr.dtype)
    kz = (jnp.dot(u, wk_ref[...], preferred_element_type=F32) + bk_ref[...]).astype(k_scr.dtype)
    for kk in range(KV_HEADS):
        k_scr[kk, WINDOW:, :] = kz[:, kk * HEAD_DIM:(kk + 1) * HEAD_DIM]
    vt_scr[:, WINDOW:] = (lax.dot_general(wvt_ref[...], u, NT_DIMS, preferred_element_type=F32)
                          + bvt_ref[...]).astype(vt_scr.dtype)

    kj = lax.broadcasted_iota(I32, (2 * WINDOW, WINDOW), 0)
    qi = lax.broadcasted_iota(I32, (2 * WINDOW, WINDOW), 1)
    band = (kj > qi) & (kj <= qi + WINDOW)
    first_key = jnp.where(t > 0, 0, WINDOW)

    def all_heads(keep):
        return jnp.tile(jnp.where(keep, 1.0, 0.0), (1, GROUP)) > 0.5

    mask_first = all_heads(band & (kj >= first_key))
    mask_rest = all_heads(band)
    for n in range(tb // WINDOW):
        rs = slice(n * WINDOW, (n + 1) * WINDOW)
        win = slice(n * WINDOW, (n + 2) * WINDOW)
        mask = mask_first if n == 0 else mask_rest
        for kk in range(KV_HEADS):
            heads = range(kk * GROUP, (kk + 1) * GROUP)
            keys = k_scr[kk, win, :]
            vals_t = vt_scr[kk * HEAD_DIM:(kk + 1) * HEAD_DIM, win]
            q_t = jnp.concatenate(
                [qt_scr[hd * HEAD_DIM:(hd + 1) * HEAD_DIM, rs] for hd in heads], axis=1)
            sink = jnp.concatenate(
                [jnp.full((1, WINDOW), sinks_ref[hd], F32) for hd in heads], axis=1)
            s = jnp.dot(keys, q_t, preferred_element_type=F32)
            s = jnp.where(mask, s, -jnp.inf)
            m = jnp.maximum(jnp.max(s, axis=0, keepdims=True), sink)
            e = jnp.exp(s - m)
            denom = jnp.sum(e, axis=0, keepdims=True) + jnp.exp(sink - m)
            o_t = jnp.dot(vals_t, e.astype(MXU), preferred_element_type=F32) / denom
            for g2 in range(GROUP // 2):
                pair = jnp.concatenate([o_t[:, (2 * g2) * WINDOW:(2 * g2 + 1) * WINDOW],
                                        o_t[:, (2 * g2 + 1) * WINDOW:(2 * g2 + 2) * WINDOW]], axis=0)
                h0 = (kk * GROUP + 2 * g2) * HEAD_DIM
                att_scr[rs, h0:h0 + 2 * HEAD_DIM] = pair.T
    k_scr[:, 0:WINDOW, :] = k_scr[:, tb:tb + WINDOW, :]
    vt_scr[:, 0:WINDOW] = vt_scr[:, tb:tb + WINDOW]

    h1 = (h + jnp.dot(att_scr[...].astype(MXU), wout_ref[...], preferred_element_type=F32)
          + bout_ref[...])
    h1_ref[0] = h1
    _route_tail(h1, (b == 0) & (t == 0), gffn_ref, wr_ref, xext_ref, cls_ref, rank_ref, cnt_ref,
                run_scr, pack_scr)


def _row_copy(src, src_row, dst, dst_row, sem):
    return pltpu.make_async_copy(src.at[pl.ds(src_row, 1)], dst.at[pl.ds(dst_row, 1)], sem)


ISSUE_UNROLL = 8
DMA_THREADS = 2


def _scatter_kernel(pos_ref, pad0_ref, pad1_ref, nused_ref, x_ref, out_ref, zero_scr, sem,
                    pad_sem):
    rows = x_ref.shape[0]
    tm = zero_scr.shape[0]
    i = pl.program_id(0)
    base = i * rows

    @pl.when(i == 0)
    def _():
        zero_scr[...] = jnp.zeros_like(zero_scr)

        def zero_tile(tile, carry):
            dst = out_ref.at[pl.ds(pl.multiple_of(tile * tm, tm), tm)]
            pltpu.make_async_copy(zero_scr, dst, pad_sem).start()
            pltpu.make_async_copy(zero_scr, dst, pad_sem).wait()
            return carry

        lax.fori_loop(nused_ref[0], out_ref.shape[0] // tm, zero_tile, 0)

        def pad_class(c, carry):
            def start(r, carry):
                _row_copy(zero_scr, 0, out_ref, r, pad_sem).start()
                return carry

            def finish(r, carry):
                _row_copy(zero_scr, 0, out_ref, r, pad_sem).wait()
                return carry

            lax.fori_loop(pad0_ref[c], pad1_ref[c], start, 0)
            return lax.fori_loop(pad0_ref[c], pad1_ref[c], finish, carry)

        lax.fori_loop(0, N_CLASS, pad_class, 0)

    def issue(g, carry):
        r0 = pl.multiple_of(g * ISSUE_UNROLL, ISSUE_UNROLL)
        group = x_ref.at[pl.ds(r0, ISSUE_UNROLL)]
        for k in range(ISSUE_UNROLL):
            _row_copy(group, k, out_ref, pos_ref[base + r0 + k],
                      sem).start(priority=k % DMA_THREADS)
        return carry

    lax.fori_loop(0, rows // ISSUE_UNROLL, issue, 0)
    pltpu.make_async_copy(x_ref, out_ref.at[pl.ds(0, rows)], sem).wait()


def _moe_kernel(e1_ref, e2_ref, nused_ref,
                x_ref, wg1_ref, wu1_ref, wd1_ref, wg2_ref, wu2_ref, wd2_ref, y_ref, unpack_scr):
    i = pl.program_id(0)
    tm = x_ref.shape[0]
    nused = nused_ref[0]

    @pl.when(i < nused)
    def _():
        for j in range(ROW_TILES):
            unpack_scr[j] = pltpu.bitcast(x_ref[:, j * 128:(j + 1) * 128], BF16).astype(F32)
        xa = [unpack_scr[j, pl.ds(0, tm, stride=2), :] for j in range(ROW_TILES)]
        xb = [unpack_scr[j, pl.ds(1, tm, stride=2), :] for j in range(ROW_TILES)]
        x = jnp.concatenate(xa[:-1] + xb[:-1], axis=1).astype(MXU)
        comb = xa[-1] + xb[-1]
        lane = lax.broadcasted_iota(I32, comb.shape, 1)

        def expert(e, wg_ref, wu_ref, wd_ref):
            cw = jnp.sum(jnp.where(lane == e, comb, 0.0), axis=1, keepdims=True)
            gate = jnp.dot(x, wg_ref[...], preferred_element_type=F32)
            up = jnp.dot(x, wu_ref[...], preferred_element_type=F32)
            hidden = (gate * _sigmoid(gate)) * up
            return cw * jnp.dot(hidden.astype(MXU), wd_ref[...], preferred_element_type=F32)

        y_ref[...] = (expert(e1_ref[i], wg1_ref, wu1_ref, wd1_ref)
                      + expert(e2_ref[i], wg2_ref, wu2_ref, wd2_ref))

    @pl.when(i >= nused)
    def _():
        y_ref[...] = jnp.zeros_like(y_ref)


def _ple_kernel(pos_ref, h1_ref, p_ref, y_ref, gple_ref, wgate_ref, wproj_ref, gfin_ref,
                out_ref, ybuf, sem, *, final):
    rows = h1_ref.shape[0]
    i = pl.program_id(0)
    nsteps = pl.num_programs(0)
    slot = i % 2

    def gather(step, buf):
        def issue(g, carry):
            r0 = pl.multiple_of(g * ISSUE_UNROLL, ISSUE_UNROLL)
            group = ybuf.at[buf, pl.ds(r0, ISSUE_UNROLL)]
            for k in range(ISSUE_UNROLL):
                _row_copy(y_ref, pos_ref[step * rows + r0 + k], group, k,
                          sem.at[buf]).start(priority=k % DMA_THREADS)
            return carry

        lax.fori_loop(0, rows // ISSUE_UNROLL, issue, 0)

    def wait(buf):
        pltpu.make_async_copy(y_ref.at[pl.ds(0, rows)], ybuf.at[buf], sem.at[buf]).wait()

    @pl.when(i == 0)
    def _():
        gather(0, 0)

    wait(slot)
    h2 = h1_ref[...] + ybuf[slot]
    nxt = jnp.minimum(i + 1, nsteps - 1)
    for r in range(rows):
        group = ybuf.at[1 - slot, pl.ds((r // 8) * 8, 8)]
        _row_copy(y_ref, pos_ref[nxt * rows + r], group, r % 8,
                  sem.at[1 - slot]).start(priority=r % DMA_THREADS)

    proj = jnp.dot(p_ref[...].astype(MXU), wproj_ref[...], preferred_element_type=F32)
    gate = _sigmoid(jnp.dot(_rms(h2, gple_ref[...]).astype(MXU), wgate_ref[...],
                            preferred_element_type=F32))
    h3 = h2 + gate * proj
    if final:
        h3 = _rms(h3, gfin_ref[...])
    out_ref[...] = h3

    @pl.when(i == nsteps - 1)
    def _():
        wait(1 - slot)


def _const_spec(shape):
    nd = len(shape)
    return pl.BlockSpec(shape, lambda *_: (0,) * nd)


def _route_out(bsz, seq, tb):
    nt = seq // tb
    n = bsz * seq
    shapes = [
        jax.ShapeDtypeStruct((bsz, seq, D_MODEL), F32),
        jax.ShapeDtypeStruct((n, ROW_WORDS), U32),
        jax.ShapeDtypeStruct((bsz * nt, 1, tb), I32),
        jax.ShapeDtypeStruct((bsz * nt, 1, tb), I32),
        jax.ShapeDtypeStruct((CLASS_PAD, 128), F32),
    ]
    specs = [
        pl.BlockSpec((1, tb, D_MODEL), lambda b, t, *_: (b, t, 0)),
        pl.BlockSpec((tb, ROW_WORDS), lambda b, t, *_: (b * nt + t, 0)),
        pl.BlockSpec((1, 1, tb), lambda b, t, *_: (b * nt + t, 0, 0)),
        pl.BlockSpec((1, 1, tb), lambda b, t, *_: (b * nt + t, 0, 0)),
        pl.BlockSpec((CLASS_PAD, 128), lambda b, t, *_: (0, 0)),
    ]
    return shapes, specs


def _params():
    return pltpu.CompilerParams(dimension_semantics=("arbitrary", "arbitrary"),
                                vmem_limit_bytes=VMEM_LIMIT)


def _layer0_mixer(h, gmix, win, lb, onorm, convw, wout, gffn, wrt):
    bsz, seq, _ = h.shape
    tb = L0_BLOCK
    assert seq % tb == 0
    shapes, ospecs = _route_out(bsz, seq, tb)
    return pl.pallas_call(
        _l0_kernel,
        grid=(bsz, seq // tb),
        in_specs=[
            pl.BlockSpec((1, tb, D_MODEL), lambda b, t: (b, t, 0)),
            _const_spec((1, D_MODEL)),
            _const_spec((D_MODEL, AB_IN)),
            _const_spec((1, A_WIDTH)),
            _const_spec((1, A_DK)),
            _const_spec((3, A_WIDTH)),
            _const_spec((D_MODEL, D_MODEL)),
            _const_spec((1, D_MODEL)),
            _const_spec((D_MODEL, 128)),
        ],
        out_specs=ospecs,
        out_shape=shapes,
        scratch_shapes=[
            pltpu.VMEM((tb, AB_IN), F32),
            pltpu.VMEM((tb, D_MODEL), F32),
            pltpu.VMEM((A_HEADS, A_DK, A_DK), F32),
            pltpu.VMEM((8, A_WIDTH), F32),
            pltpu.VMEM((CLASS_PAD, 128), F32),
            pltpu.VMEM((ROW_TILES, 2 * tb, 128), F32),
        ],
        compiler_params=_params(),
        name="layer0_mixer",
    )(h, gmix, win, lb, onorm, convw, wout, gffn, wrt)


def _layer1_mixer(h, sinks, gmix, win, bin_, wout, bout, gffn, wrt):
    bsz, seq, _ = h.shape
    tb = L1_BLOCK
    assert seq % tb == 0
    shapes, ospecs = _route_out(bsz, seq, tb)
    qw = Q_HEADS * HEAD_DIM
    vw = KV_HEADS * HEAD_DIM
    wqt, wk, wvt = win[:, :qw].T, win[:, qw:qw + vw], win[:, qw + vw:].T
    bqt, bk, bvt = bin_[:, :qw].reshape(qw, 1), bin_[:, qw:qw + vw], bin_[:, qw + vw:].reshape(vw, 1)
    grid_spec = pltpu.PrefetchScalarGridSpec(
        num_scalar_prefetch=1,
        grid=(bsz, seq // tb),
        in_specs=[
            pl.BlockSpec((1, tb, D_MODEL), lambda b, t, *_: (b, t, 0)),
            _const_spec((1, D_MODEL)),
            _const_spec((qw, D_MODEL)),
            _const_spec((qw, 1)),
            _const_spec((D_MODEL, vw)),
            _const_spec((1, vw)),
            _const_spec((vw, D_MODEL)),
            _const_spec((vw, 1)),
            _const_spec((D_MODEL, D_MODEL)),
            _const_spec((1, D_MODEL)),
            _const_spec((1, D_MODEL)),
            _const_spec((D_MODEL, 128)),
        ],
        out_specs=ospecs,
        scratch_shapes=[
            pltpu.VMEM((qw, tb), MXU),
            pltpu.VMEM((KV_HEADS, tb + WINDOW, HEAD_DIM), MXU),
            pltpu.VMEM((vw, tb + WINDOW), MXU),
            pltpu.VMEM((tb, D_MODEL), F32),
            pltpu.VMEM((CLASS_PAD, 128), F32),
            pltpu.VMEM((ROW_TILES, 2 * tb, 128), F32),
        ],
    )
    return pl.pallas_call(
        _l1_kernel, grid_spec=grid_spec, out_shape=shapes, compiler_params=_params(),
        name="layer1_mixer",
    )(sinks, h, gmix, wqt, bqt, wk, bk, wvt, bvt, wout, bout, gffn, wrt)


def _dispatch(plan, xext):
    pos, _, _, nused, pad0, pad1, n_rows = plan
    n = xext.shape[0]
    blk = MOVE_BLOCK
    assert n % blk == 0
    grid_spec = pltpu.PrefetchScalarGridSpec(
        num_scalar_prefetch=4,
        grid=(n // blk,),
        in_specs=[pl.BlockSpec((blk, ROW_WORDS), lambda i, *_: (i, 0))],
        out_specs=pl.BlockSpec(memory_space=pl.ANY),
        scratch_shapes=[pltpu.VMEM((MOE_TILE, ROW_WORDS), U32), pltpu.SemaphoreType.DMA(()),
                        pltpu.SemaphoreType.DMA(())],
    )
    return pl.pallas_call(
        _scatter_kernel, grid_spec=grid_spec,
        out_shape=jax.ShapeDtypeStruct((n_rows, ROW_WORDS), U32),
        compiler_params=pltpu.CompilerParams(dimension_semantics=("arbitrary",),
                                             vmem_limit_bytes=VMEM_LIMIT),
        name="moe_dispatch",
    )(pos, pad0, pad1, nused, xext)


def _moe(plan, xs, layer, wg, wu, wd):
    _, e1, e2, nused, _, _, n_rows = plan
    tm = MOE_TILE
    n_tiles = n_rows // tm

    def wspec(shape, which):
        return pl.BlockSpec((None, None) + shape,
                            lambda i, e1, e2, *_: (layer, (e1, e2)[which][i], 0, 0))

    grid_spec = pltpu.PrefetchScalarGridSpec(
        num_scalar_prefetch=3,
        grid=(n_tiles,),
        in_specs=[
            pl.BlockSpec((tm, ROW_WORDS), lambda i, *_: (i, 0)),
            wspec((D_MODEL, D_EXPERT), 0), wspec((D_MODEL, D_EXPERT), 0), wspec((D_EXPERT, D_MODEL), 0),
            wspec((D_MODEL, D_EXPERT), 1), wspec((D_MODEL, D_EXPERT), 1), wspec((D_EXPERT, D_MODEL), 1),
        ],
        out_specs=pl.BlockSpec((tm, D_MODEL), lambda i, *_: (i, 0)),
        scratch_shapes=[pltpu.VMEM((ROW_TILES, 2 * tm, 128), F32)],
    )
    return pl.pallas_call(
        _moe_kernel, grid_spec=grid_spec,
        out_shape=jax.ShapeDtypeStruct((n_rows, D_MODEL), F32),
        compiler_params=pltpu.CompilerParams(dimension_semantics=("arbitrary",),
                                             vmem_limit_bytes=VMEM_LIMIT),
        name="moe_experts",
    )(e1, e2, nused, xs, wg, wu, wd, wg, wu, wd)


def _combine_ple(pos, h1, p, layer, ys, gple, wgate, wproj, gfin, final):
    n = h1.shape[0]
    blk = MOVE_BLOCK
    assert n % blk == 0
    grid_spec = pltpu.PrefetchScalarGridSpec(
        num_scalar_prefetch=1,
        grid=(n // blk,),
        in_specs=[
            pl.BlockSpec((blk, D_MODEL), lambda i, *_: (i, 0)),
            pl.BlockSpec((None, blk, PLE_DIM), lambda i, *_: (layer, i, 0)),
            pl.BlockSpec(memory_space=pl.ANY),
            _const_spec((1, D_MODEL)),
            _const_spec((D_MODEL, D_MODEL)),
            _const_spec((PLE_DIM, D_MODEL)),
            _const_spec((1, D_MODEL)),
        ],
        out_specs=pl.BlockSpec((blk, D_MODEL), lambda i, *_: (i, 0)),
        scratch_shapes=[pltpu.VMEM((2, blk, D_MODEL), F32), pltpu.SemaphoreType.DMA((2,))],
    )
    return pl.pallas_call(
        functools.partial(_ple_kernel, final=final), grid_spec=grid_spec,
        out_shape=jax.ShapeDtypeStruct((n, D_MODEL), F32),
        compiler_params=pltpu.CompilerParams(dimension_semantics=("arbitrary",),
                                             vmem_limit_bytes=VMEM_LIMIT),
        name="combine_ple",
    )(pos, h1, p, ys, gple, wgate, wproj, gfin)


def _routing_plan(cls, rank, counts, n_tokens):
    tm = MOE_TILE
    n_tiles = n_tokens // tm + N_CLASS
    cnt = counts[:N_CLASS, 0].astype(I32)
    tiles_per_class = (cnt + tm - 1) // tm
    tile_end = jnp.cumsum(tiles_per_class)
    class_row0 = (tile_end - tiles_per_class) * tm
    pos = jnp.take(class_row0, cls.reshape(-1)) + rank.reshape(-1)
    nused = tile_end[-1]
    tidx = jnp.minimum(jnp.arange(n_tiles, dtype=I32), nused - 1)
    tcls = jnp.sum((tidx[:, None] >= tile_end[None, :]).astype(I32), axis=1)
    tcls = jnp.minimum(tcls, N_CLASS - 1)
    lo = jnp.array([a for a, _ in PAIRS], I32)
    hi = jnp.array([b for _, b in PAIRS], I32)
    grp = tcls // len(PAIRS)
    e1 = grp * PER_GROUP + jnp.take(lo, tcls % len(PAIRS))
    e2 = grp * PER_GROUP + jnp.take(hi, tcls % len(PAIRS))
    pad0 = class_row0 + cnt
    pad1 = tile_end * tm
    return (pos.astype(I32), e1.astype(I32), e2.astype(I32), nused.reshape(1).astype(I32),
            pad0.astype(I32), pad1.astype(I32), n_tiles * tm)


def _ffn_and_ple(h1, xext, cls, rank, counts, p, layer, wg, wu, wd, gple, wgate, wproj, gfin,
                 final):
    bsz, seq, _ = h1.shape
    n = bsz * seq
    plan = _routing_plan(cls, rank, counts, n)
    ys = _moe(plan, _dispatch(plan, xext), layer, wg, wu, wd)
    out = _combine_ple(plan[0], h1.reshape(n, D_MODEL), p.reshape(p.shape[0], n, PLE_DIM), layer,
                       ys, gple, wgate, wproj, gfin, final)
    return out.reshape(bsz, seq, D_MODEL)


def kernel(x, p, norm_mix, norm_ffn, norm_ple, norm_final, w_in_ab, hgrn_lb_logits, hgrn_out_norm, conv_w, w_out_ab, w_in_c, b_in_c, sinks, w_out_c, b_out_c, w_router, w_gate_e, w_up_e, w_down_e, w_ple_gate, w_ple_proj):
    depth = p.shape[0]
    lower_bounds = jnp.cumsum(jax.nn.softmax(hgrn_lb_logits.astype(F32), axis=0), axis=0)
    wr32 = w_router.astype(F32)
    wr_hi = wr32.astype(BF16)
    wr_lo = (wr32 - wr_hi.astype(F32)).astype(BF16)
    wrt = jnp.concatenate(
        [wr_hi, wr_lo, jnp.zeros((D_MODEL, 128 - 2 * N_EXPERTS), BF16)], axis=1)
    gfin = norm_final.reshape(1, D_MODEL)
    wg, wu, wd = w_gate_e.astype(MXU), w_up_e.astype(MXU), w_down_e.astype(MXU)
    h = x
    for i in range(depth):
        j = i // 2
        gmix = norm_mix[i].reshape(1, D_MODEL)
        gffn = norm_ffn[i].reshape(1, D_MODEL)
        if i % 2 == 0:
            h1, xext, cls, rank, counts = _layer0_mixer(
                h, gmix, w_in_ab[j].astype(MXU), lower_bounds[i].reshape(1, A_WIDTH),
                hgrn_out_norm[j].reshape(1, A_DK), conv_w[j], w_out_ab[j].astype(MXU), gffn, wrt)
        else:
            h1, xext, cls, rank, counts = _layer1_mixer(
                h, sinks[j].astype(F32), gmix, w_in_c[j].astype(MXU), b_in_c[j].reshape(1, C_IN),
                w_out_c[j].astype(MXU), b_out_c[j].reshape(1, D_MODEL), gffn, wrt)
        h = _ffn_and_ple(
            h1, xext, cls, rank, counts, p, i, wg, wu, wd,
            norm_ple[i].reshape(1, D_MODEL), w_ple_gate[i].astype(MXU), w_ple_proj[i].astype(MXU),
            gfin, final=(i == depth - 1))
    return h
```

```python
import functools

import jax
import jax.numpy as jnp
from jax import lax
from jax.experimental import pallas as pl
from jax.experimental.pallas import tpu as pltpu

F32 = jnp.float32
BF16 = jnp.bfloat16
MXU = jnp.bfloat16
I32 = jnp.int32

D_MODEL = 1024
EPS = 1e-6
A_WIDTH = 512
A_HEADS = 4
A_DK = 128
A_CHUNK = 32
AB_IN = 3584
HEAD_DIM = 64
Q_HEADS = 16
KV_HEADS = 2
GROUP = 8
WINDOW = 128
C_IN = 1280
N_EXPERTS = 16
N_GROUPS = 4
PER_GROUP = 4
D_EXPERT = 512
PLE_DIM = 256

PAIRS = ((0, 1), (0, 2), (0, 3), (1, 2), (1, 3), (2, 3))
N_CLASS = N_GROUPS * len(PAIRS)
CLASS_PAD = 32
ROW_WORDS = D_MODEL + 128

L0_BLOCK = 256
L1_BLOCK = 512
MOE_TILE = 256
MOVE_BLOCK = 512
VMEM_LIMIT = 56 * 1024 * 1024

NT_DIMS = (((1,), (1,)), ((), ()))
TN_DIMS = (((0,), (0,)), ((), ()))


def _rms(x, g):
    ms = jnp.mean(x * x, axis=-1, keepdims=True)
    return x * lax.rsqrt(ms + EPS) * g


def _sigmoid(x):
    return 1.0 / (1.0 + jnp.exp(-x))


def _route_tail(h1, first, gffn_ref, wr_ref, xext_ref, cls_ref, rank_ref, cnt_ref, run_scr):
    tb = h1.shape[0]

    @pl.when(first)
    def _():
        run_scr[...] = jnp.zeros_like(run_scr)

    xn = _rms(h1, gffn_ref[...])
    x_hi = xn.astype(BF16)
    x_lo = (xn - x_hi.astype(F32)).astype(BF16)
    prod = (jnp.dot(x_hi, wr_ref[...], preferred_element_type=F32)
            + jnp.dot(x_lo, wr_ref[...], preferred_element_type=F32)).T
    logits = prod[0:N_EXPERTS, :] + prod[N_EXPERTS:2 * N_EXPERTS, :]
    mx = jnp.max(logits, axis=0, keepdims=True)
    ex = jnp.exp(logits - mx)
    sc = ex / jnp.sum(ex, axis=0, keepdims=True)
    rows = [sc[i:i + 1, :] for i in range(N_EXPERTS)]

    gscore = []
    for g in range(N_GROUPS):
        v = rows[PER_GROUP * g:PER_GROUP * (g + 1)]
        best = v[0] + v[1]
        for (i, j) in PAIRS[1:]:
            best = jnp.maximum(best, v[i] + v[j])
        gscore.append(best)
    gmax = jnp.maximum(jnp.maximum(gscore[0], gscore[1]), jnp.maximum(gscore[2], gscore[3]))
    sel = jnp.where(gscore[0] >= gmax, 0, jnp.where(gscore[1] >= gmax, 1,
                                                    jnp.where(gscore[2] >= gmax, 2, 3))).astype(I32)
    v = [jnp.where(sel == 0, rows[i], jnp.where(sel == 1, rows[4 + i],
                                                jnp.where(sel == 2, rows[8 + i], rows[12 + i])))
         for i in range(PER_GROUP)]
    chosen = []
    for i in range(PER_GROUP):
        r = jnp.zeros_like(sel)
        for j in range(PER_GROUP):
            if j == i:
                continue
            ahead = (v[j] >= v[i]) if j < i else (v[j] > v[i])
            r = r + jnp.where(ahead, 1, 0).astype(I32)
        chosen.append(r < 2)
    vsum = jnp.zeros_like(v[0])
    code = jnp.zeros_like(sel)
    for i in range(PER_GROUP):
        vsum = vsum + jnp.where(chosen[i], v[i], 0.0)
        code = code + jnp.where(chosen[i], 1 << i, 0).astype(I32)
    wts = [jnp.where(chosen[i], v[i] / vsum, 0.0) for i in range(PER_GROUP)]
    pair = jnp.where(code == 3, 0, jnp.where(code == 5, 1, jnp.where(code == 9, 2,
                     jnp.where(code == 6, 3, jnp.where(code == 10, 4, 5))))).astype(I32)
    cls = sel * len(PAIRS) + pair

    erow = lax.broadcasted_iota(I32, (128, tb), 0)
    comb_t = jnp.zeros((128, tb), F32)
    for i in range(PER_GROUP):
        comb_t = comb_t + jnp.where(erow == sel * PER_GROUP + i, wts[i], 0.0)
    comb = comb_t.T

    xext_ref[:, :D_MODEL] = xn
    xext_ref[:, D_MODEL:] = comb

    crow = lax.broadcasted_iota(I32, (CLASS_PAD, tb), 0)
    onehot = jnp.where(crow == cls, 1.0, 0.0)
    ii = lax.broadcasted_iota(I32, (tb, tb), 0)
    jj = lax.broadcasted_iota(I32, (tb, tb), 1)
    upper = jnp.where(ii < jj, 1.0, 0.0).astype(MXU)
    prefix = jnp.dot(onehot.astype(MXU), upper, preferred_element_type=F32)
    run = run_scr[...]
    rank = jnp.sum(onehot * (prefix + run[:, 0:1]), axis=0, keepdims=True)
    run_new = run + jnp.sum(onehot, axis=1, keepdims=True)
    run_scr[...] = run_new
    cnt_ref[...] = run_new
    cls_ref[0] = cls
    rank_ref[0] = rank.astype(I32)


def _l0_kernel(h_ref, gmix_ref, win_ref, lb_ref, onorm_ref, convw_ref, wout_ref, gffn_ref, wr_ref,
               h1_ref, xext_ref, cls_ref, rank_ref, cnt_ref,
               z_scr, cat_scr, st_scr, carry_scr, run_scr):
    b = pl.program_id(0)
    t = pl.program_id(1)
    tb = h_ref.shape[1]

    @pl.when(t == 0)
    def _():
        st_scr[...] = jnp.zeros_like(st_scr)
        carry_scr[...] = jnp.zeros_like(carry_scr)

    h = h_ref[0]
    u = _rms(h, gmix_ref[...]).astype(MXU)
    z_scr[...] = jnp.dot(u, win_ref[...], preferred_element_type=F32)

    gate_b = z_scr[:, 4 * A_WIDTH:5 * A_WIDTH]
    cb = z_scr[:, 5 * A_WIDTH:6 * A_WIDTH] * z_scr[:, 6 * A_WIDTH:7 * A_WIDTH]
    row = lax.broadcasted_iota(I32, cb.shape, 0)
    prev1 = carry_scr[7:8, :]
    prev2 = carry_scr[6:7, :]
    m1 = jnp.where(row == 0, prev1, pltpu.roll(cb, 1, axis=0))
    m2 = jnp.where(row == 0, prev2, jnp.where(row == 1, prev1, pltpu.roll(cb, 2, axis=0)))
    cw = convw_ref[...]
    cat_scr[:, A_WIDTH:] = gate_b * (cw[0:1, :] * m2 + cw[1:2, :] * m1 + cw[2:3, :] * cb)
    carry_scr[...] = cb[tb - 8:tb, :]

    nc = tb // A_CHUNK
    lb = lb_ref[...]
    onorm = onorm_ref[...]
    zq = z_scr[:, 0:A_WIDTH]
    zf = z_scr[:, A_WIDTH:2 * A_WIDTH]
    zi = z_scr[:, 2 * A_WIDTH:3 * A_WIDTH]
    zg = z_scr[:, 3 * A_WIDTH:4 * A_WIDTH]
    f = lb + (1.0 - lb) * _sigmoid(zf)
    logf = jnp.log(f)
    ri = lax.broadcasted_iota(I32, (tb, tb), 0)
    rj = lax.broadcasted_iota(I32, (tb, tb), 1)
    same_chunk = (ri // A_CHUNK) == (rj // A_CHUNK)
    causal = same_chunk & (ri >= rj)
    tri = jnp.where(causal, 1.0, 0.0).astype(BF16)
    p1 = logf.astype(BF16)
    r1 = logf - p1.astype(F32)
    p2 = r1.astype(BF16)
    p3 = (r1 - p2.astype(F32)).astype(BF16)
    pieces = jnp.dot(tri, jnp.concatenate([p1, p2, p3], axis=1), preferred_element_type=F32)
    bcum = (pieces[:, 0:A_WIDTH] + pieces[:, A_WIDTH:2 * A_WIDTH]) + pieces[:, 2 * A_WIDTH:]
    blast3 = bcum.reshape(nc, A_CHUNK, A_WIDTH)[:, A_CHUNK - 1:A_CHUNK, :]
    blast = jnp.broadcast_to(blast3, (nc, A_CHUNK, A_WIDTH)).reshape(tb, A_WIDTH)
    k = 1.0 - f
    q_dec = ((zq * _sigmoid(zq)) * jnp.exp(bcum)).astype(MXU)
    k_inv = (k * jnp.exp(-bcum)).astype(MXU)
    k_tail = (k * jnp.exp(blast - bcum)).astype(MXU)
    decay = jnp.exp(blast3)
    gate = _sigmoid(zg)
    vals = zi.astype(MXU)
    in_block = (lax.broadcasted_iota(I32, (tb, nc * A_DK), 0) // A_CHUNK
                == lax.broadcasted_iota(I32, (tb, nc * A_DK), 1) // A_DK)
    zero = jnp.zeros((), MXU)
    for hd in range(A_HEADS):
        sl = slice(hd * A_DK, (hd + 1) * A_DK)
        qd = q_dec[:, sl]
        vv = vals[:, sl]
        scores = lax.dot_general(qd, k_inv[:, sl], NT_DIMS, preferred_element_type=F32)
        scores = jnp.where(causal, scores, 0.0).astype(MXU)
        o = jnp.dot(scores, vv, preferred_element_type=F32)
        kt_blocks = jnp.where(in_block, jnp.tile(k_tail[:, sl], (1, nc)), zero)
        kv_all = lax.dot_general(vv, kt_blocks, TN_DIMS, preferred_element_type=F32)
        st = st_scr[hd]
        states = []
        for c in range(nc):
            cs = slice(c * A_DK, (c + 1) * A_DK)
            states.append(st.astype(MXU))
            st = st * decay[c, :, sl] + kv_all[:, cs]
        st_scr[hd] = st
        qd_blocks = jnp.where(in_block, jnp.tile(qd, (1, nc)), zero)
        o = o + lax.dot_general(qd_blocks, jnp.concatenate(states, axis=1), NT_DIMS,
                                preferred_element_type=F32)
        cat_scr[:, sl] = _rms(o, onorm) * gate[:, sl]

    h1 = h + jnp.dot(cat_scr[...].astype(MXU), wout_ref[...], preferred_element_type=F32)
    h1_ref[0] = h1
    _route_tail(h1, (b == 0) & (t == 0), gffn_ref, wr_ref, xext_ref, cls_ref, rank_ref, cnt_ref,
                run_scr)


def _l1_kernel(sinks_ref, h_ref, gmix_ref, wqt_ref, bqt_ref, wk_ref, bk_ref, wvt_ref, bvt_ref,
               wout_ref, bout_ref, gffn_ref, wr_ref,
               h1_ref, xext_ref, cls_ref, rank_ref, cnt_ref,
               qt_scr, k_scr, vt_scr, att_scr, run_scr):
    b = pl.program_id(0)
    t = pl.program_id(1)
    tb = h_ref.shape[1]

    @pl.when(t == 0)
    def _():
        k_scr[:, 0:WINDOW, :] = jnp.zeros((KV_HEADS, WINDOW, HEAD_DIM), k_scr.dtype)
        vt_scr[:, 0:WINDOW] = jnp.zeros((KV_HEADS * HEAD_DIM, WINDOW), vt_scr.dtype)

    h = h_ref[0]
    u = _rms(h, gmix_ref[...]).astype(MXU)
    scale = HEAD_DIM ** -0.5
    qt_scr[...] = ((lax.dot_general(wqt_ref[...], u, NT_DIMS, preferred_element_type=F32)
                    + bqt_ref[...]) * scale).astype(qt_scr.dtype)
    kz = (jnp.dot(u, wk_ref[...], preferred_element_type=F32) + bk_ref[...]).astype(k_scr.dtype)
    for kk in range(KV_HEADS):
        k_scr[kk, WINDOW:, :] = kz[:, kk * HEAD_DIM:(kk + 1) * HEAD_DIM]
    vt_scr[:, WINDOW:] = (lax.dot_general(wvt_ref[...], u, NT_DIMS, preferred_element_type=F32)
                          + bvt_ref[...]).astype(vt_scr.dtype)

    kj = lax.broadcasted_iota(I32, (2 * WINDOW, WINDOW), 0)
    qi = lax.broadcasted_iota(I32, (2 * WINDOW, WINDOW), 1)
    band = (kj > qi) & (kj <= qi + WINDOW)
    first_key = jnp.where(t > 0, 0, WINDOW)

    def all_heads(keep):
        return jnp.tile(jnp.where(keep, 1.0, 0.0), (1, GROUP)) > 0.5

    mask_first = all_heads(band & (kj >= first_key))
    mask_rest = all_heads(band)
    for n in range(tb // WINDOW):
        rs = slice(n * WINDOW, (n + 1) * WINDOW)
        win = slice(n * WINDOW, (n + 2) * WINDOW)
        mask = mask_first if n == 0 else mask_rest
        for kk in range(KV_HEADS):
            heads = range(kk * GROUP, (kk + 1) * GROUP)
            keys = k_scr[kk, win, :]
            vals_t = vt_scr[kk * HEAD_DIM:(kk + 1) * HEAD_DIM, win]
            q_t = jnp.concatenate(
                [qt_scr[hd * HEAD_DIM:(hd + 1) * HEAD_DIM, rs] for hd in heads], axis=1)
            sink = jnp.concatenate(
                [jnp.full((1, WINDOW), sinks_ref[hd], F32) for hd in heads], axis=1)
            s = jnp.dot(keys, q_t, preferred_element_type=F32)
            s = jnp.where(mask, s, -jnp.inf)
            m = jnp.maximum(jnp.max(s, axis=0, keepdims=True), sink)
            e = jnp.exp(s - m)
            denom = jnp.sum(e, axis=0, keepdims=True) + jnp.exp(sink - m)
            o_t = jnp.dot(vals_t, e.astype(MXU), preferred_element_type=F32) / denom
            for g2 in range(GROUP // 2):
                pair = jnp.concatenate([o_t[:, (2 * g2) * WINDOW:(2 * g2 + 1) * WINDOW],
                                        o_t[:, (2 * g2 + 1) * WINDOW:(2 * g2 + 2) * WINDOW]], axis=0)
                h0 = (kk * GROUP + 2 * g2) * HEAD_DIM
                att_scr[rs, h0:h0 + 2 * HEAD_DIM] = pair.T
    k_scr[:, 0:WINDOW, :] = k_scr[:, tb:tb + WINDOW, :]
    vt_scr[:, 0:WINDOW] = vt_scr[:, tb:tb + WINDOW]

    h1 = (h + jnp.dot(att_scr[...].astype(MXU), wout_ref[...], preferred_element_type=F32)
          + bout_ref[...])
    h1_ref[0] = h1
    _route_tail(h1, (b == 0) & (t == 0), gffn_ref, wr_ref, xext_ref, cls_ref, rank_ref, cnt_ref,
                run_scr)


def _row_copy(src, src_row, dst, dst_row, sem):
    return pltpu.make_async_copy(src.at[pl.ds(src_row, 1)], dst.at[pl.ds(dst_row, 1)], sem)


ISSUE_UNROLL = 8
DMA_THREADS = 2


def _scatter_kernel(pos_ref, pad0_ref, pad1_ref, nused_ref, x_ref, out_ref, zero_scr, sem,
                    pad_sem):
    rows = x_ref.shape[0]
    tm = zero_scr.shape[0]
    i = pl.program_id(0)
    base = i * rows

    @pl.when(i == 0)
    def _():
        zero_scr[...] = jnp.zeros_like(zero_scr)
        n_tiles = out_ref.shape[0] // tm

        def tile_copy(tile):
            dst = out_ref.at[pl.ds(pl.multiple_of(tile * tm, tm), tm)]
            return pltpu.make_async_copy(zero_scr, dst, pad_sem)

        def row_copy(r):
            return _row_copy(zero_scr, 0, out_ref, r, pad_sem)

        def for_all(on_tile, on_row):
            def tile_body(tile, carry):
                on_tile(tile)
                return carry

            def row_body(r, carry):
                on_row(r)
                return carry

            def class_body(c, carry):
                return lax.fori_loop(pad0_ref[c], pad1_ref[c], row_body, carry)

            lax.fori_loop(nused_ref[0], n_tiles, tile_body, 0)
            lax.fori_loop(0, N_CLASS, class_body, 0)

        for_all(lambda tile: tile_copy(tile).start(), lambda r: row_copy(r).start())
        for_all(lambda tile: tile_copy(tile).wait(), lambda r: row_copy(r).wait())

    def issue(g, carry):
        r0 = pl.multiple_of(g * ISSUE_UNROLL, ISSUE_UNROLL)
        group = x_ref.at[pl.ds(r0, ISSUE_UNROLL)]
        for k in range(ISSUE_UNROLL):
            _row_copy(group, k, out_ref, pos_ref[base + r0 + k],
                      sem).start(priority=k % DMA_THREADS)
        return carry

    lax.fori_loop(0, rows // ISSUE_UNROLL, issue, 0)
    pltpu.make_async_copy(x_ref, out_ref.at[pl.ds(0, rows)], sem).wait()


def _moe_kernel(e1_ref, e2_ref, nused_ref,
                x_ref, wg1_ref, wu1_ref, wd1_ref, wg2_ref, wu2_ref, wd2_ref, y_ref):
    i = pl.program_id(0)
    nused = nused_ref[0]

    @pl.when(i < nused)
    def _():
        x = x_ref[:, :D_MODEL].astype(MXU)
        comb = x_ref[:, D_MODEL:]
        lane = lax.broadcasted_iota(I32, comb.shape, 1)

        def expert(e, wg_ref, wu_ref, wd_ref):
            cw = jnp.sum(jnp.where(lane == e, comb, 0.0), axis=1, keepdims=True)
            gate = jnp.dot(x, wg_ref[...], preferred_element_type=F32)
            up = jnp.dot(x, wu_ref[...], preferred_element_type=F32)
            hidden = (gate * _sigmoid(gate)) * up
            return cw * jnp.dot(hidden.astype(MXU), wd_ref[...], preferred_element_type=F32)

        y_ref[...] = (expert(e1_ref[i], wg1_ref, wu1_ref, wd1_ref)
                      + expert(e2_ref[i], wg2_ref, wu2_ref, wd2_ref))

    @pl.when(i >= nused)
    def _():
        y_ref[...] = jnp.zeros_like(y_ref)


def _ple_kernel(pos_ref, h1_ref, p_ref, y_ref, gple_ref, wgate_ref, wproj_ref, gfin_ref,
                out_ref, ybuf, sem, *, final):
    rows = h1_ref.shape[0]
    i = pl.program_id(0)
    nsteps = pl.num_programs(0)
    slot = i % 2

    def gather(step, buf):
        def issue(g, carry):
            r0 = pl.multiple_of(g * ISSUE_UNROLL, ISSUE_UNROLL)
            group = ybuf.at[buf, pl.ds(r0, ISSUE_UNROLL)]
            for k in range(ISSUE_UNROLL):
                _row_copy(y_ref, pos_ref[step * rows + r0 + k], group, k,
                          sem.at[buf]).start(priority=k % DMA_THREADS)
            return carry

        lax.fori_loop(0, rows // ISSUE_UNROLL, issue, 0)

    def wait(buf):
        pltpu.make_async_copy(y_ref.at[pl.ds(0, rows)], ybuf.at[buf], sem.at[buf]).wait()

    @pl.when(i == 0)
    def _():
        gather(0, 0)

    wait(slot)
    h2 = h1_ref[...] + ybuf[slot]
    nxt = jnp.minimum(i + 1, nsteps - 1)
    for r in range(rows):
        group = ybuf.at[1 - slot, pl.ds((r // 8) * 8, 8)]
        _row_copy(y_ref, pos_ref[nxt * rows + r], group, r % 8,
                  sem.at[1 - slot]).start(priority=r % DMA_THREADS)

    proj = jnp.dot(p_ref[...].astype(MXU), wproj_ref[...], preferred_element_type=F32)
    gate = _sigmoid(jnp.dot(_rms(h2, gple_ref[...]).astype(MXU), wgate_ref[...],
                            preferred_element_type=F32))
    h3 = h2 + gate * proj
    if final:
        h3 = _rms(h3, gfin_ref[...])
    out_ref[...] = h3

    @pl.when(i == nsteps - 1)
    def _():
        wait(1 - slot)


def _const_spec(shape):
    nd = len(shape)
    return pl.BlockSpec(shape, lambda *_: (0,) * nd)


def _route_out(bsz, seq, tb):
    nt = seq // tb
    n = bsz * seq
    shapes = [
        jax.ShapeDtypeStruct((bsz, seq, D_MODEL), F32),
        jax.ShapeDtypeStruct((n, ROW_WORDS), F32),
        jax.ShapeDtypeStruct((bsz * nt, 1, tb), I32),
        jax.ShapeDtypeStruct((bsz * nt, 1, tb), I32),
        jax.ShapeDtypeStruct((CLASS_PAD, 128), F32),
    ]
    specs = [
        pl.BlockSpec((1, tb, D_MODEL), lambda b, t, *_: (b, t, 0)),
        pl.BlockSpec((tb, ROW_WORDS), lambda b, t, *_: (b * nt + t, 0)),
        pl.BlockSpec((1, 1, tb), lambda b, t, *_: (b * nt + t, 0, 0)),
        pl.BlockSpec((1, 1, tb), lambda b, t, *_: (b * nt + t, 0, 0)),
        pl.BlockSpec((CLASS_PAD, 128), lambda b, t, *_: (0, 0)),
    ]
    return shapes, specs


def _params():
    return pltpu.CompilerParams(dimension_semantics=("arbitrary", "arbitrary"),
                                vmem_limit_bytes=VMEM_LIMIT)


def _layer0_mixer(h, gmix, win, lb, onorm, convw, wout, gffn, wrt):
    bsz, seq, _ = h.shape
    tb = L0_BLOCK
    assert seq % tb == 0
    shapes, ospecs = _route_out(bsz, seq, tb)
    return pl.pallas_call(
        _l0_kernel,
        grid=(bsz, seq // tb),
        in_specs=[
            pl.BlockSpec((1, tb, D_MODEL), lambda b, t: (b, t, 0)),
            _const_spec((1, D_MODEL)),
            _const_spec((D_MODEL, AB_IN)),
            _const_spec((1, A_WIDTH)),
            _const_spec((1, A_DK)),
            _const_spec((3, A_WIDTH)),
            _const_spec((D_MODEL, D_MODEL)),
            _const_spec((1, D_MODEL)),
            _const_spec((D_MODEL, 128)),
        ],
        out_specs=ospecs,
        out_shape=shapes,
        scratch_shapes=[
            pltpu.VMEM((tb, AB_IN), F32),
            pltpu.VMEM((tb, D_MODEL), F32),
            pltpu.VMEM((A_HEADS, A_DK, A_DK), F32),
            pltpu.VMEM((8, A_WIDTH), F32),
            pltpu.VMEM((CLASS_PAD, 128), F32),
        ],
        compiler_params=_params(),
        name="layer0_mixer",
    )(h, gmix, win, lb, onorm, convw, wout, gffn, wrt)


def _layer1_mixer(h, sinks, gmix, win, bin_, wout, bout, gffn, wrt):
    bsz, seq, _ = h.shape
    tb = L1_BLOCK
    assert seq % tb == 0
    shapes, ospecs = _route_out(bsz, seq, tb)
    qw = Q_HEADS * HEAD_DIM
    vw = KV_HEADS * HEAD_DIM
    wqt, wk, wvt = win[:, :qw].T, win[:, qw:qw + vw], win[:, qw + vw:].T
    bqt, bk, bvt = bin_[:, :qw].reshape(qw, 1), bin_[:, qw:qw + vw], bin_[:, qw + vw:].reshape(vw, 1)
    grid_spec = pltpu.PrefetchScalarGridSpec(
        num_scalar_prefetch=1,
        grid=(bsz, seq // tb),
        in_specs=[
            pl.BlockSpec((1, tb, D_MODEL), lambda b, t, *_: (b, t, 0)),
            _const_spec((1, D_MODEL)),
            _const_spec((qw, D_MODEL)),
            _const_spec((qw, 1)),
            _const_spec((D_MODEL, vw)),
            _const_spec((1, vw)),
            _const_spec((vw, D_MODEL)),
            _const_spec((vw, 1)),
            _const_spec((D_MODEL, D_MODEL)),
            _const_spec((1, D_MODEL)),
            _const_spec((1, D_MODEL)),
            _const_spec((D_MODEL, 128)),
        ],
        out_specs=ospecs,
        scratch_shapes=[
            pltpu.VMEM((qw, tb), MXU),
            pltpu.VMEM((KV_HEADS, tb + WINDOW, HEAD_DIM), MXU),
            pltpu.VMEM((vw, tb + WINDOW), MXU),
            pltpu.VMEM((tb, D_MODEL), F32),
            pltpu.VMEM((CLASS_PAD, 128), F32),
        ],
    )
    return pl.pallas_call(
        _l1_kernel, grid_spec=grid_spec, out_shape=shapes, compiler_params=_params(),
        name="layer1_mixer",
    )(sinks, h, gmix, wqt, bqt, wk, bk, wvt, bvt, wout, bout, gffn, wrt)


def _dispatch(plan, xext):
    pos, _, _, nused, pad0, pad1, n_rows = plan
    n = xext.shape[0]
    blk = MOVE_BLOCK
    assert n % blk == 0
    grid_spec = pltpu.PrefetchScalarGridSpec(
        num_scalar_prefetch=4,
        grid=(n // blk,),
        in_specs=[pl.BlockSpec((blk, ROW_WORDS), lambda i, *_: (i, 0))],
        out_specs=pl.BlockSpec(memory_space=pl.ANY),
        scratch_shapes=[pltpu.VMEM((MOE_TILE, ROW_WORDS), F32), pltpu.SemaphoreType.DMA(()),
                        pltpu.SemaphoreType.DMA(())],
    )
    return pl.pallas_call(
        _scatter_kernel, grid_spec=grid_spec,
        out_shape=jax.ShapeDtypeStruct((n_rows, ROW_WORDS), F32),
        compiler_params=pltpu.CompilerParams(dimension_semantics=("arbitrary",),
                                             vmem_limit_bytes=VMEM_LIMIT),
        name="moe_dispatch",
    )(pos, pad0, pad1, nused, xext)


def _moe(plan, xs, layer, wg, wu, wd):
    _, e1, e2, nused, _, _, n_rows = plan
    tm = MOE_TILE
    n_tiles = n_rows // tm

    def wspec(shape, which):
        return pl.BlockSpec((None, None) + shape,
                            lambda i, e1, e2, *_: (layer, (e1, e2)[which][i], 0, 0))

    grid_spec = pltpu.PrefetchScalarGridSpec(
        num_scalar_prefetch=3,
        grid=(n_tiles,),
        in_specs=[
            pl.BlockSpec((tm, ROW_WORDS), lambda i, *_: (i, 0)),
            wspec((D_MODEL, D_EXPERT), 0), wspec((D_MODEL, D_EXPERT), 0), wspec((D_EXPERT, D_MODEL), 0),
            wspec((D_MODEL, D_EXPERT), 1), wspec((D_MODEL, D_EXPERT), 1), wspec((D_EXPERT, D_MODEL), 1),
        ],
        out_specs=pl.BlockSpec((tm, D_MODEL), lambda i, *_: (i, 0)),
    )
    return pl.pallas_call(
        _moe_kernel, grid_spec=grid_spec,
        out_shape=jax.ShapeDtypeStruct((n_rows, D_MODEL), F32),
        compiler_params=pltpu.CompilerParams(dimension_semantics=("arbitrary",),
                                             vmem_limit_bytes=VMEM_LIMIT),
        name="moe_experts",
    )(e1, e2, nused, xs, wg, wu, wd, wg, wu, wd)


def _combine_ple(pos, h1, p, layer, ys, gple, wgate, wproj, gfin, final):
    n = h1.shape[0]
    blk = MOVE_BLOCK
    assert n % blk == 0
    grid_spec = pltpu.PrefetchScalarGridSpec(
        num_scalar_prefetch=1,
        grid=(n // blk,),
        in_specs=[
            pl.BlockSpec((blk, D_MODEL), lambda i, *_: (i, 0)),
            pl.BlockSpec((None, blk, PLE_DIM), lambda i, *_: (layer, i, 0)),
            pl.BlockSpec(memory_space=pl.ANY),
            _const_spec((1, D_MODEL)),
            _const_spec((D_MODEL, D_MODEL)),
            _const_spec((PLE_DIM, D_MODEL)),
            _const_spec((1, D_MODEL)),
        ],
        out_specs=pl.BlockSpec((blk, D_MODEL), lambda i, *_: (i, 0)),
        scratch_shapes=[pltpu.VMEM((2, blk, D_MODEL), F32), pltpu.SemaphoreType.DMA((2,))],
    )
    return pl.pallas_call(
        functools.partial(_ple_kernel, final=final), grid_spec=grid_spec,
        out_shape=jax.ShapeDtypeStruct((n, D_MODEL), F32),
        compiler_params=pltpu.CompilerParams(dimension_semantics=("arbitrary",),
                                             vmem_limit_bytes=VMEM_LIMIT),
        name="combine_ple",
    )(pos, h1, p, ys, gple, wgate, wproj, gfin)


def _routing_plan(cls, rank, counts, n_tokens):
    tm = MOE_TILE
    n_tiles = n_tokens // tm + N_CLASS
    cnt = counts[:N_CLASS, 0].astype(I32)
    tiles_per_class = (cnt + tm - 1) // tm
    tile_end = jnp.cumsum(tiles_per_class)
    class_row0 = (tile_end - tiles_per_class) * tm
    pos = jnp.take(class_row0, cls.reshape(-1)) + rank.reshape(-1)
    nused = tile_end[-1]
    tidx = jnp.minimum(jnp.arange(n_tiles, dtype=I32), nused - 1)
    tcls = jnp.sum((tidx[:, None] >= tile_end[None, :]).astype(I32), axis=1)
    tcls = jnp.minimum(tcls, N_CLASS - 1)
    lo = jnp.array([a for a, _ in PAIRS], I32)
    hi = jnp.array([b for _, b in PAIRS], I32)
    grp = tcls // len(PAIRS)
    e1 = grp * PER_GROUP + jnp.take(lo, tcls % len(PAIRS))
    e2 = grp * PER_GROUP + jnp.take(hi, tcls % len(PAIRS))
    pad0 = class_row0 + cnt
    pad1 = tile_end * tm
    return (pos.astype(I32), e1.astype(I32), e2.astype(I32), nused.reshape(1).astype(I32),
            pad0.astype(I32), pad1.astype(I32), n_tiles * tm)


def _ffn_and_ple(h1, xext, cls, rank, counts, p, layer, wg, wu, wd, gple, wgate, wproj, gfin,
                 final):
    bsz, seq, _ = h1.shape
    n = bsz * seq
    plan = _routing_plan(cls, rank, counts, n)
    ys = _moe(plan, _dispatch(plan, xext), layer, wg, wu, wd)
    out = _combine_ple(plan[0], h1.reshape(n, D_MODEL), p.reshape(p.shape[0], n, PLE_DIM), layer,
                       ys, gple, wgate, wproj, gfin, final)
    return out.reshape(bsz, seq, D_MODEL)


def kernel(x, p, norm_mix, norm_ffn, norm_ple, norm_final, w_in_ab, hgrn_lb_logits, hgrn_out_norm, conv_w, w_out_ab, w_in_c, b_in_c, sinks, w_out_c, b_out_c, w_router, w_gate_e, w_up_e, w_down_e, w_ple_gate, w_ple_proj):
    depth = p.shape[0]
    lower_bounds = jnp.cumsum(jax.nn.softmax(hgrn_lb_logits.astype(F32), axis=0), axis=0)
    wr32 = w_router.astype(F32)
    wr_hi = wr32.astype(BF16)
    wr_lo = (wr32 - wr_hi.astype(F32)).astype(BF16)
    wrt = jnp.concatenate(
        [wr_hi, wr_lo, jnp.zeros((D_MODEL, 128 - 2 * N_EXPERTS), BF16)], axis=1)
    gfin = norm_final.reshape(1, D_MODEL)
    wg, wu, wd = w_gate_e.astype(MXU), w_up_e.astype(MXU), w_down_e.astype(MXU)
    h = x
    for i in range(depth):
        j = i // 2
        gmix = norm_mix[i].reshape(1, D_MODEL)
        gffn = norm_ffn[i].reshape(1, D_MODEL)
        if i % 2 == 0:
            h1, xext, cls, rank, counts = _layer0_mixer(
                h, gmix, w_in_ab[j].astype(MXU), lower_bounds[i].reshape(1, A_WIDTH),
                hgrn_out_norm[j].reshape(1, A_DK), conv_w[j], w_out_ab[j].astype(MXU), gffn, wrt)
        else:
            h1, xext, cls, rank, counts = _layer1_mixer(
                h, sinks[j].astype(F32), gmix, w_in_c[j].astype(MXU), b_in_c[j].reshape(1, C_IN),
                w_out_c[j].astype(MXU), b_out_c[j].reshape(1, D_MODEL), gffn, wrt)
        h = _ffn_and_ple(
            h1, xext, cls, rank, counts, p, i, wg, wu, wd,
            norm_ple[i].reshape(1, D_MODEL), w_ple_gate[i].astype(MXU), w_ple_proj[i].astype(MXU),
            gfin, final=(i == depth - 1))
    return h
```

```python
import functools

import jax
import jax.numpy as jnp
from jax import lax
from jax.experimental import pallas as pl
from jax.experimental.pallas import tpu as pltpu

F32 = jnp.float32
BF16 = jnp.bfloat16
MXU = jnp.bfloat16
I32 = jnp.int32

D_MODEL = 1024
EPS = 1e-6
A_WIDTH = 512
A_HEADS = 4
A_DK = 128
A_CHUNK = 32
AB_IN = 3584
HEAD_DIM = 64
Q_HEADS = 16
KV_HEADS = 2
GROUP = 8
WINDOW = 128
C_IN = 1280
N_EXPERTS = 16
N_GROUPS = 4
PER_GROUP = 4
D_EXPERT = 512
PLE_DIM = 256

PAIRS = ((0, 1), (0, 2), (0, 3), (1, 2), (1, 3), (2, 3))
N_CLASS = N_GROUPS * len(PAIRS)
CLASS_PAD = 32
ROW_WORDS = D_MODEL + 128

L0_BLOCK = 256
L1_BLOCK = 512
MOE_TILE = 256
MOVE_BLOCK = 512
VMEM_LIMIT = 56 * 1024 * 1024

NT_DIMS = (((1,), (1,)), ((), ()))
TN_DIMS = (((0,), (0,)), ((), ()))


def _rms(x, g):
    ms = jnp.mean(x * x, axis=-1, keepdims=True)
    return x * lax.rsqrt(ms + EPS) * g


def _sigmoid(x):
    return 1.0 / (1.0 + jnp.exp(-x))


def _route_tail(h1, first, gffn_ref, wr_ref, xext_ref, cls_ref, rank_ref, cnt_ref, run_scr):
    tb = h1.shape[0]

    @pl.when(first)
    def _():
        run_scr[...] = jnp.zeros_like(run_scr)

    xn = _rms(h1, gffn_ref[...])
    x_hi = xn.astype(BF16)
    x_lo = (xn - x_hi.astype(F32)).astype(BF16)
    prod = (jnp.dot(x_hi, wr_ref[...], preferred_element_type=F32)
            + jnp.dot(x_lo, wr_ref[...], preferred_element_type=F32)).T
    logits = prod[0:N_EXPERTS, :] + prod[N_EXPERTS:2 * N_EXPERTS, :]
    mx = jnp.max(logits, axis=0, keepdims=True)
    ex = jnp.exp(logits - mx)
    sc = ex / jnp.sum(ex, axis=0, keepdims=True)
    rows = [sc[i:i + 1, :] for i in range(N_EXPERTS)]

    gscore = []
    for g in range(N_GROUPS):
        v = rows[PER_GROUP * g:PER_GROUP * (g + 1)]
        best = v[0] + v[1]
        for (i, j) in PAIRS[1:]:
            best = jnp.maximum(best, v[i] + v[j])
        gscore.append(best)
    gmax = jnp.maximum(jnp.maximum(gscore[0], gscore[1]), jnp.maximum(gscore[2], gscore[3]))
    sel = jnp.where(gscore[0] >= gmax, 0, jnp.where(gscore[1] >= gmax, 1,
                                                    jnp.where(gscore[2] >= gmax, 2, 3))).astype(I32)
    v = [jnp.where(sel == 0, rows[i], jnp.where(sel == 1, rows[4 + i],
                                                jnp.where(sel == 2, rows[8 + i], rows[12 + i])))
         for i in range(PER_GROUP)]
    chosen = []
    for i in range(PER_GROUP):
        r = jnp.zeros_like(sel)
        for j in range(PER_GROUP):
            if j == i:
                continue
            ahead = (v[j] >= v[i]) if j < i else (v[j] > v[i])
            r = r + jnp.where(ahead, 1, 0).astype(I32)
        chosen.append(r < 2)
    vsum = jnp.zeros_like(v[0])
    code = jnp.zeros_like(sel)
    for i in range(PER_GROUP):
        vsum = vsum + jnp.where(chosen[i], v[i], 0.0)
        code = code + jnp.where(chosen[i], 1 << i, 0).astype(I32)
    wts = [jnp.where(chosen[i], v[i] / vsum, 0.0) for i in range(PER_GROUP)]
    pair = jnp.where(code == 3, 0, jnp.where(code == 5, 1, jnp.where(code == 9, 2,
                     jnp.where(code == 6, 3, jnp.where(code == 10, 4, 5))))).astype(I32)
    cls = sel * len(PAIRS) + pair

    erow = lax.broadcasted_iota(I32, (128, tb), 0)
    comb_t = jnp.zeros((128, tb), F32)
    for i in range(PER_GROUP):
        comb_t = comb_t + jnp.where(erow == sel * PER_GROUP + i, wts[i], 0.0)
    comb = comb_t.T

    xext_ref[:, :D_MODEL] = xn
    xext_ref[:, D_MODEL:] = comb

    crow = lax.broadcasted_iota(I32, (CLASS_PAD, tb), 0)
    onehot = jnp.where(crow == cls, 1.0, 0.0)
    ii = lax.broadcasted_iota(I32, (tb, tb), 0)
    jj = lax.broadcasted_iota(I32, (tb, tb), 1)
    upper = jnp.where(ii < jj, 1.0, 0.0).astype(MXU)
    prefix = jnp.dot(onehot.astype(MXU), upper, preferred_element_type=F32)
    run = run_scr[...]
    rank = jnp.sum(onehot * (prefix + run[:, 0:1]), axis=0, keepdims=True)
    run_new = run + jnp.sum(onehot, axis=1, keepdims=True)
    run_scr[...] = run_new
    cnt_ref[...] = run_new
    cls_ref[0] = cls
    rank_ref[0] = rank.astype(I32)


def _l0_kernel(h_ref, gmix_ref, win_ref, lb_ref, onorm_ref, convw_ref, wout_ref, gffn_ref, wr_ref,
               h1_ref, xext_ref, cls_ref, rank_ref, cnt_ref,
               z_scr, cat_scr, st_scr, carry_scr, run_scr):
    b = pl.program_id(0)
    t = pl.program_id(1)
    tb = h_ref.shape[1]

    @pl.when(t == 0)
    def _():
        st_scr[...] = jnp.zeros_like(st_scr)
        carry_scr[...] = jnp.zeros_like(carry_scr)

    h = h_ref[0]
    u = _rms(h, gmix_ref[...]).astype(MXU)
    z_scr[...] = jnp.dot(u, win_ref[...], preferred_element_type=F32)

    gate_b = z_scr[:, 4 * A_WIDTH:5 * A_WIDTH]
    cb = z_scr[:, 5 * A_WIDTH:6 * A_WIDTH] * z_scr[:, 6 * A_WIDTH:7 * A_WIDTH]
    row = lax.broadcasted_iota(I32, cb.shape, 0)
    prev1 = carry_scr[7:8, :]
    prev2 = carry_scr[6:7, :]
    m1 = jnp.where(row == 0, prev1, pltpu.roll(cb, 1, axis=0))
    m2 = jnp.where(row == 0, prev2, jnp.where(row == 1, prev1, pltpu.roll(cb, 2, axis=0)))
    cw = convw_ref[...]
    cat_scr[:, A_WIDTH:] = gate_b * (cw[0:1, :] * m2 + cw[1:2, :] * m1 + cw[2:3, :] * cb)
    carry_scr[...] = cb[tb - 8:tb, :]

    nc = tb // A_CHUNK
    lb = lb_ref[...]
    onorm = onorm_ref[...]
    zq = z_scr[:, 0:A_WIDTH]
    zf = z_scr[:, A_WIDTH:2 * A_WIDTH]
    zi = z_scr[:, 2 * A_WIDTH:3 * A_WIDTH]
    zg = z_scr[:, 3 * A_WIDTH:4 * A_WIDTH]
    f = lb + (1.0 - lb) * _sigmoid(zf)
    logf = jnp.log(f)
    ri = lax.broadcasted_iota(I32, (tb, tb), 0)
    rj = lax.broadcasted_iota(I32, (tb, tb), 1)
    same_chunk = (ri // A_CHUNK) == (rj // A_CHUNK)
    causal = same_chunk & (ri >= rj)
    tri = jnp.where(causal, 1.0, 0.0).astype(BF16)
    p1 = logf.astype(BF16)
    r1 = logf - p1.astype(F32)
    p2 = r1.astype(BF16)
    p3 = (r1 - p2.astype(F32)).astype(BF16)
    pieces = jnp.dot(tri, jnp.concatenate([p1, p2, p3], axis=1), preferred_element_type=F32)
    bcum = (pieces[:, 0:A_WIDTH] + pieces[:, A_WIDTH:2 * A_WIDTH]) + pieces[:, 2 * A_WIDTH:]
    blast3 = bcum.reshape(nc, A_CHUNK, A_WIDTH)[:, A_CHUNK - 1:A_CHUNK, :]
    blast = jnp.broadcast_to(blast3, (nc, A_CHUNK, A_WIDTH)).reshape(tb, A_WIDTH)
    k = 1.0 - f
    q_dec = ((zq * _sigmoid(zq)) * jnp.exp(bcum)).astype(MXU)
    k_inv = (k * jnp.exp(-bcum)).astype(MXU)
    k_tail = (k * jnp.exp(blast - bcum)).astype(MXU)
    decay = jnp.exp(blast3)
    gate = _sigmoid(zg)
    vals = zi.astype(MXU)
    in_block = (lax.broadcasted_iota(I32, (tb, nc * A_DK), 0) // A_CHUNK
                == lax.broadcasted_iota(I32, (tb, nc * A_DK), 1) // A_DK)
    zero = jnp.zeros((), MXU)
    for hd in range(A_HEADS):
        sl = slice(hd * A_DK, (hd + 1) * A_DK)
        qd = q_dec[:, sl]
        vv = vals[:, sl]
        scores = lax.dot_general(qd, k_inv[:, sl], NT_DIMS, preferred_element_type=F32)
        scores = jnp.where(causal, scores, 0.0).astype(MXU)
        o = jnp.dot(scores, vv, preferred_element_type=F32)
        kt_blocks = jnp.where(in_block, jnp.tile(k_tail[:, sl], (1, nc)), zero)
        kv_all = lax.dot_general(vv, kt_blocks, TN_DIMS, preferred_element_type=F32)
        st = st_scr[hd]
        states = []
        for c in range(nc):
            cs = slice(c * A_DK, (c + 1) * A_DK)
            states.append(st.astype(MXU))
            st = st * decay[c, :, sl] + kv_all[:, cs]
        st_scr[hd] = st
        qd_blocks = jnp.where(in_block, jnp.tile(qd, (1, nc)), zero)
        o = o + lax.dot_general(qd_blocks, jnp.concatenate(states, axis=1), NT_DIMS,
                                preferred_element_type=F32)
        cat_scr[:, sl] = _rms(o, onorm) * gate[:, sl]

    h1 = h + jnp.dot(cat_scr[...].astype(MXU), wout_ref[...], preferred_element_type=F32)
    h1_ref[0] = h1
    _route_tail(h1, (b == 0) & (t == 0), gffn_ref, wr_ref, xext_ref, cls_ref, rank_ref, cnt_ref,
                run_scr)


def _l1_kernel(sinks_ref, h_ref, gmix_ref, wqt_ref, bqt_ref, wk_ref, bk_ref, wvt_ref, bvt_ref,
               wout_ref, bout_ref, gffn_ref, wr_ref,
               h1_ref, xext_ref, cls_ref, rank_ref, cnt_ref,
               qt_scr, k_scr, vt_scr, att_scr, run_scr):
    b = pl.program_id(0)
    t = pl.program_id(1)
    tb = h_ref.shape[1]

    @pl.when(t == 0)
    def _():
        k_scr[:, 0:WINDOW, :] = jnp.zeros((KV_HEADS, WINDOW, HEAD_DIM), k_scr.dtype)
        vt_scr[:, 0:WINDOW] = jnp.zeros((KV_HEADS * HEAD_DIM, WINDOW), vt_scr.dtype)

    h = h_ref[0]
    u = _rms(h, gmix_ref[...]).astype(MXU)
    scale = HEAD_DIM ** -0.5
    qt_scr[...] = ((lax.dot_general(wqt_ref[...], u, NT_DIMS, preferred_element_type=F32)
                    + bqt_ref[...]) * scale).astype(qt_scr.dtype)
    kz = (jnp.dot(u, wk_ref[...], preferred_element_type=F32) + bk_ref[...]).astype(k_scr.dtype)
    for kk in range(KV_HEADS):
        k_scr[kk, WINDOW:, :] = kz[:, kk * HEAD_DIM:(kk + 1) * HEAD_DIM]
    vt_scr[:, WINDOW:] = (lax.dot_general(wvt_ref[...], u, NT_DIMS, preferred_element_type=F32)
                          + bvt_ref[...]).astype(vt_scr.dtype)

    kj = lax.broadcasted_iota(I32, (2 * WINDOW, WINDOW), 0)
    qi = lax.broadcasted_iota(I32, (2 * WINDOW, WINDOW), 1)
    band = (kj > qi) & (kj <= qi + WINDOW)
    first_key = jnp.where(t > 0, 0, WINDOW)

    def all_heads(keep):
        return jnp.tile(jnp.where(keep, 1.0, 0.0), (1, GROUP)) > 0.5

    mask_first = all_heads(band & (kj >= first_key))
    mask_rest = all_heads(band)
    for n in range(tb // WINDOW):
        rs = slice(n * WINDOW, (n + 1) * WINDOW)
        win = slice(n * WINDOW, (n + 2) * WINDOW)
        mask = mask_first if n == 0 else mask_rest
        for kk in range(KV_HEADS):
            heads = range(kk * GROUP, (kk + 1) * GROUP)
            keys = k_scr[kk, win, :]
            vals_t = vt_scr[kk * HEAD_DIM:(kk + 1) * HEAD_DIM, win]
            q_t = jnp.concatenate(
                [qt_scr[hd * HEAD_DIM:(hd + 1) * HEAD_DIM, rs] for hd in heads], axis=1)
            sink = jnp.concatenate(
                [jnp.full((1, WINDOW), sinks_ref[hd], F32) for hd in heads], axis=1)
            s = jnp.dot(keys, q_t, preferred_element_type=F32)
            s = jnp.where(mask, s, -jnp.inf)
            m = jnp.maximum(jnp.max(s, axis=0, keepdims=True), sink)
            e = jnp.exp(s - m)
            denom = jnp.sum(e, axis=0, keepdims=True) + jnp.exp(sink - m)
            o_t = jnp.dot(vals_t, e.astype(MXU), preferred_element_type=F32) / denom
            for g2 in range(GROUP // 2):
                pair = jnp.concatenate([o_t[:, (2 * g2) * WINDOW:(2 * g2 + 1) * WINDOW],
                                        o_t[:, (2 * g2 + 1) * WINDOW:(2 * g2 + 2) * WINDOW]], axis=0)
                h0 = (kk * GROUP + 2 * g2) * HEAD_DIM
                att_scr[rs, h0:h0 + 2 * HEAD_DIM] = pair.T
    k_scr[:, 0:WINDOW, :] = k_scr[:, tb:tb + WINDOW, :]
    vt_scr[:, 0:WINDOW] = vt_scr[:, tb:tb + WINDOW]

    h1 = (h + jnp.dot(att_scr[...].astype(MXU), wout_ref[...], preferred_element_type=F32)
          + bout_ref[...])
    h1_ref[0] = h1
    _route_tail(h1, (b == 0) & (t == 0), gffn_ref, wr_ref, xext_ref, cls_ref, rank_ref, cnt_ref,
                run_scr)


def _row_copy(src, src_row, dst, dst_row, sem):
    return pltpu.make_async_copy(src.at[pl.ds(src_row, 1)], dst.at[pl.ds(dst_row, 1)], sem)


ISSUE_UNROLL = 8
DMA_THREADS = 2


def _scatter_kernel(pos_ref, pad0_ref, pad1_ref, nused_ref, x_ref, out_ref, zero_scr, sem,
                    pad_sem):
    rows = x_ref.shape[0]
    tm = zero_scr.shape[0]
    i = pl.program_id(0)
    base = i * rows

    @pl.when(i == 0)
    def _():
        zero_scr[...] = jnp.zeros_like(zero_scr)
        n_tiles = out_ref.shape[0] // tm

        def tile_copy(tile):
            dst = out_ref.at[pl.ds(pl.multiple_of(tile * tm, tm), tm)]
            return pltpu.make_async_copy(zero_scr, dst, pad_sem)

        def row_copy(r):
            return _row_copy(zero_scr, 0, out_ref, r, pad_sem)

        def for_all(on_tile, on_row):
            def tile_body(tile, carry):
                on_tile(tile)
                return carry

            def row_body(r, carry):
                on_row(r)
                return carry

            def class_body(c, carry):
                return lax.fori_loop(pad0_ref[c], pad1_ref[c], row_body, carry)

            lax.fori_loop(nused_ref[0], n_tiles, tile_body, 0)
            lax.fori_loop(0, N_CLASS, class_body, 0)

        for_all(lambda tile: tile_copy(tile).start(), lambda r: row_copy(r).start())
        for_all(lambda tile: tile_copy(tile).wait(), lambda r: row_copy(r).wait())

    def issue(g, carry):
        r0 = pl.multiple_of(g * ISSUE_UNROLL, ISSUE_UNROLL)
        group = x_ref.at[pl.ds(r0, ISSUE_UNROLL)]
        for k in range(ISSUE_UNROLL):
            _row_copy(group, k, out_ref, pos_ref[base + r0 + k], sem).start()
        return carry

    lax.fori_loop(0, rows // ISSUE_UNROLL, issue, 0)
    pltpu.make_async_copy(x_ref, out_ref.at[pl.ds(0, rows)], sem).wait()


def _moe_kernel(e1_ref, e2_ref, nused_ref,
                x_ref, wg1_ref, wu1_ref, wd1_ref, wg2_ref, wu2_ref, wd2_ref, y_ref):
    i = pl.program_id(0)
    nused = nused_ref[0]

    @pl.when(i < nused)
    def _():
        x = x_ref[:, :D_MODEL].astype(MXU)
        comb = x_ref[:, D_MODEL:]
        lane = lax.broadcasted_iota(I32, comb.shape, 1)

        def expert(e, wg_ref, wu_ref, wd_ref):
            cw = jnp.sum(jnp.where(lane == e, comb, 0.0), axis=1, keepdims=True)
            gate = jnp.dot(x, wg_ref[...], preferred_element_type=F32)
            up = jnp.dot(x, wu_ref[...], preferred_element_type=F32)
            hidden = (gate * _sigmoid(gate)) * up
            return cw * jnp.dot(hidden.astype(MXU), wd_ref[...], preferred_element_type=F32)

        y_ref[...] = (expert(e1_ref[i], wg1_ref, wu1_ref, wd1_ref)
                      + expert(e2_ref[i], wg2_ref, wu2_ref, wd2_ref))

    @pl.when(i >= nused)
    def _():
        y_ref[...] = jnp.zeros_like(y_ref)


def _ple_kernel(pos_ref, h1_ref, p_ref, y_ref, gple_ref, wgate_ref, wproj_ref, gfin_ref,
                out_ref, ybuf, sem, *, final):
    rows = h1_ref.shape[0]
    i = pl.program_id(0)
    nsteps = pl.num_programs(0)
    slot = i % 2

    def gather(step, buf):
        def issue(g, carry):
            r0 = pl.multiple_of(g * ISSUE_UNROLL, ISSUE_UNROLL)
            group = ybuf.at[buf, pl.ds(r0, ISSUE_UNROLL)]
            for k in range(ISSUE_UNROLL):
                _row_copy(y_ref, pos_ref[step * rows + r0 + k], group, k,
                          sem.at[buf]).start(priority=k % DMA_THREADS)
            return carry

        lax.fori_loop(0, rows // ISSUE_UNROLL, issue, 0)

    def wait(buf):
        pltpu.make_async_copy(y_ref.at[pl.ds(0, rows)], ybuf.at[buf], sem.at[buf]).wait()

    @pl.when(i == 0)
    def _():
        gather(0, 0)

    wait(slot)
    h2 = h1_ref[...] + ybuf[slot]
    nxt = jnp.minimum(i + 1, nsteps - 1)
    for r in range(rows):
        group = ybuf.at[1 - slot, pl.ds((r // 8) * 8, 8)]
        _row_copy(y_ref, pos_ref[nxt * rows + r], group, r % 8,
                  sem.at[1 - slot]).start(priority=r % DMA_THREADS)

    proj = jnp.dot(p_ref[...].astype(MXU), wproj_ref[...], preferred_element_type=F32)
    gate = _sigmoid(jnp.dot(_rms(h2, gple_ref[...]).astype(MXU), wgate_ref[...],
                            preferred_element_type=F32))
    h3 = h2 + gate * proj
    if final:
        h3 = _rms(h3, gfin_ref[...])
    out_ref[...] = h3

    @pl.when(i == nsteps - 1)
    def _():
        wait(1 - slot)


def _const_spec(shape):
    nd = len(shape)
    return pl.BlockSpec(shape, lambda *_: (0,) * nd)


def _route_out(bsz, seq, tb):
    nt = seq // tb
    n = bsz * seq
    shapes = [
        jax.ShapeDtypeStruct((bsz, seq, D_MODEL), F32),
        jax.ShapeDtypeStruct((n, ROW_WORDS), F32),
        jax.ShapeDtypeStruct((bsz * nt, 1, tb), I32),
        jax.ShapeDtypeStruct((bsz * nt, 1, tb), I32),
        jax.ShapeDtypeStruct((CLASS_PAD, 128), F32),
    ]
    specs = [
        pl.BlockSpec((1, tb, D_MODEL), lambda b, t, *_: (b, t, 0)),
        pl.BlockSpec((tb, ROW_WORDS), lambda b, t, *_: (b * nt + t, 0)),
        pl.BlockSpec((1, 1, tb), lambda b, t, *_: (b * nt + t, 0, 0)),
        pl.BlockSpec((1, 1, tb), lambda b, t, *_: (b * nt + t, 0, 0)),
        pl.BlockSpec((CLASS_PAD, 128), lambda b, t, *_: (0, 0)),
    ]
    return shapes, specs


def _params():
    return pltpu.CompilerParams(dimension_semantics=("arbitrary", "arbitrary"),
                                vmem_limit_bytes=VMEM_LIMIT)


def _layer0_mixer(h, gmix, win, lb, onorm, convw, wout, gffn, wrt):
    bsz, seq, _ = h.shape
    tb = L0_BLOCK
    assert seq % tb == 0
    shapes, ospecs = _route_out(bsz, seq, tb)
    return pl.pallas_call(
        _l0_kernel,
        grid=(bsz, seq // tb),
        in_specs=[
            pl.BlockSpec((1, tb, D_MODEL), lambda b, t: (b, t, 0)),
            _const_spec((1, D_MODEL)),
            _const_spec((D_MODEL, AB_IN)),
            _const_spec((1, A_WIDTH)),
            _const_spec((1, A_DK)),
            _const_spec((3, A_WIDTH)),
            _const_spec((D_MODEL, D_MODEL)),
            _const_spec((1, D_MODEL)),
            _const_spec((D_MODEL, 128)),
        ],
        out_specs=ospecs,
        out_shape=shapes,
        scratch_shapes=[
            pltpu.VMEM((tb, AB_IN), F32),
            pltpu.VMEM((tb, D_MODEL), F32),
            pltpu.VMEM((A_HEADS, A_DK, A_DK), F32),
            pltpu.VMEM((8, A_WIDTH), F32),
            pltpu.VMEM((CLASS_PAD, 128), F32),
        ],
        compiler_params=_params(),
        name="layer0_mixer",
    )(h, gmix, win, lb, onorm, convw, wout, gffn, wrt)


def _layer1_mixer(h, sinks, gmix, win, bin_, wout, bout, gffn, wrt):
    bsz, seq, _ = h.shape
    tb = L1_BLOCK
    assert seq % tb == 0
    shapes, ospecs = _route_out(bsz, seq, tb)
    qw = Q_HEADS * HEAD_DIM
    vw = KV_HEADS * HEAD_DIM
    wqt, wk, wvt = win[:, :qw].T, win[:, qw:qw + vw], win[:, qw + vw:].T
    bqt, bk, bvt = bin_[:, :qw].reshape(qw, 1), bin_[:, qw:qw + vw], bin_[:, qw + vw:].reshape(vw, 1)
    grid_spec = pltpu.PrefetchScalarGridSpec(
        num_scalar_prefetch=1,
        grid=(bsz, seq // tb),
        in_specs=[
            pl.BlockSpec((1, tb, D_MODEL), lambda b, t, *_: (b, t, 0)),
            _const_spec((1, D_MODEL)),
            _const_spec((qw, D_MODEL)),
            _const_spec((qw, 1)),
            _const_spec((D_MODEL, vw)),
            _const_spec((1, vw)),
            _const_spec((vw, D_MODEL)),
            _const_spec((vw, 1)),
            _const_spec((D_MODEL, D_MODEL)),
            _const_spec((1, D_MODEL)),
            _const_spec((1, D_MODEL)),
            _const_spec((D_MODEL, 128)),
        ],
        out_specs=ospecs,
        scratch_shapes=[
            pltpu.VMEM((qw, tb), MXU),
            pltpu.VMEM((KV_HEADS, tb + WINDOW, HEAD_DIM), MXU),
            pltpu.VMEM((vw, tb + WINDOW), MXU),
            pltpu.VMEM((tb, D_MODEL), F32),
            pltpu.VMEM((CLASS_PAD, 128), F32),
        ],
    )
    return pl.pallas_call(
        _l1_kernel, grid_spec=grid_spec, out_shape=shapes, compiler_params=_params(),
        name="layer1_mixer",
    )(sinks, h, gmix, wqt, bqt, wk, bk, wvt, bvt, wout, bout, gffn, wrt)


def _dispatch(plan, xext):
    pos, _, _, nused, pad0, pad1, n_rows = plan
    n = xext.shape[0]
    blk = MOVE_BLOCK
    assert n % blk == 0
    grid_spec = pltpu.PrefetchScalarGridSpec(
        num_scalar_prefetch=4,
        grid=(n // blk,),
        in_specs=[pl.BlockSpec((blk, ROW_WORDS), lambda i, *_: (i, 0))],
        out_specs=pl.BlockSpec(memory_space=pl.ANY),
        scratch_shapes=[pltpu.VMEM((MOE_TILE, ROW_WORDS), F32), pltpu.SemaphoreType.DMA(()),
                        pltpu.SemaphoreType.DMA(())],
    )
    return pl.pallas_call(
        _scatter_kernel, grid_spec=grid_spec,
        out_shape=jax.ShapeDtypeStruct((n_rows, ROW_WORDS), F32),
        compiler_params=pltpu.CompilerParams(dimension_semantics=("arbitrary",),
                                             vmem_limit_bytes=VMEM_LIMIT),
        name="moe_dispatch",
    )(pos, pad0, pad1, nused, xext)


def _moe(plan, xs, layer, wg, wu, wd):
    _, e1, e2, nused, _, _, n_rows = plan
    tm = MOE_TILE
    n_tiles = n_rows // tm

    def wspec(shape, which):
        return pl.BlockSpec((None, None) + shape,
                            lambda i, e1, e2, *_: (layer, (e1, e2)[which][i], 0, 0))

    grid_spec = pltpu.PrefetchScalarGridSpec(
        num_scalar_prefetch=3,
        grid=(n_tiles,),
        in_specs=[
            pl.BlockSpec((tm, ROW_WORDS), lambda i, *_: (i, 0)),
            wspec((D_MODEL, D_EXPERT), 0), wspec((D_MODEL, D_EXPERT), 0), wspec((D_EXPERT, D_MODEL), 0),
            wspec((D_MODEL, D_EXPERT), 1), wspec((D_MODEL, D_EXPERT), 1), wspec((D_EXPERT, D_MODEL), 1),
        ],
        out_specs=pl.BlockSpec((tm, D_MODEL), lambda i, *_: (i, 0)),
    )
    return pl.pallas_call(
        _moe_kernel, grid_spec=grid_spec,
        out_shape=jax.ShapeDtypeStruct((n_rows, D_MODEL), F32),
        compiler_params=pltpu.CompilerParams(dimension_semantics=("arbitrary",),
                                             vmem_limit_bytes=VMEM_LIMIT),
        name="moe_experts",
    )(e1, e2, nused, xs, wg, wu, wd, wg, wu, wd)


def _combine_ple(pos, h1, p, layer, ys, gple, wgate, wproj, gfin, final):
    n = h1.shape[0]
    blk = MOVE_BLOCK
    assert n % blk == 0
    grid_spec = pltpu.PrefetchScalarGridSpec(
        num_scalar_prefetch=1,
        grid=(n // blk,),
        in_specs=[
            pl.BlockSpec((blk, D_MODEL), lambda i, *_: (i, 0)),
            pl.BlockSpec((None, blk, PLE_DIM), lambda i, *_: (layer, i, 0)),
            pl.BlockSpec(memory_space=pl.ANY),
            _const_spec((1, D_MODEL)),
            _const_spec((D_MODEL, D_MODEL)),
            _const_spec((PLE_DIM, D_MODEL)),
            _const_spec((1, D_MODEL)),
        ],
        out_specs=pl.BlockSpec((blk, D_MODEL), lambda i, *_: (i, 0)),
        scratch_shapes=[pltpu.VMEM((2, blk, D_MODEL), F32), pltpu.SemaphoreType.DMA((2,))],
    )
    return pl.pallas_call(
        functools.partial(_ple_kernel, final=final), grid_spec=grid_spec,
        out_shape=jax.ShapeDtypeStruct((n, D_MODEL), F32),
        compiler_params=pltpu.CompilerParams(dimension_semantics=("arbitrary",),
                                             vmem_limit_bytes=VMEM_LIMIT),
        name="combine_ple",
    )(pos, h1, p, ys, gple, wgate, wproj, gfin)


def _routing_plan(cls, rank, counts, n_tokens):
    tm = MOE_TILE
    n_tiles = n_tokens // tm + N_CLASS
    cnt = counts[:N_CLASS, 0].astype(I32)
    tiles_per_class = (cnt + tm - 1) // tm
    tile_end = jnp.cumsum(tiles_per_class)
    class_row0 = (tile_end - tiles_per_class) * tm
    pos = jnp.take(class_row0, cls.reshape(-1)) + rank.reshape(-1)
    nused = tile_end[-1]
    tidx = jnp.minimum(jnp.arange(n_tiles, dtype=I32), nused - 1)
    tcls = jnp.sum((tidx[:, None] >= tile_end[None, :]).astype(I32), axis=1)
    tcls = jnp.minimum(tcls, N_CLASS - 1)
    lo = jnp.array([a for a, _ in PAIRS], I32)
    hi = jnp.array([b for _, b in PAIRS], I32)
    grp = tcls // len(PAIRS)
    e1 = grp * PER_GROUP + jnp.take(lo, tcls % len(PAIRS))
    e2 = grp * PER_GROUP + jnp.take(hi, tcls % len(PAIRS))
    pad0 = class_row0 + cnt
    pad1 = tile_end * tm
    return (pos.astype(I32), e1.astype(I32), e2.astype(I32), nused.reshape(1).astype(I32),
            pad0.astype(I32), pad1.astype(I32), n_tiles * tm)


def _ffn_and_ple(h1, xext, cls, rank, counts, p, layer, wg, wu, wd, gple, wgate, wproj, gfin,
                 final):
    bsz, seq, _ = h1.shape
    n = bsz * seq
    plan = _routing_plan(cls, rank, counts, n)
    ys = _moe(plan, _dispatch(plan, xext), layer, wg, wu, wd)
    out = _combine_ple(plan[0], h1.reshape(n, D_MODEL), p.reshape(p.shape[0], n, PLE_DIM), layer,
                       ys, gple, wgate, wproj, gfin, final)
    return out.reshape(bsz, seq, D_MODEL)


def kernel(x, p, norm_mix, norm_ffn, norm_ple, norm_final, w_in_ab, hgrn_lb_logits, hgrn_out_norm, conv_w, w_out_ab, w_in_c, b_in_c, sinks, w_out_c, b_out_c, w_router, w_gate_e, w_up_e, w_down_e, w_ple_gate, w_ple_proj):
    depth = p.shape[0]
    lower_bounds = jnp.cumsum(jax.nn.softmax(hgrn_lb_logits.astype(F32), axis=0), axis=0)
    wr32 = w_router.astype(F32)
    wr_hi = wr32.astype(BF16)
    wr_lo = (wr32 - wr_hi.astype(F32)).astype(BF16)
    wrt = jnp.concatenate(
        [wr_hi, wr_lo, jnp.zeros((D_MODEL, 128 - 2 * N_EXPERTS), BF16)], axis=1)
    gfin = norm_final.reshape(1, D_MODEL)
    wg, wu, wd = w_gate_e.astype(MXU), w_up_e.astype(MXU), w_down_e.astype(MXU)
    h = x
    for i in range(depth):
        j = i // 2
        gmix = norm_mix[i].reshape(1, D_MODEL)
        gffn = norm_ffn[i].reshape(1, D_MODEL)
        if i % 2 == 0:
            h1, xext, cls, rank, counts = _layer0_mixer(
                h, gmix, w_in_ab[j].astype(MXU), lower_bounds[i].reshape(1, A_WIDTH),
                hgrn_out_norm[j].reshape(1, A_DK), conv_w[j], w_out_ab[j].astype(MXU), gffn, wrt)
        else:
            h1, xext, cls, rank, counts = _layer1_mixer(
                h, sinks[j].astype(F32), gmix, w_in_c[j].astype(MXU), b_in_c[j].reshape(1, C_IN),
                w_out_c[j].astype(MXU), b_out_c[j].reshape(1, D_MODEL), gffn, wrt)
        h = _ffn_and_ple(
            h1, xext, cls, rank, counts, p, i, wg, wu, wd,
            norm_ple[i].reshape(1, D_MODEL), w_ple_gate[i].astype(MXU), w_ple_proj[i].astype(MXU),
            gfin, final=(i == depth - 1))
    return h
```

```python
import functools

import jax
import jax.numpy as jnp
from jax import lax
from jax.experimental import pallas as pl
from jax.experimental.pallas import tpu as pltpu

F32 = jnp.float32
BF16 = jnp.bfloat16
MXU = jnp.bfloat16
I32 = jnp.int32

D_MODEL = 1024
EPS = 1e-6
A_WIDTH = 512
A_HEADS = 4
A_DK = 128
A_CHUNK = 32
AB_IN = 3584
HEAD_DIM = 64
Q_HEADS = 16
KV_HEADS = 2
GROUP = 8
WINDOW = 128
C_IN = 1280
N_EXPERTS = 16
N_GROUPS = 4
PER_GROUP = 4
D_EXPERT = 512
PLE_DIM = 256

PAIRS = ((0, 1), (0, 2), (0, 3), (1, 2), (1, 3), (2, 3))
N_CLASS = N_GROUPS * len(PAIRS)
CLASS_PAD = 32
ROW_WORDS = D_MODEL + 128

L0_BLOCK = 256
L1_BLOCK = 512
MOE_TILE = 256
MOVE_BLOCK = 512
SCATTER_BLOCK = 2048
VMEM_LIMIT = 56 * 1024 * 1024

NT_DIMS = (((1,), (1,)), ((), ()))
TN_DIMS = (((0,), (0,)), ((), ()))


def _rms(x, g):
    ms = jnp.mean(x * x, axis=-1, keepdims=True)
    return x * lax.rsqrt(ms + EPS) * g


def _sigmoid(x):
    return 1.0 / (1.0 + jnp.exp(-x))


def _route_tail(h1, first, gffn_ref, wr_ref, xext_ref, cls_ref, rank_ref, cnt_ref, run_scr):
    tb = h1.shape[0]

    @pl.when(first)
    def _():
        run_scr[...] = jnp.zeros_like(run_scr)

    xn = _rms(h1, gffn_ref[...])
    x_hi = xn.astype(BF16)
    x_lo = (xn - x_hi.astype(F32)).astype(BF16)
    prod = (jnp.dot(x_hi, wr_ref[...], preferred_element_type=F32)
            + jnp.dot(x_lo, wr_ref[...], preferred_element_type=F32)).T
    logits = prod[0:N_EXPERTS, :] + prod[N_EXPERTS:2 * N_EXPERTS, :]
    mx = jnp.max(logits, axis=0, keepdims=True)
    ex = jnp.exp(logits - mx)
    sc = ex / jnp.sum(ex, axis=0, keepdims=True)
    rows = [sc[i:i + 1, :] for i in range(N_EXPERTS)]

    gscore = []
    for g in range(N_GROUPS):
        v = rows[PER_GROUP * g:PER_GROUP * (g + 1)]
        best = v[0] + v[1]
        for (i, j) in PAIRS[1:]:
            best = jnp.maximum(best, v[i] + v[j])
        gscore.append(best)
    gmax = jnp.maximum(jnp.maximum(gscore[0], gscore[1]), jnp.maximum(gscore[2], gscore[3]))
    sel = jnp.where(gscore[0] >= gmax, 0, jnp.where(gscore[1] >= gmax, 1,
                                                    jnp.where(gscore[2] >= gmax, 2, 3))).astype(I32)
    v = [jnp.where(sel == 0, rows[i], jnp.where(sel == 1, rows[4 + i],
                                                jnp.where(sel == 2, rows[8 + i], rows[12 + i])))
         for i in range(PER_GROUP)]
    chosen = []
    for i in range(PER_GROUP):
        r = jnp.zeros_like(sel)
        for j in range(PER_GROUP):
            if j == i:
                continue
            ahead = (v[j] >= v[i]) if j < i else (v[j] > v[i])
            r = r + jnp.where(ahead, 1, 0).astype(I32)
        chosen.append(r < 2)
    vsum = jnp.zeros_like(v[0])
    code = jnp.zeros_like(sel)
    for i in range(PER_GROUP):
        vsum = vsum + jnp.where(chosen[i], v[i], 0.0)
        code = code + jnp.where(chosen[i], 1 << i, 0).astype(I32)
    wts = [jnp.where(chosen[i], v[i] / vsum, 0.0) for i in range(PER_GROUP)]
    pair = jnp.where(code == 3, 0, jnp.where(code == 5, 1, jnp.where(code == 9, 2,
                     jnp.where(code == 6, 3, jnp.where(code == 10, 4, 5))))).astype(I32)
    cls = sel * len(PAIRS) + pair

    erow = lax.broadcasted_iota(I32, (128, tb), 0)
    comb_t = jnp.zeros((128, tb), F32)
    for i in range(PER_GROUP):
        comb_t = comb_t + jnp.where(erow == sel * PER_GROUP + i, wts[i], 0.0)
    comb = comb_t.T

    xext_ref[:, :D_MODEL] = xn
    xext_ref[:, D_MODEL:] = comb

    crow = lax.broadcasted_iota(I32, (CLASS_PAD, tb), 0)
    onehot = jnp.where(crow == cls, 1.0, 0.0)
    ii = lax.broadcasted_iota(I32, (tb, tb), 0)
    jj = lax.broadcasted_iota(I32, (tb, tb), 1)
    upper = jnp.where(ii < jj, 1.0, 0.0).astype(MXU)
    prefix = jnp.dot(onehot.astype(MXU), upper, preferred_element_type=F32)
    run = run_scr[...]
    rank = jnp.sum(onehot * (prefix + run[:, 0:1]), axis=0, keepdims=True)
    run_new = run + jnp.sum(onehot, axis=1, keepdims=True)
    run_scr[...] = run_new
    cnt_ref[...] = run_new
    cls_ref[0] = cls
    rank_ref[0] = rank.astype(I32)


def _l0_kernel(h_ref, gmix_ref, win_ref, lb_ref, onorm_ref, convw_ref, wout_ref, gffn_ref, wr_ref,
               h1_ref, xext_ref, cls_ref, rank_ref, cnt_ref,
               z_scr, cat_scr, st_scr, carry_scr, run_scr):
    b = pl.program_id(0)
    t = pl.program_id(1)
    tb = h_ref.shape[1]

    @pl.when(t == 0)
    def _():
        st_scr[...] = jnp.zeros_like(st_scr)
        carry_scr[...] = jnp.zeros_like(carry_scr)

    h = h_ref[0]
    u = _rms(h, gmix_ref[...]).astype(MXU)
    z_scr[...] = jnp.dot(u, win_ref[...], preferred_element_type=F32)

    gate_b = z_scr[:, 4 * A_WIDTH:5 * A_WIDTH]
    cb = z_scr[:, 5 * A_WIDTH:6 * A_WIDTH] * z_scr[:, 6 * A_WIDTH:7 * A_WIDTH]
    row = lax.broadcasted_iota(I32, cb.shape, 0)
    prev1 = carry_scr[7:8, :]
    prev2 = carry_scr[6:7, :]
    m1 = jnp.where(row == 0, prev1, pltpu.roll(cb, 1, axis=0))
    m2 = jnp.where(row == 0, prev2, jnp.where(row == 1, prev1, pltpu.roll(cb, 2, axis=0)))
    cw = convw_ref[...]
    cat_scr[:, A_WIDTH:] = gate_b * (cw[0:1, :] * m2 + cw[1:2, :] * m1 + cw[2:3, :] * cb)
    carry_scr[...] = cb[tb - 8:tb, :]

    nc = tb // A_CHUNK
    lb = lb_ref[...]
    onorm = onorm_ref[...]
    zq = z_scr[:, 0:A_WIDTH]
    zf = z_scr[:, A_WIDTH:2 * A_WIDTH]
    zi = z_scr[:, 2 * A_WIDTH:3 * A_WIDTH]
    zg = z_scr[:, 3 * A_WIDTH:4 * A_WIDTH]
    f = lb + (1.0 - lb) * _sigmoid(zf)
    logf = jnp.log(f)
    ri = lax.broadcasted_iota(I32, (tb, tb), 0)
    rj = lax.broadcasted_iota(I32, (tb, tb), 1)
    same_chunk = (ri // A_CHUNK) == (rj // A_CHUNK)
    causal = same_chunk & (ri >= rj)
    tri = jnp.where(causal, 1.0, 0.0).astype(BF16)
    p1 = logf.astype(BF16)
    r1 = logf - p1.astype(F32)
    p2 = r1.astype(BF16)
    p3 = (r1 - p2.astype(F32)).astype(BF16)
    pieces = jnp.dot(tri, jnp.concatenate([p1, p2, p3], axis=1), preferred_element_type=F32)
    bcum = (pieces[:, 0:A_WIDTH] + pieces[:, A_WIDTH:2 * A_WIDTH]) + pieces[:, 2 * A_WIDTH:]
    blast3 = bcum.reshape(nc, A_CHUNK, A_WIDTH)[:, A_CHUNK - 1:A_CHUNK, :]
    blast = jnp.broadcast_to(blast3, (nc, A_CHUNK, A_WIDTH)).reshape(tb, A_WIDTH)
    k = 1.0 - f
    q_dec = ((zq * _sigmoid(zq)) * jnp.exp(bcum)).astype(MXU)
    k_inv = (k * jnp.exp(-bcum)).astype(MXU)
    k_tail = (k * jnp.exp(blast - bcum)).astype(MXU)
    decay = jnp.exp(blast3)
    gate = _sigmoid(zg)
    vals = zi.astype(MXU)
    in_block = (lax.broadcasted_iota(I32, (tb, nc * A_DK), 0) // A_CHUNK
                == lax.broadcasted_iota(I32, (tb, nc * A_DK), 1) // A_DK)
    zero = jnp.zeros((), MXU)
    for hd in range(A_HEADS):
        sl = slice(hd * A_DK, (hd + 1) * A_DK)
        qd = q_dec[:, sl]
        vv = vals[:, sl]
        scores = lax.dot_general(qd, k_inv[:, sl], NT_DIMS, preferred_element_type=F32)
        scores = jnp.where(causal, scores, 0.0).astype(MXU)
        o = jnp.dot(scores, vv, preferred_element_type=F32)
        kt_blocks = jnp.where(in_block, jnp.tile(k_tail[:, sl], (1, nc)), zero)
        kv_all = lax.dot_general(vv, kt_blocks, TN_DIMS, preferred_element_type=F32)
        st = st_scr[hd]
        states = []
        for c in range(nc):
            cs = slice(c * A_DK, (c + 1) * A_DK)
            states.append(st.astype(MXU))
            st = st * decay[c, :, sl] + kv_all[:, cs]
        st_scr[hd] = st
        qd_blocks = jnp.where(in_block, jnp.tile(qd, (1, nc)), zero)
        o = o + lax.dot_general(qd_blocks, jnp.concatenate(states, axis=1), NT_DIMS,
                                preferred_element_type=F32)
        cat_scr[:, sl] = _rms(o, onorm) * gate[:, sl]

    h1 = h + jnp.dot(cat_scr[...].astype(MXU), wout_ref[...], preferred_element_type=F32)
    h1_ref[0] = h1
    _route_tail(h1, (b == 0) & (t == 0), gffn_ref, wr_ref, xext_ref, cls_ref, rank_ref, cnt_ref,
                run_scr)


def _l1_kernel(sinks_ref, h_ref, gmix_ref, wqt_ref, bqt_ref, wk_ref, bk_ref, wvt_ref, bvt_ref,
               wout_ref, bout_ref, gffn_ref, wr_ref,
               h1_ref, xext_ref, cls_ref, rank_ref, cnt_ref,
               qt_scr, k_scr, vt_scr, att_scr, run_scr):
    b = pl.program_id(0)
    t = pl.program_id(1)
    tb = h_ref.shape[1]

    @pl.when(t == 0)
    def _():
        k_scr[:, 0:WINDOW, :] = jnp.zeros((KV_HEADS, WINDOW, HEAD_DIM), k_scr.dtype)
        vt_scr[:, 0:WINDOW] = jnp.zeros((KV_HEADS * HEAD_DIM, WINDOW), vt_scr.dtype)

    h = h_ref[0]
    u = _rms(h, gmix_ref[...]).astype(MXU)
    scale = HEAD_DIM ** -0.5
    qt_scr[...] = ((lax.dot_general(wqt_ref[...], u, NT_DIMS, preferred_element_type=F32)
                    + bqt_ref[...]) * scale).astype(qt_scr.dtype)
    kz = (jnp.dot(u, wk_ref[...], preferred_element_type=F32) + bk_ref[...]).astype(k_scr.dtype)
    for kk in range(KV_HEADS):
        k_scr[kk, WINDOW:, :] = kz[:, kk * HEAD_DIM:(kk + 1) * HEAD_DIM]
    vt_scr[:, WINDOW:] = (lax.dot_general(wvt_ref[...], u, NT_DIMS, preferred_element_type=F32)
                          + bvt_ref[...]).astype(vt_scr.dtype)

    kj = lax.broadcasted_iota(I32, (2 * WINDOW, WINDOW), 0)
    qi = lax.broadcasted_iota(I32, (2 * WINDOW, WINDOW), 1)
    band = (kj > qi) & (kj <= qi + WINDOW)
    first_key = jnp.where(t > 0, 0, WINDOW)

    def all_heads(keep):
        return jnp.tile(jnp.where(keep, 1.0, 0.0), (1, GROUP)) > 0.5

    mask_first = all_heads(band & (kj >= first_key))
    mask_rest = all_heads(band)
    for n in range(tb // WINDOW):
        rs = slice(n * WINDOW, (n + 1) * WINDOW)
        win = slice(n * WINDOW, (n + 2) * WINDOW)
        mask = mask_first if n == 0 else mask_rest
        for kk in range(KV_HEADS):
            heads = range(kk * GROUP, (kk + 1) * GROUP)
            keys = k_scr[kk, win, :]
            vals_t = vt_scr[kk * HEAD_DIM:(kk + 1) * HEAD_DIM, win]
            q_t = jnp.concatenate(
                [qt_scr[hd * HEAD_DIM:(hd + 1) * HEAD_DIM, rs] for hd in heads], axis=1)
            sink = jnp.concatenate(
                [jnp.full((1, WINDOW), sinks_ref[hd], F32) for hd in heads], axis=1)
            s = jnp.dot(keys, q_t, preferred_element_type=F32)
            s = jnp.where(mask, s, -jnp.inf)
            m = jnp.maximum(jnp.max(s, axis=0, keepdims=True), sink)
            e = jnp.exp(s - m)
            denom = jnp.sum(e, axis=0, keepdims=True) + jnp.exp(sink - m)
            o_t = jnp.dot(vals_t, e.astype(MXU), preferred_element_type=F32) / denom
            for g2 in range(GROUP // 2):
                pair = jnp.concatenate([o_t[:, (2 * g2) * WINDOW:(2 * g2 + 1) * WINDOW],
                                        o_t[:, (2 * g2 + 1) * WINDOW:(2 * g2 + 2) * WINDOW]], axis=0)
                h0 = (kk * GROUP + 2 * g2) * HEAD_DIM
                att_scr[rs, h0:h0 + 2 * HEAD_DIM] = pair.T
    k_scr[:, 0:WINDOW, :] = k_scr[:, tb:tb + WINDOW, :]
    vt_scr[:, 0:WINDOW] = vt_scr[:, tb:tb + WINDOW]

    h1 = (h + jnp.dot(att_scr[...].astype(MXU), wout_ref[...], preferred_element_type=F32)
          + bout_ref[...])
    h1_ref[0] = h1
    _route_tail(h1, (b == 0) & (t == 0), gffn_ref, wr_ref, xext_ref, cls_ref, rank_ref, cnt_ref,
                run_scr)


def _row_copy(src, src_row, dst, dst_row, sem):
    return pltpu.make_async_copy(src.at[pl.ds(src_row, 1)], dst.at[pl.ds(dst_row, 1)], sem)


ISSUE_UNROLL = 8
DMA_THREADS = 2


def _scatter_kernel(pos_ref, pad0_ref, pad1_ref, nused_ref, x_ref, out_ref, zero_scr, sem,
                    pad_sem):
    rows = x_ref.shape[0]
    tm = zero_scr.shape[0]
    i = pl.program_id(0)
    base = i * rows

    @pl.when(i == 0)
    def _():
        zero_scr[...] = jnp.zeros_like(zero_scr)
        n_tiles = out_ref.shape[0] // tm

        def tile_copy(tile):
            dst = out_ref.at[pl.ds(pl.multiple_of(tile * tm, tm), tm)]
            return pltpu.make_async_copy(zero_scr, dst, pad_sem)

        def row_copy(r):
            return _row_copy(zero_scr, 0, out_ref, r, pad_sem)

        def for_all(on_tile, on_row):
            def tile_body(tile, carry):
                on_tile(tile)
                return carry

            def row_body(r, carry):
                on_row(r)
                return carry

            def class_body(c, carry):
                return lax.fori_loop(pad0_ref[c], pad1_ref[c], row_body, carry)

            lax.fori_loop(nused_ref[0], n_tiles, tile_body, 0)
            lax.fori_loop(0, N_CLASS, class_body, 0)

        for_all(lambda tile: tile_copy(tile).start(), lambda r: row_copy(r).start())
        for_all(lambda tile: tile_copy(tile).wait(), lambda r: row_copy(r).wait())

    def issue(g, carry):
        r0 = pl.multiple_of(g * ISSUE_UNROLL, ISSUE_UNROLL)
        group = x_ref.at[pl.ds(r0, ISSUE_UNROLL)]
        for k in range(ISSUE_UNROLL):
            _row_copy(group, k, out_ref, pos_ref[base + r0 + k], sem).start()
        return carry

    lax.fori_loop(0, rows // ISSUE_UNROLL, issue, 0)
    pltpu.make_async_copy(x_ref, out_ref.at[pl.ds(0, rows)], sem).wait()


def _moe_kernel(e1_ref, e2_ref, nused_ref,
                x_ref, wg1_ref, wu1_ref, wd1_ref, wg2_ref, wu2_ref, wd2_ref, y_ref):
    i = pl.program_id(0)
    nused = nused_ref[0]

    @pl.when(i < nused)
    def _():
        x = x_ref[:, :D_MODEL].astype(MXU)
        comb = x_ref[:, D_MODEL:]
        lane = lax.broadcasted_iota(I32, comb.shape, 1)

        def expert(e, wg_ref, wu_ref, wd_ref):
            cw = jnp.sum(jnp.where(lane == e, comb, 0.0), axis=1, keepdims=True)
            gate = jnp.dot(x, wg_ref[...], preferred_element_type=F32)
            up = jnp.dot(x, wu_ref[...], preferred_element_type=F32)
            hidden = (gate * _sigmoid(gate)) * up
            return cw * jnp.dot(hidden.astype(MXU), wd_ref[...], preferred_element_type=F32)

        y_ref[...] = (expert(e1_ref[i], wg1_ref, wu1_ref, wd1_ref)
                      + expert(e2_ref[i], wg2_ref, wu2_ref, wd2_ref))

    @pl.when(i >= nused)
    def _():
        y_ref[...] = jnp.zeros_like(y_ref)


def _ple_kernel(pos_ref, h1_ref, p_ref, y_ref, gple_ref, wgate_ref, wproj_ref, gfin_ref,
                out_ref, ybuf, sem, *, final):
    rows = h1_ref.shape[0]
    i = pl.program_id(0)
    nsteps = pl.num_programs(0)
    slot = i % 2

    def gather(step, buf):
        def issue(g, carry):
            r0 = pl.multiple_of(g * ISSUE_UNROLL, ISSUE_UNROLL)
            group = ybuf.at[buf, pl.ds(r0, ISSUE_UNROLL)]
            for k in range(ISSUE_UNROLL):
                _row_copy(y_ref, pos_ref[step * rows + r0 + k], group, k,
                          sem.at[buf]).start(priority=k % DMA_THREADS)
            return carry

        lax.fori_loop(0, rows // ISSUE_UNROLL, issue, 0)

    def wait(buf):
        pltpu.make_async_copy(y_ref.at[pl.ds(0, rows)], ybuf.at[buf], sem.at[buf]).wait()

    @pl.when(i == 0)
    def _():
        gather(0, 0)

    wait(slot)
    h2 = h1_ref[...] + ybuf[slot]
    nxt = jnp.minimum(i + 1, nsteps - 1)
    for r in range(rows):
        group = ybuf.at[1 - slot, pl.ds((r // 8) * 8, 8)]
        _row_copy(y_ref, pos_ref[nxt * rows + r], group, r % 8,
                  sem.at[1 - slot]).start(priority=r % DMA_THREADS)

    proj = jnp.dot(p_ref[...].astype(MXU), wproj_ref[...], preferred_element_type=F32)
    gate = _sigmoid(jnp.dot(_rms(h2, gple_ref[...]).astype(MXU), wgate_ref[...],
                            preferred_element_type=F32))
    h3 = h2 + gate * proj
    if final:
        h3 = _rms(h3, gfin_ref[...])
    out_ref[...] = h3

    @pl.when(i == nsteps - 1)
    def _():
        wait(1 - slot)


def _const_spec(shape):
    nd = len(shape)
    return pl.BlockSpec(shape, lambda *_: (0,) * nd)


def _route_out(bsz, seq, tb):
    nt = seq // tb
    n = bsz * seq
    shapes = [
        jax.ShapeDtypeStruct((bsz, seq, D_MODEL), F32),
        jax.ShapeDtypeStruct((n, ROW_WORDS), F32),
        jax.ShapeDtypeStruct((bsz * nt, 1, tb), I32),
        jax.ShapeDtypeStruct((bsz * nt, 1, tb), I32),
        jax.ShapeDtypeStruct((CLASS_PAD, 128), F32),
    ]
    specs = [
        pl.BlockSpec((1, tb, D_MODEL), lambda b, t, *_: (b, t, 0)),
        pl.BlockSpec((tb, ROW_WORDS), lambda b, t, *_: (b * nt + t, 0)),
        pl.BlockSpec((1, 1, tb), lambda b, t, *_: (b * nt + t, 0, 0)),
        pl.BlockSpec((1, 1, tb), lambda b, t, *_: (b * nt + t, 0, 0)),
        pl.BlockSpec((CLASS_PAD, 128), lambda b, t, *_: (0, 0)),
    ]
    return shapes, specs


def _params():
    return pltpu.CompilerParams(dimension_semantics=("arbitrary", "arbitrary"),
                                vmem_limit_bytes=VMEM_LIMIT)


def _layer0_mixer(h, gmix, win, lb, onorm, convw, wout, gffn, wrt):
    bsz, seq, _ = h.shape
    tb = L0_BLOCK
    assert seq % tb == 0
    shapes, ospecs = _route_out(bsz, seq, tb)
    return pl.pallas_call(
        _l0_kernel,
        grid=(bsz, seq // tb),
        in_specs=[
            pl.BlockSpec((1, tb, D_MODEL), lambda b, t: (b, t, 0)),
            _const_spec((1, D_MODEL)),
            _const_spec((D_MODEL, AB_IN)),
            _const_spec((1, A_WIDTH)),
            _const_spec((1, A_DK)),
            _const_spec((3, A_WIDTH)),
            _const_spec((D_MODEL, D_MODEL)),
            _const_spec((1, D_MODEL)),
            _const_spec((D_MODEL, 128)),
        ],
        out_specs=ospecs,
        out_shape=shapes,
        scratch_shapes=[
            pltpu.VMEM((tb, AB_IN), F32),
            pltpu.VMEM((tb, D_MODEL), F32),
            pltpu.VMEM((A_HEADS, A_DK, A_DK), F32),
            pltpu.VMEM((8, A_WIDTH), F32),
            pltpu.VMEM((CLASS_PAD, 128), F32),
        ],
        compiler_params=_params(),
        name="layer0_mixer",
    )(h, gmix, win, lb, onorm, convw, wout, gffn, wrt)


def _layer1_mixer(h, sinks, gmix, win, bin_, wout, bout, gffn, wrt):
    bsz, seq, _ = h.shape
    tb = L1_BLOCK
    assert seq % tb == 0
    shapes, ospecs = _route_out(bsz, seq, tb)
    qw = Q_HEADS * HEAD_DIM
    vw = KV_HEADS * HEAD_DIM
    wqt, wk, wvt = win[:, :qw].T, win[:, qw:qw + vw], win[:, qw + vw:].T
    bqt, bk, bvt = bin_[:, :qw].reshape(qw, 1), bin_[:, qw:qw + vw], bin_[:, qw + vw:].reshape(vw, 1)
    grid_spec = pltpu.PrefetchScalarGridSpec(
        num_scalar_prefetch=1,
        grid=(bsz, seq // tb),
        in_specs=[
            pl.BlockSpec((1, tb, D_MODEL), lambda b, t, *_: (b, t, 0)),
            _const_spec((1, D_MODEL)),
            _const_spec((qw, D_MODEL)),
            _const_spec((qw, 1)),
            _const_spec((D_MODEL, vw)),
            _const_spec((1, vw)),
            _const_spec((vw, D_MODEL)),
            _const_spec((vw, 1)),
            _const_spec((D_MODEL, D_MODEL)),
            _const_spec((1, D_MODEL)),
            _const_spec((1, D_MODEL)),
            _const_spec((D_MODEL, 128)),
        ],
        out_specs=ospecs,
        scratch_shapes=[
            pltpu.VMEM((qw, tb), MXU),
            pltpu.VMEM((KV_HEADS, tb + WINDOW, HEAD_DIM), MXU),
            pltpu.VMEM((vw, tb + WINDOW), MXU),
            pltpu.VMEM((tb, D_MODEL), F32),
            pltpu.VMEM((CLASS_PAD, 128), F32),
        ],
    )
    return pl.pallas_call(
        _l1_kernel, grid_spec=grid_spec, out_shape=shapes, compiler_params=_params(),
        name="layer1_mixer",
    )(sinks, h, gmix, wqt, bqt, wk, bk, wvt, bvt, wout, bout, gffn, wrt)


def _dispatch(plan, xext):
    pos, _, _, nused, pad0, pad1, n_rows = plan
    n = xext.shape[0]
    blk = min(SCATTER_BLOCK, n)
    assert n % blk == 0
    grid_spec = pltpu.PrefetchScalarGridSpec(
        num_scalar_prefetch=4,
        grid=(n // blk,),
        in_specs=[pl.BlockSpec((blk, ROW_WORDS), lambda i, *_: (i, 0))],
        out_specs=pl.BlockSpec(memory_space=pl.ANY),
        scratch_shapes=[pltpu.VMEM((MOE_TILE, ROW_WORDS), F32), pltpu.SemaphoreType.DMA(()),
                        pltpu.SemaphoreType.DMA(())],
    )
    return pl.pallas_call(
        _scatter_kernel, grid_spec=grid_spec,
        out_shape=jax.ShapeDtypeStruct((n_rows, ROW_WORDS), F32),
        compiler_params=pltpu.CompilerParams(dimension_semantics=("arbitrary",),
                                             vmem_limit_bytes=VMEM_LIMIT),
        name="moe_dispatch",
    )(pos, pad0, pad1, nused, xext)


def _moe(plan, xs, layer, wg, wu, wd):
    _, e1, e2, nused, _, _, n_rows = plan
    tm = MOE_TILE
    n_tiles = n_rows // tm

    def wspec(shape, which):
        return pl.BlockSpec((None, None) + shape,
                            lambda i, e1, e2, *_: (layer, (e1, e2)[which][i], 0, 0))

    grid_spec = pltpu.PrefetchScalarGridSpec(
        num_scalar_prefetch=3,
        grid=(n_tiles,),
        in_specs=[
            pl.BlockSpec((tm, ROW_WORDS), lambda i, *_: (i, 0)),
            wspec((D_MODEL, D_EXPERT), 0), wspec((D_MODEL, D_EXPERT), 0), wspec((D_EXPERT, D_MODEL), 0),
            wspec((D_MODEL, D_EXPERT), 1), wspec((D_MODEL, D_EXPERT), 1), wspec((D_EXPERT, D_MODEL), 1),
        ],
        out_specs=pl.BlockSpec((tm, D_MODEL), lambda i, *_: (i, 0)),
    )
    return pl.pallas_call(
        _moe_kernel, grid_spec=grid_spec,
        out_shape=jax.ShapeDtypeStruct((n_rows, D_MODEL), F32),
        compiler_params=pltpu.CompilerParams(dimension_semantics=("arbitrary",),
                                             vmem_limit_bytes=VMEM_LIMIT),
        name="moe_experts",
    )(e1, e2, nused, xs, wg, wu, wd, wg, wu, wd)


def _combine_ple(pos, h1, p, layer, ys, gple, wgate, wproj, gfin, final):
    n = h1.shape[0]
    blk = MOVE_BLOCK
    assert n % blk == 0
    grid_spec = pltpu.PrefetchScalarGridSpec(
        num_scalar_prefetch=1,
        grid=(n // blk,),
        in_specs=[
            pl.BlockSpec((blk, D_MODEL), lambda i, *_: (i, 0)),
            pl.BlockSpec((None, blk, PLE_DIM), lambda i, *_: (layer, i, 0)),
            pl.BlockSpec(memory_space=pl.ANY),
            _const_spec((1, D_MODEL)),
            _const_spec((D_MODEL, D_MODEL)),
            _const_spec((PLE_DIM, D_MODEL)),
            _const_spec((1, D_MODEL)),
        ],
        out_specs=pl.BlockSpec((blk, D_MODEL), lambda i, *_: (i, 0)),
        scratch_shapes=[pltpu.VMEM((2, blk, D_MODEL), F32), pltpu.SemaphoreType.DMA((2,))],
    )
    return pl.pallas_call(
        functools.partial(_ple_kernel, final=final), grid_spec=grid_spec,
        out_shape=jax.ShapeDtypeStruct((n, D_MODEL), F32),
        compiler_params=pltpu.CompilerParams(dimension_semantics=("arbitrary",),
                                             vmem_limit_bytes=VMEM_LIMIT),
        name="combine_ple",
    )(pos, h1, p, ys, gple, wgate, wproj, gfin)


def _routing_plan(cls, rank, counts, n_tokens):
    tm = MOE_TILE
    n_tiles = n_tokens // tm + N_CLASS
    cnt = counts[:N_CLASS, 0].astype(I32)
    tiles_per_class = (cnt + tm - 1) // tm
    tile_end = jnp.cumsum(tiles_per_class)
    class_row0 = (tile_end - tiles_per_class) * tm
    pos = jnp.take(class_row0, cls.reshape(-1)) + rank.reshape(-1)
    nused = tile_end[-1]
    tidx = jnp.minimum(jnp.arange(n_tiles, dtype=I32), nused - 1)
    tcls = jnp.sum((tidx[:, None] >= tile_end[None, :]).astype(I32), axis=1)
    tcls = jnp.minimum(tcls, N_CLASS - 1)
    lo = jnp.array([a for a, _ in PAIRS], I32)
    hi = jnp.array([b for _, b in PAIRS], I32)
    grp = tcls // len(PAIRS)
    e1 = grp * PER_GROUP + jnp.take(lo, tcls % len(PAIRS))
    e2 = grp * PER_GROUP + jnp.take(hi, tcls % len(PAIRS))
    pad0 = class_row0 + cnt
    pad1 = tile_end * tm
    return (pos.astype(I32), e1.astype(I32), e2.astype(I32), nused.reshape(1).astype(I32),
            pad0.astype(I32), pad1.astype(I32), n_tiles * tm)


def _ffn_and_ple(h1, xext, cls, rank, counts, p, layer, wg, wu, wd, gple, wgate, wproj, gfin,
                 final):
    bsz, seq, _ = h1.shape
    n = bsz * seq
    plan = _routing_plan(cls, rank, counts, n)
    ys = _moe(plan, _dispatch(plan, xext), layer, wg, wu, wd)
    out = _combine_ple(plan[0], h1.reshape(n, D_MODEL), p.reshape(p.shape[0], n, PLE_DIM), layer,
                       ys, gple, wgate, wproj, gfin, final)
    return out.reshape(bsz, seq, D_MODEL)


def kernel(x, p, norm_mix, norm_ffn, norm_ple, norm_final, w_in_ab, hgrn_lb_logits, hgrn_out_norm, conv_w, w_out_ab, w_in_c, b_in_c, sinks, w_out_c, b_out_c, w_router, w_gate_e, w_up_e, w_down_e, w_ple_gate, w_ple_proj):
    depth = p.shape[0]
    lower_bounds = jnp.cumsum(jax.nn.softmax(hgrn_lb_logits.astype(F32), axis=0), axis=0)
    wr32 = w_router.astype(F32)
    wr_hi = wr32.astype(BF16)
    wr_lo = (wr32 - wr_hi.astype(F32)).astype(BF16)
    wrt = jnp.concatenate(
        [wr_hi, wr_lo, jnp.zeros((D_MODEL, 128 - 2 * N_EXPERTS), BF16)], axis=1)
    gfin = norm_final.reshape(1, D_MODEL)
    wg, wu, wd = w_gate_e.astype(MXU), w_up_e.astype(MXU), w_down_e.astype(MXU)
    h = x
    for i in range(depth):
        j = i // 2
        gmix = norm_mix[i].reshape(1, D_MODEL)
        gffn = norm_ffn[i].reshape(1, D_MODEL)
        if i % 2 == 0:
            h1, xext, cls, rank, counts = _layer0_mixer(
                h, gmix, w_in_ab[j].astype(MXU), lower_bounds[i].reshape(1, A_WIDTH),
                hgrn_out_norm[j].reshape(1, A_DK), conv_w[j], w_out_ab[j].astype(MXU), gffn, wrt)
        else:
            h1, xext, cls, rank, counts = _layer1_mixer(
                h, sinks[j].astype(F32), gmix, w_in_c[j].astype(MXU), b_in_c[j].reshape(1, C_IN),
                w_out_c[j].astype(MXU), b_out_c[j].reshape(1, D_MODEL), gffn, wrt)
        h = _ffn_and_ple(
            h1, xext, cls, rank, counts, p, i, wg, wu, wd,
            norm_ple[i].reshape(1, D_MODEL), w_ple_gate[i].astype(MXU), w_ple_proj[i].astype(MXU),
            gfin, final=(i == depth - 1))
    return h
```

```python
import functools

import jax
import jax.numpy as jnp
from jax import lax
from jax.experimental import pallas as pl
from jax.experimental.pallas import tpu as pltpu

F32 = jnp.float32
BF16 = jnp.bfloat16
MXU = jnp.bfloat16
I32 = jnp.int32

D_MODEL = 1024
EPS = 1e-6
A_WIDTH = 512
A_HEADS = 4
A_DK = 128
A_CHUNK = 32
AB_IN = 3584
HEAD_DIM = 64
Q_HEADS = 16
KV_HEADS = 2
GROUP = 8
WINDOW = 128
C_IN = 1280
N_EXPERTS = 16
N_GROUPS = 4
PER_GROUP = 4
D_EXPERT = 512
PLE_DIM = 256

PAIRS = ((0, 1), (0, 2), (0, 3), (1, 2), (1, 3), (2, 3))
N_CLASS = N_GROUPS * len(PAIRS)
CLASS_PAD = 32
ROW_WORDS = D_MODEL + 128

L0_BLOCK = 256
L1_BLOCK = 512
MOE_TILE = 256
MOVE_BLOCK = 512
SCATTER_BLOCK = 4096
VMEM_LIMIT = 56 * 1024 * 1024

NT_DIMS = (((1,), (1,)), ((), ()))
TN_DIMS = (((0,), (0,)), ((), ()))


def _rms(x, g):
    ms = jnp.mean(x * x, axis=-1, keepdims=True)
    return x * lax.rsqrt(ms + EPS) * g


def _sigmoid(x):
    return 1.0 / (1.0 + jnp.exp(-x))


def _route_tail(h1, first, gffn_ref, wr_ref, xext_ref, cls_ref, rank_ref, cnt_ref, run_scr):
    tb = h1.shape[0]

    @pl.when(first)
    def _():
        run_scr[...] = jnp.zeros_like(run_scr)

    xn = _rms(h1, gffn_ref[...])
    x_hi = xn.astype(BF16)
    x_lo = (xn - x_hi.astype(F32)).astype(BF16)
    prod = (jnp.dot(x_hi, wr_ref[...], preferred_element_type=F32)
            + jnp.dot(x_lo, wr_ref[...], preferred_element_type=F32)).T
    logits = prod[0:N_EXPERTS, :] + prod[N_EXPERTS:2 * N_EXPERTS, :]
    mx = jnp.max(logits, axis=0, keepdims=True)
    ex = jnp.exp(logits - mx)
    sc = ex / jnp.sum(ex, axis=0, keepdims=True)
    rows = [sc[i:i + 1, :] for i in range(N_EXPERTS)]

    gscore = []
    for g in range(N_GROUPS):
        v = rows[PER_GROUP * g:PER_GROUP * (g + 1)]
        best = v[0] + v[1]
        for (i, j) in PAIRS[1:]:
            best = jnp.maximum(best, v[i] + v[j])
        gscore.append(best)
    gmax = jnp.maximum(jnp.maximum(gscore[0], gscore[1]), jnp.maximum(gscore[2], gscore[3]))
    sel = jnp.where(gscore[0] >= gmax, 0, jnp.where(gscore[1] >= gmax, 1,
                                                    jnp.where(gscore[2] >= gmax, 2, 3))).astype(I32)
    v = [jnp.where(sel == 0, rows[i], jnp.where(sel == 1, rows[4 + i],
                                                jnp.where(sel == 2, rows[8 + i], rows[12 + i])))
         for i in range(PER_GROUP)]
    chosen = []
    for i in range(PER_GROUP):
        r = jnp.zeros_like(sel)
        for j in range(PER_GROUP):
            if j == i:
                continue
            ahead = (v[j] >= v[i]) if j < i else (v[j] > v[i])
            r = r + jnp.where(ahead, 1, 0).astype(I32)
        chosen.append(r < 2)
    vsum = jnp.zeros_like(v[0])
    code = jnp.zeros_like(sel)
    for i in range(PER_GROUP):
        vsum = vsum + jnp.where(chosen[i], v[i], 0.0)
        code = code + jnp.where(chosen[i], 1 << i, 0).astype(I32)
    wts = [jnp.where(chosen[i], v[i] / vsum, 0.0) for i in range(PER_GROUP)]
    pair = jnp.where(code == 3, 0, jnp.where(code == 5, 1, jnp.where(code == 9, 2,
                     jnp.where(code == 6, 3, jnp.where(code == 10, 4, 5))))).astype(I32)
    cls = sel * len(PAIRS) + pair

    erow = lax.broadcasted_iota(I32, (128, tb), 0)
    comb_t = jnp.zeros((128, tb), F32)
    for i in range(PER_GROUP):
        comb_t = comb_t + jnp.where(erow == sel * PER_GROUP + i, wts[i], 0.0)
    comb = comb_t.T

    xext_ref[:, :D_MODEL] = xn
    xext_ref[:, D_MODEL:] = comb

    crow = lax.broadcasted_iota(I32, (CLASS_PAD, tb), 0)
    onehot = jnp.where(crow == cls, 1.0, 0.0)
    ii = lax.broadcasted_iota(I32, (tb, tb), 0)
    jj = lax.broadcasted_iota(I32, (tb, tb), 1)
    upper = jnp.where(ii < jj, 1.0, 0.0).astype(MXU)
    prefix = jnp.dot(onehot.astype(MXU), upper, preferred_element_type=F32)
    run = run_scr[...]
    rank = jnp.sum(onehot * (prefix + run[:, 0:1]), axis=0, keepdims=True)
    run_new = run + jnp.sum(onehot, axis=1, keepdims=True)
    run_scr[...] = run_new
    cnt_ref[...] = run_new
    cls_ref[0] = cls
    rank_ref[0] = rank.astype(I32)


def _l0_kernel(h_ref, gmix_ref, win_ref, lb_ref, onorm_ref, convw_ref, wout_ref, gffn_ref, wr_ref,
               h1_ref, xext_ref, cls_ref, rank_ref, cnt_ref,
               z_scr, cat_scr, st_scr, carry_scr, run_scr):
    b = pl.program_id(0)
    t = pl.program_id(1)
    tb = h_ref.shape[1]

    @pl.when(t == 0)
    def _():
        st_scr[...] = jnp.zeros_like(st_scr)
        carry_scr[...] = jnp.zeros_like(carry_scr)

    h = h_ref[0]
    u = _rms(h, gmix_ref[...]).astype(MXU)
    z_scr[...] = jnp.dot(u, win_ref[...], preferred_element_type=F32)

    gate_b = z_scr[:, 4 * A_WIDTH:5 * A_WIDTH]
    cb = z_scr[:, 5 * A_WIDTH:6 * A_WIDTH] * z_scr[:, 6 * A_WIDTH:7 * A_WIDTH]
    row = lax.broadcasted_iota(I32, cb.shape, 0)
    prev1 = carry_scr[7:8, :]
    prev2 = carry_scr[6:7, :]
    m1 = jnp.where(row == 0, prev1, pltpu.roll(cb, 1, axis=0))
    m2 = jnp.where(row == 0, prev2, jnp.where(row == 1, prev1, pltpu.roll(cb, 2, axis=0)))
    cw = convw_ref[...]
    cat_scr[:, A_WIDTH:] = gate_b * (cw[0:1, :] * m2 + cw[1:2, :] * m1 + cw[2:3, :] * cb)
    carry_scr[...] = cb[tb - 8:tb, :]

    nc = tb // A_CHUNK
    lb = lb_ref[...]
    onorm = onorm_ref[...]
    zq = z_scr[:, 0:A_WIDTH]
    zf = z_scr[:, A_WIDTH:2 * A_WIDTH]
    zi = z_scr[:, 2 * A_WIDTH:3 * A_WIDTH]
    zg = z_scr[:, 3 * A_WIDTH:4 * A_WIDTH]
    f = lb + (1.0 - lb) * _sigmoid(zf)
    logf = jnp.log(f)
    ri = lax.broadcasted_iota(I32, (tb, tb), 0)
    rj = lax.broadcasted_iota(I32, (tb, tb), 1)
    same_chunk = (ri // A_CHUNK) == (rj // A_CHUNK)
    causal = same_chunk & (ri >= rj)
    tri = jnp.where(causal, 1.0, 0.0).astype(BF16)
    p1 = logf.astype(BF16)
    r1 = logf - p1.astype(F32)
    p2 = r1.astype(BF16)
    p3 = (r1 - p2.astype(F32)).astype(BF16)
    pieces = jnp.dot(tri, jnp.concatenate([p1, p2, p3], axis=1), preferred_element_type=F32)
    bcum = (pieces[:, 0:A_WIDTH] + pieces[:, A_WIDTH:2 * A_WIDTH]) + pieces[:, 2 * A_WIDTH:]
    blast3 = bcum.reshape(nc, A_CHUNK, A_WIDTH)[:, A_CHUNK - 1:A_CHUNK, :]
    blast = jnp.broadcast_to(blast3, (nc, A_CHUNK, A_WIDTH)).reshape(tb, A_WIDTH)
    k = 1.0 - f
    q_dec = ((zq * _sigmoid(zq)) * jnp.exp(bcum)).astype(MXU)
    k_inv = (k * jnp.exp(-bcum)).astype(MXU)
    k_tail = (k * jnp.exp(blast - bcum)).astype(MXU)
    decay = jnp.exp(blast3)
    gate = _sigmoid(zg)
    vals = zi.astype(MXU)
    in_block = (lax.broadcasted_iota(I32, (tb, nc * A_DK), 0) // A_CHUNK
                == lax.broadcasted_iota(I32, (tb, nc * A_DK), 1) // A_DK)
    zero = jnp.zeros((), MXU)
    for hd in range(A_HEADS):
        sl = slice(hd * A_DK, (hd + 1) * A_DK)
        qd = q_dec[:, sl]
        vv = vals[:, sl]
        scores = lax.dot_general(qd, k_inv[:, sl], NT_DIMS, preferred_element_type=F32)
        scores = jnp.where(causal, scores, 0.0).astype(MXU)
        o = jnp.dot(scores, vv, preferred_element_type=F32)
        kt_blocks = jnp.where(in_block, jnp.tile(k_tail[:, sl], (1, nc)), zero)
        kv_all = lax.dot_general(vv, kt_blocks, TN_DIMS, preferred_element_type=F32)
        st = st_scr[hd]
        states = []
        for c in range(nc):
            cs = slice(c * A_DK, (c + 1) * A_DK)
            states.append(st.astype(MXU))
            st = st * decay[c, :, sl] + kv_all[:, cs]
        st_scr[hd] = st
        qd_blocks = jnp.where(in_block, jnp.tile(qd, (1, nc)), zero)
        o = o + lax.dot_general(qd_blocks, jnp.concatenate(states, axis=1), NT_DIMS,
                                preferred_element_type=F32)
        cat_scr[:, sl] = _rms(o, onorm) * gate[:, sl]

    h1 = h + jnp.dot(cat_scr[...].astype(MXU), wout_ref[...], preferred_element_type=F32)
    h1_ref[0] = h1
    _route_tail(h1, (b == 0) & (t == 0), gffn_ref, wr_ref, xext_ref, cls_ref, rank_ref, cnt_ref,
                run_scr)


def _l1_kernel(sinks_ref, h_ref, gmix_ref, wqt_ref, bqt_ref, wk_ref, bk_ref, wvt_ref, bvt_ref,
               wout_ref, bout_ref, gffn_ref, wr_ref,
               h1_ref, xext_ref, cls_ref, rank_ref, cnt_ref,
               qt_scr, k_scr, vt_scr, att_scr, run_scr):
    b = pl.program_id(0)
    t = pl.program_id(1)
    tb = h_ref.shape[1]

    @pl.when(t == 0)
    def _():
        k_scr[:, 0:WINDOW, :] = jnp.zeros((KV_HEADS, WINDOW, HEAD_DIM), k_scr.dtype)
        vt_scr[:, 0:WINDOW] = jnp.zeros((KV_HEADS * HEAD_DIM, WINDOW), vt_scr.dtype)

    h = h_ref[0]
    u = _rms(h, gmix_ref[...]).astype(MXU)
    scale = HEAD_DIM ** -0.5
    qt_scr[...] = ((lax.dot_general(wqt_ref[...], u, NT_DIMS, preferred_element_type=F32)
                    + bqt_ref[...]) * scale).astype(qt_scr.dtype)
    kz = (jnp.dot(u, wk_ref[...], preferred_element_type=F32) + bk_ref[...]).astype(k_scr.dtype)
    for kk in range(KV_HEADS):
        k_scr[kk, WINDOW:, :] = kz[:, kk * HEAD_DIM:(kk + 1) * HEAD_DIM]
    vt_scr[:, WINDOW:] = (lax.dot_general(wvt_ref[...], u, NT_DIMS, preferred_element_type=F32)
                          + bvt_ref[...]).astype(vt_scr.dtype)

    kj = lax.broadcasted_iota(I32, (2 * WINDOW, WINDOW), 0)
    qi = lax.broadcasted_iota(I32, (2 * WINDOW, WINDOW), 1)
    band = (kj > qi) & (kj <= qi + WINDOW)
    first_key = jnp.where(t > 0, 0, WINDOW)

    def all_heads(keep):
        return jnp.tile(jnp.where(keep, 1.0, 0.0), (1, GROUP)) > 0.5

    mask_first = all_heads(band & (kj >= first_key))
    mask_rest = all_heads(band)
    for n in range(tb // WINDOW):
        rs = slice(n * WINDOW, (n + 1) * WINDOW)
        win = slice(n * WINDOW, (n + 2) * WINDOW)
        mask = mask_first if n == 0 else mask_rest
        for kk in range(KV_HEADS):
            heads = range(kk * GROUP, (kk + 1) * GROUP)
            keys = k_scr[kk, win, :]
            vals_t = vt_scr[kk * HEAD_DIM:(kk + 1) * HEAD_DIM, win]
            q_t = jnp.concatenate(
                [qt_scr[hd * HEAD_DIM:(hd + 1) * HEAD_DIM, rs] for hd in heads], axis=1)
            sink = jnp.concatenate(
                [jnp.full((1, WINDOW), sinks_ref[hd], F32) for hd in heads], axis=1)
            s = jnp.dot(keys, q_t, preferred_element_type=F32)
            s = jnp.where(mask, s, -jnp.inf)
            m = jnp.maximum(jnp.max(s, axis=0, keepdims=True), sink)
            e = jnp.exp(s - m)
            denom = jnp.sum(e, axis=0, keepdims=True) + jnp.exp(sink - m)
            o_t = jnp.dot(vals_t, e.astype(MXU), preferred_element_type=F32) / denom
            for g2 in range(GROUP // 2):
                pair = jnp.concatenate([o_t[:, (2 * g2) * WINDOW:(2 * g2 + 1) * WINDOW],
                                        o_t[:, (2 * g2 + 1) * WINDOW:(2 * g2 + 2) * WINDOW]], axis=0)
                h0 = (kk * GROUP + 2 * g2) * HEAD_DIM
                att_scr[rs, h0:h0 + 2 * HEAD_DIM] = pair.T
    k_scr[:, 0:WINDOW, :] = k_scr[:, tb:tb + WINDOW, :]
    vt_scr[:, 0:WINDOW] = vt_scr[:, tb:tb + WINDOW]

    h1 = (h + jnp.dot(att_scr[...].astype(MXU), wout_ref[...], preferred_element_type=F32)
          + bout_ref[...])
    h1_ref[0] = h1
    _route_tail(h1, (b == 0) & (t == 0), gffn_ref, wr_ref, xext_ref, cls_ref, rank_ref, cnt_ref,
                run_scr)


def _row_copy(src, src_row, dst, dst_row, sem):
    return pltpu.make_async_copy(src.at[pl.ds(src_row, 1)], dst.at[pl.ds(dst_row, 1)], sem)


ISSUE_UNROLL = 8
DMA_THREADS = 2


def _scatter_kernel(pos_ref, pad0_ref, pad1_ref, nused_ref, x_ref, out_ref, zero_scr, sem,
                    pad_sem):
    rows = x_ref.shape[0]
    tm = zero_scr.shape[0]
    i = pl.program_id(0)
    base = i * rows

    @pl.when(i == 0)
    def _():
        zero_scr[...] = jnp.zeros_like(zero_scr)
        n_tiles = out_ref.shape[0] // tm

        def tile_copy(tile):
            dst = out_ref.at[pl.ds(pl.multiple_of(tile * tm, tm), tm)]
            return pltpu.make_async_copy(zero_scr, dst, pad_sem)

        def row_copy(r):
            return _row_copy(zero_scr, 0, out_ref, r, pad_sem)

        def for_all(on_tile, on_row):
            def tile_body(tile, carry):
                on_tile(tile)
                return carry

            def row_body(r, carry):
                on_row(r)
                return carry

            def class_body(c, carry):
                return lax.fori_loop(pad0_ref[c], pad1_ref[c], row_body, carry)

            lax.fori_loop(nused_ref[0], n_tiles, tile_body, 0)
            lax.fori_loop(0, N_CLASS, class_body, 0)

        for_all(lambda tile: tile_copy(tile).start(), lambda r: row_copy(r).start())
        for_all(lambda tile: tile_copy(tile).wait(), lambda r: row_copy(r).wait())

    def issue(g, carry):
        r0 = pl.multiple_of(g * ISSUE_UNROLL, ISSUE_UNROLL)
        group = x_ref.at[pl.ds(r0, ISSUE_UNROLL)]
        for k in range(ISSUE_UNROLL):
            _row_copy(group, k, out_ref, pos_ref[base + r0 + k], sem).start()
        return carry

    lax.fori_loop(0, rows // ISSUE_UNROLL, issue, 0)
    pltpu.make_async_copy(x_ref, out_ref.at[pl.ds(0, rows)], sem).wait()


def _moe_kernel(e1_ref, e2_ref, nused_ref,
                x_ref, wg1_ref, wu1_ref, wd1_ref, wg2_ref, wu2_ref, wd2_ref, y_ref):
    i = pl.program_id(0)
    nused = nused_ref[0]

    @pl.when(i < nused)
    def _():
        x = x_ref[:, :D_MODEL].astype(MXU)
        comb = x_ref[:, D_MODEL:]
        lane = lax.broadcasted_iota(I32, comb.shape, 1)

        def expert(e, wg_ref, wu_ref, wd_ref):
            cw = jnp.sum(jnp.where(lane == e, comb, 0.0), axis=1, keepdims=True)
            gate = jnp.dot(x, wg_ref[...], preferred_element_type=F32)
            up = jnp.dot(x, wu_ref[...], preferred_element_type=F32)
            hidden = (gate * _sigmoid(gate)) * up
            return cw * jnp.dot(hidden.astype(MXU), wd_ref[...], preferred_element_type=F32)

        y_ref[...] = (expert(e1_ref[i], wg1_ref, wu1_ref, wd1_ref)
                      + expert(e2_ref[i], wg2_ref, wu2_ref, wd2_ref))

    @pl.when(i >= nused)
    def _():
        y_ref[...] = jnp.zeros_like(y_ref)


def _ple_kernel(pos_ref, h1_ref, p_ref, y_ref, gple_ref, wgate_ref, wproj_ref, gfin_ref,
                out_ref, ybuf, sem, *, final):
    rows = h1_ref.shape[0]
    i = pl.program_id(0)
    nsteps = pl.num_programs(0)
    slot = i % 2

    def gather(step, buf):
        def issue(g, carry):
            r0 = pl.multiple_of(g * ISSUE_UNROLL, ISSUE_UNROLL)
            group = ybuf.at[buf, pl.ds(r0, ISSUE_UNROLL)]
            for k in range(ISSUE_UNROLL):
                _row_copy(y_ref, pos_ref[step * rows + r0 + k], group, k,
                          sem.at[buf]).start(priority=k % DMA_THREADS)
            return carry

        lax.fori_loop(0, rows // ISSUE_UNROLL, issue, 0)

    def wait(buf):
        pltpu.make_async_copy(y_ref.at[pl.ds(0, rows)], ybuf.at[buf], sem.at[buf]).wait()

    @pl.when(i == 0)
    def _():
        gather(0, 0)

    wait(slot)
    h2 = h1_ref[...] + ybuf[slot]
    nxt = jnp.minimum(i + 1, nsteps - 1)
    for r in range(rows):
        group = ybuf.at[1 - slot, pl.ds((r // 8) * 8, 8)]
        _row_copy(y_ref, pos_ref[nxt * rows + r], group, r % 8,
                  sem.at[1 - slot]).start(priority=r % DMA_THREADS)

    proj = jnp.dot(p_ref[...].astype(MXU), wproj_ref[...], preferred_element_type=F32)
    gate = _sigmoid(jnp.dot(_rms(h2, gple_ref[...]).astype(MXU), wgate_ref[...],
                            preferred_element_type=F32))
    h3 = h2 + gate * proj
    if final:
        h3 = _rms(h3, gfin_ref[...])
    out_ref[...] = h3

    @pl.when(i == nsteps - 1)
    def _():
        wait(1 - slot)


def _const_spec(shape):
    nd = len(shape)
    return pl.BlockSpec(shape, lambda *_: (0,) * nd)


def _route_out(bsz, seq, tb):
    nt = seq // tb
    n = bsz * seq
    shapes = [
        jax.ShapeDtypeStruct((bsz, seq, D_MODEL), F32),
        jax.ShapeDtypeStruct((n, ROW_WORDS), F32),
        jax.ShapeDtypeStruct((bsz * nt, 1, tb), I32),
        jax.ShapeDtypeStruct((bsz * nt, 1, tb), I32),
        jax.ShapeDtypeStruct((CLASS_PAD, 128), F32),
    ]
    specs = [
        pl.BlockSpec((1, tb, D_MODEL), lambda b, t, *_: (b, t, 0)),
        pl.BlockSpec((tb, ROW_WORDS), lambda b, t, *_: (b * nt + t, 0)),
        pl.BlockSpec((1, 1, tb), lambda b, t, *_: (b * nt + t, 0, 0)),
        pl.BlockSpec((1, 1, tb), lambda b, t, *_: (b * nt + t, 0, 0)),
        pl.BlockSpec((CLASS_PAD, 128), lambda b, t, *_: (0, 0)),
    ]
    return shapes, specs


def _params():
    return pltpu.CompilerParams(dimension_semantics=("arbitrary", "arbitrary"),
                                vmem_limit_bytes=VMEM_LIMIT)


def _layer0_mixer(h, gmix, win, lb, onorm, convw, wout, gffn, wrt):
    bsz, seq, _ = h.shape
    tb = L0_BLOCK
    assert seq % tb == 0
    shapes, ospecs = _route_out(bsz, seq, tb)
    return pl.pallas_call(
        _l0_kernel,
        grid=(bsz, seq // tb),
        in_specs=[
            pl.BlockSpec((1, tb, D_MODEL), lambda b, t: (b, t, 0)),
            _const_spec((1, D_MODEL)),
            _const_spec((D_MODEL, AB_IN)),
            _const_spec((1, A_WIDTH)),
            _const_spec((1, A_DK)),
            _const_spec((3, A_WIDTH)),
            _const_spec((D_MODEL, D_MODEL)),
            _const_spec((1, D_MODEL)),
            _const_spec((D_MODEL, 128)),
        ],
        out_specs=ospecs,
        out_shape=shapes,
        scratch_shapes=[
            pltpu.VMEM((tb, AB_IN), F32),
            pltpu.VMEM((tb, D_MODEL), F32),
            pltpu.VMEM((A_HEADS, A_DK, A_DK), F32),
            pltpu.VMEM((8, A_WIDTH), F32),
            pltpu.VMEM((CLASS_PAD, 128), F32),
        ],
        compiler_params=_params(),
        name="layer0_mixer",
    )(h, gmix, win, lb, onorm, convw, wout, gffn, wrt)


def _layer1_mixer(h, sinks, gmix, win, bin_, wout, bout, gffn, wrt):
    bsz, seq, _ = h.shape
    tb = L1_BLOCK
    assert seq % tb == 0
    shapes, ospecs = _route_out(bsz, seq, tb)
    qw = Q_HEADS * HEAD_DIM
    vw = KV_HEADS * HEAD_DIM
    wqt, wk, wvt = win[:, :qw].T, win[:, qw:qw + vw], win[:, qw + vw:].T
    bqt, bk, bvt = bin_[:, :qw].reshape(qw, 1), bin_[:, qw:qw + vw], bin_[:, qw + vw:].reshape(vw, 1)
    grid_spec = pltpu.PrefetchScalarGridSpec(
        num_scalar_prefetch=1,
        grid=(bsz, seq // tb),
        in_specs=[
            pl.BlockSpec((1, tb, D_MODEL), lambda b, t, *_: (b, t, 0)),
            _const_spec((1, D_MODEL)),
            _const_spec((qw, D_MODEL)),
            _const_spec((qw, 1)),
            _const_spec((D_MODEL, vw)),
            _const_spec((1, vw)),
            _const_spec((vw, D_MODEL)),
            _const_spec((vw, 1)),
            _const_spec((D_MODEL, D_MODEL)),
            _const_spec((1, D_MODEL)),
            _const_spec((1, D_MODEL)),
            _const_spec((D_MODEL, 128)),
        ],
        out_specs=ospecs,
        scratch_shapes=[
            pltpu.VMEM((qw, tb), MXU),
            pltpu.VMEM((KV_HEADS, tb + WINDOW, HEAD_DIM), MXU),
            pltpu.VMEM((vw, tb + WINDOW), MXU),
            pltpu.VMEM((tb, D_MODEL), F32),
            pltpu.VMEM((CLASS_PAD, 128), F32),
        ],
    )
    return pl.pallas_call(
        _l1_kernel, grid_spec=grid_spec, out_shape=shapes, compiler_params=_params(),
        name="layer1_mixer",
    )(sinks, h, gmix, wqt, bqt, wk, bk, wvt, bvt, wout, bout, gffn, wrt)


def _dispatch(plan, xext):
    pos, _, _, nused, pad0, pad1, n_rows = plan
    n = xext.shape[0]
    blk = min(SCATTER_BLOCK, n)
    assert n % blk == 0
    grid_spec = pltpu.PrefetchScalarGridSpec(
        num_scalar_prefetch=4,
        grid=(n // blk,),
        in_specs=[pl.BlockSpec((blk, ROW_WORDS), lambda i, *_: (i, 0))],
        out_specs=pl.BlockSpec(memory_space=pl.ANY),
        scratch_shapes=[pltpu.VMEM((MOE_TILE, ROW_WORDS), F32), pltpu.SemaphoreType.DMA(()),
                        pltpu.SemaphoreType.DMA(())],
    )
    return pl.pallas_call(
        _scatter_kernel, grid_spec=grid_spec,
        out_shape=jax.ShapeDtypeStruct((n_rows, ROW_WORDS), F32),
        compiler_params=pltpu.CompilerParams(dimension_semantics=("arbitrary",),
                                             vmem_limit_bytes=VMEM_LIMIT),
        name="moe_dispatch",
    )(pos, pad0, pad1, nused, xext)


def _moe(plan, xs, layer, wg, wu, wd):
    _, e1, e2, nused, _, _, n_rows = plan
    tm = MOE_TILE
    n_tiles = n_rows // tm

    def wspec(shape, which):
        return pl.BlockSpec((None, None) + shape,
                            lambda i, e1, e2, *_: (layer, (e1, e2)[which][i], 0, 0))

    grid_spec = pltpu.PrefetchScalarGridSpec(
        num_scalar_prefetch=3,
        grid=(n_tiles,),
        in_specs=[
            pl.BlockSpec((tm, ROW_WORDS), lambda i, *_: (i, 0)),
            wspec((D_MODEL, D_EXPERT), 0), wspec((D_MODEL, D_EXPERT), 0), wspec((D_EXPERT, D_MODEL), 0),
            wspec((D_MODEL, D_EXPERT), 1), wspec((D_MODEL, D_EXPERT), 1), wspec((D_EXPERT, D_MODEL), 1),
        ],
        out_specs=pl.BlockSpec((tm, D_MODEL), lambda i, *_: (i, 0)),
    )
    return pl.pallas_call(
        _moe_kernel, grid_spec=grid_spec,
        out_shape=jax.ShapeDtypeStruct((n_rows, D_MODEL), F32),
        compiler_params=pltpu.CompilerParams(dimension_semantics=("arbitrary",),
                                             vmem_limit_bytes=VMEM_LIMIT),
        name="moe_experts",
    )(e1, e2, nused, xs, wg, wu, wd, wg, wu, wd)


def _combine_ple(pos, h1, p, layer, ys, gple, wgate, wproj, gfin, final):
    n = h1.shape[0]
    blk = MOVE_BLOCK
    assert n % blk == 0
    grid_spec = pltpu.PrefetchScalarGridSpec(
        num_scalar_prefetch=1,
        grid=(n // blk,),
        in_specs=[
            pl.BlockSpec((blk, D_MODEL), lambda i, *_: (i, 0)),
            pl.BlockSpec((None, blk, PLE_DIM), lambda i, *_: (layer, i, 0)),
            pl.BlockSpec(memory_space=pl.ANY),
            _const_spec((1, D_MODEL)),
            _const_spec((D_MODEL, D_MODEL)),
            _const_spec((PLE_DIM, D_MODEL)),
            _const_spec((1, D_MODEL)),
        ],
        out_specs=pl.BlockSpec((blk, D_MODEL), lambda i, *_: (i, 0)),
        scratch_shapes=[pltpu.VMEM((2, blk, D_MODEL), F32), pltpu.SemaphoreType.DMA((2,))],
    )
    return pl.pallas_call(
        functools.partial(_ple_kernel, final=final), grid_spec=grid_spec,
        out_shape=jax.ShapeDtypeStruct((n, D_MODEL), F32),
        compiler_params=pltpu.CompilerParams(dimension_semantics=("arbitrary",),
                                             vmem_limit_bytes=VMEM_LIMIT),
        name="combine_ple",
    )(pos, h1, p, ys, gple, wgate, wproj, gfin)


def _routing_plan(cls, rank, counts, n_tokens):
    tm = MOE_TILE
    n_tiles = n_tokens // tm + N_CLASS
    cnt = counts[:N_CLASS, 0].astype(I32)
    tiles_per_class = (cnt + tm - 1) // tm
    tile_end = jnp.cumsum(tiles_per_class)
    class_row0 = (tile_end - tiles_per_class) * tm
    pos = jnp.take(class_row0, cls.reshape(-1)) + rank.reshape(-1)
    nused = tile_end[-1]
    tidx = jnp.minimum(jnp.arange(n_tiles, dtype=I32), nused - 1)
    tcls = jnp.sum((tidx[:, None] >= tile_end[None, :]).astype(I32), axis=1)
    tcls = jnp.minimum(tcls, N_CLASS - 1)
    lo = jnp.array([a for a, _ in PAIRS], I32)
    hi = jnp.array([b for _, b in PAIRS], I32)
    grp = tcls // len(PAIRS)
    e1 = grp * PER_GROUP + jnp.take(lo, tcls % len(PAIRS))
    e2 = grp * PER_GROUP + jnp.take(hi, tcls % len(PAIRS))
    pad0 = class_row0 + cnt
    pad1 = tile_end * tm
    return (pos.astype(I32), e1.astype(I32), e2.astype(I32), nused.reshape(1).astype(I32),
            pad0.astype(I32), pad1.astype(I32), n_tiles * tm)


def _ffn_and_ple(h1, xext, cls, rank, counts, p, layer, wg, wu, wd, gple, wgate, wproj, gfin,
                 final):
    bsz, seq, _ = h1.shape
    n = bsz * seq
    plan = _routing_plan(cls, rank, counts, n)
    ys = _moe(plan, _dispatch(plan, xext), layer, wg, wu, wd)
    out = _combine_ple(plan[0], h1.reshape(n, D_MODEL), p.reshape(p.shape[0], n, PLE_DIM), layer,
                       ys, gple, wgate, wproj, gfin, final)
    return out.reshape(bsz, seq, D_MODEL)


def kernel(x, p, norm_mix, norm_ffn, norm_ple, norm_final, w_in_ab, hgrn_lb_logits, hgrn_out_norm, conv_w, w_out_ab, w_in_c, b_in_c, sinks, w_out_c, b_out_c, w_router, w_gate_e, w_up_e, w_down_e, w_ple_gate, w_ple_proj):
    depth = p.shape[0]
    lower_bounds = jnp.cumsum(jax.nn.softmax(hgrn_lb_logits.astype(F32), axis=0), axis=0)
    wr32 = w_router.astype(F32)
    wr_hi = wr32.astype(BF16)
    wr_lo = (wr32 - wr_hi.astype(F32)).astype(BF16)
    wrt = jnp.concatenate(
        [wr_hi, wr_lo, jnp.zeros((D_MODEL, 128 - 2 * N_EXPERTS), BF16)], axis=1)
    gfin = norm_final.reshape(1, D_MODEL)
    wg, wu, wd = w_gate_e.astype(MXU), w_up_e.astype(MXU), w_down_e.astype(MXU)
    h = x
    for i in range(depth):
        j = i // 2
        gmix = norm_mix[i].reshape(1, D_MODEL)
        gffn = norm_ffn[i].reshape(1, D_MODEL)
        if i % 2 == 0:
            h1, xext, cls, rank, counts = _layer0_mixer(
                h, gmix, w_in_ab[j].astype(MXU), lower_bounds[i].reshape(1, A_WIDTH),
                hgrn_out_norm[j].reshape(1, A_DK), conv_w[j], w_out_ab[j].astype(MXU), gffn, wrt)
        else:
            h1, xext, cls, rank, counts = _layer1_mixer(
                h, sinks[j].astype(F32), gmix, w_in_c[j].astype(MXU), b_in_c[j].reshape(1, C_IN),
                w_out_c[j].astype(MXU), b_out_c[j].reshape(1, D_MODEL), gffn, wrt)
        h = _ffn_and_ple(
            h1, xext, cls, rank, counts, p, i, wg, wu, wd,
            norm_ple[i].reshape(1, D_MODEL), w_ple_gate[i].astype(MXU), w_ple_proj[i].astype(MXU),
            gfin, final=(i == depth - 1))
    return h
```
